```python
import jax, jax.numpy as jnp
from jax import lax
import numpy as np

D_MODEL = 1024
BATCH = 8
SEQ = 2048
DEPTH = 2

GRID_W = 64
CTX_LEN = 256
N_MIXERS = 2
N_MOD = 6
NORM_EPS = 1e-6
CHUNK = 128
GM_WIDTH = 2 * D_MODEL
GM_GROUPS = 8
HEAD_DIM = 128
N_Q_HEADS = D_MODEL // HEAD_DIM
N_KV_HEADS = 2
GQA_GROUP = N_Q_HEADS // N_KV_HEADS
AXIS_DIM = HEAD_DIM // 2
ROPE_THETA = 10000.0
Q_BLOCK = 128
N_EXPERTS = 32
TOP_K = 4
D_EXPERT = D_MODEL
SWIGLU_LIMIT = 7.0
SWIGLU_ALPHA = 1.702
EXPERT_BLOCK = 128
N_A_LAYERS = (DEPTH + 1) // 2
N_B_LAYERS = DEPTH // 2

kernel_name = 'hybrid_gmlp_gqa_moe_diffusion_trunk'


def rmsnorm(x, gain):
    xf = x.astype(jnp.float32)
    y = xf * lax.rsqrt(jnp.mean(xf * xf, axis=-1, keepdims=True) + NORM_EPS)
    return (y * gain.astype(jnp.float32)).astype(x.dtype)


def modulate(h, shift, scale):
    return h * (1 + scale) + shift


def axial_rope_tables(rows):
    row = jnp.repeat(jnp.arange(rows, dtype=jnp.int32), GRID_W).astype(jnp.float32)
    col = jnp.tile(jnp.arange(GRID_W, dtype=jnp.int32), rows).astype(jnp.float32)
    inv_freq = 1.0 / (ROPE_THETA ** (jnp.arange(0, AXIS_DIM, 2, dtype=jnp.float32) / AXIS_DIM))
    ang = jnp.stack([row[:, None] * inv_freq, col[:, None] * inv_freq], axis=0)
    return jnp.cos(ang), jnp.sin(ang)


def apply_axial_rope(x, cos, sin):
    xr = x.reshape(*x.shape[:-1], 2, 2, AXIS_DIM // 2)
    x1, x2 = xr[..., 0, :], xr[..., 1, :]
    cs = jnp.transpose(cos, (1, 0, 2))[None, :, None].astype(x.dtype)
    sn = jnp.transpose(sin, (1, 0, 2))[None, :, None].astype(x.dtype)
    out = jnp.stack([x1 * cs - x2 * sn, x2 * cs + x1 * sn], axis=-2)
    return out.reshape(x.shape)


def chunk_gmlp(h, w_in, b_in, v_gain, w_s, b_s, w_out):
    B_, L, _ = h.shape
    uv = jax.nn.gelu(h @ w_in + b_in, approximate=False)
    u, v = jnp.split(uv, 2, axis=-1)
    v = rmsnorm(v, v_gain)
    v = v.reshape(B_, L // CHUNK, CHUNK, GM_GROUPS, GM_WIDTH // GM_GROUPS)
    s = jnp.einsum('gpq,bnqgc->bnpgc', w_s, v) + b_s.T[:, :, None]
    return (u * s.reshape(B_, L, GM_WIDTH)) @ w_out


def qkv_heads(h, w_qkv, q_gain, k_gain):
    B_, L, _ = h.shape
    qkv = h @ w_qkv
    q, k, v = jnp.split(qkv, [N_Q_HEADS * HEAD_DIM, (N_Q_HEADS + N_KV_HEADS) * HEAD_DIM], axis=-1)
    q = rmsnorm(q.reshape(B_, L, N_Q_HEADS, HEAD_DIM), q_gain)
    k = rmsnorm(k.reshape(B_, L, N_KV_HEADS, HEAD_DIM), k_gain)
    v = v.reshape(B_, L, N_KV_HEADS, HEAD_DIM)
    return q, k, v


def attend(q, k, v):
    B_, Lq = q.shape[:2]
    qg = q.reshape(B_, Lq, N_KV_HEADS, GQA_GROUP, HEAD_DIM)
    s = jnp.einsum('bqkgd,bskd->bkgqs', qg, k).astype(jnp.float32) * (HEAD_DIM ** -0.5)
    p = jax.nn.softmax(s, axis=-1).astype(v.dtype)
    o = jnp.einsum('bkgqs,bskd->bqkgd', p, v)
    return o.reshape(B_, Lq, N_Q_HEADS * HEAD_DIM)


def gqa_mixer(h_lat, h_ctx, w_qkv, q_gain, k_gain, w_o, cos, sin, with_ctx_out):
    B_, L, _ = h_lat.shape
    q_l, k_l, v_l = qkv_heads(h_lat, w_qkv, q_gain, k_gain)
    q_l = apply_axial_rope(q_l, cos, sin)
    k_l = apply_axial_rope(k_l, cos, sin)
    q_c, k_c, v_c = qkv_heads(h_ctx, w_qkv, q_gain, k_gain)
    k_all = jnp.concatenate([k_c, k_l], axis=1)
    v_all = jnp.concatenate([v_c, v_l], axis=1)
    nb = L // Q_BLOCK
    q_blocks = jnp.moveaxis(q_l.reshape(B_, nb, Q_BLOCK, N_Q_HEADS, HEAD_DIM), 1, 0)
    o = lax.map(lambda qb: attend(qb, k_all, v_all), q_blocks)
    o_lat = jnp.moveaxis(o, 0, 1).reshape(B_, L, N_Q_HEADS * HEAD_DIM) @ w_o
    if with_ctx_out:
        o_ctx = attend(q_c, k_c, v_c) @ w_o
        return o_lat, o_ctx
    return o_lat, None


def moe_ffn(h, router_w, router_b, w_gu, b_gu, w_down, b_down):
    T, D = h.shape
    logits = (h @ router_w + router_b).astype(jnp.float32)
    top_val, top_idx = lax.top_k(logits, TOP_K)
    gates = jax.nn.softmax(top_val, axis=-1)
    n_assign = T * TOP_K
    flat_e = top_idx.reshape(-1)
    order = jnp.argsort(flat_e, stable=True)
    sorted_e = flat_e[order]
    tok = (order // TOP_K).astype(jnp.int32)
    gate_sorted = gates.reshape(-1)[order]
    counts = jnp.bincount(flat_e, length=N_EXPERTS)
    padded = (counts + EXPERT_BLOCK - 1) // EXPERT_BLOCK * EXPERT_BLOCK
    pad_end = jnp.cumsum(padded)
    pad_start = pad_end - padded
    grp_start = jnp.cumsum(counts) - counts
    dest = pad_start[sorted_e] + jnp.arange(n_assign, dtype=jnp.int32) - grp_start[sorted_e]
    n_blocks = -(-n_assign // EXPERT_BLOCK) + N_EXPERTS
    cap = n_blocks * EXPERT_BLOCK
    buf_tok = jnp.full((cap,), T, jnp.int32).at[dest].set(tok)
    buf_gate = jnp.zeros((cap,), jnp.float32).at[dest].set(gate_sorted)
    block_expert = jnp.minimum(
        jnp.searchsorted(pad_end, jnp.arange(n_blocks, dtype=jnp.int32) * EXPERT_BLOCK, side='right'),
        N_EXPERTS - 1)
    h_pad = jnp.concatenate([h, jnp.zeros((1, D), h.dtype)], axis=0)
    xb = h_pad[buf_tok].reshape(n_blocks, EXPERT_BLOCK, D)

    def expert_block(args):
        xblk, e = args
        gu = xblk @ w_gu[e] + b_gu[e]
        g, u = jnp.split(gu, 2, axis=-1)
        g = jnp.minimum(g, SWIGLU_LIMIT)
        u = jnp.clip(u, -SWIGLU_LIMIT, SWIGLU_LIMIT)
        act = g * jax.nn.sigmoid(SWIGLU_ALPHA * g) * (u + 1)
        return act @ w_down[e] + b_down[e]

    yb = lax.map(expert_block, (xb, block_expert)).reshape(cap, D)
    y = jax.ops.segment_sum(yb * buf_gate[:, None].astype(yb.dtype), buf_tok, num_segments=T + 1)
    return y[:T]


def setup_inputs(seed: int = 0) -> dict:
    key = jax.random.key(seed)
    ks = iter(jax.random.split(key, 32))
    f32 = jnp.float32

    def nrm(shape, scale):
        return jax.random.normal(next(ks), shape, f32) * scale

    def gain(shape):
        return jnp.ones(shape, f32) + nrm(shape, 0.05)

    D = D_MODEL
    qkv_w = (N_Q_HEADS + 2 * N_KV_HEADS) * HEAD_DIM
    return {
        'x': nrm((BATCH, SEQ, D), 1.0),
        'c': nrm((BATCH, D), 1.0),
        'ctx': nrm((BATCH, CTX_LEN, D), 1.0),
        'c_ctx': nrm((D,), 1.0),
        'ada_w': nrm((DEPTH, D, N_MOD * D), 0.5 * D ** -0.5),
        'ada_b': nrm((DEPTH, N_MOD * D), 0.02),
        'norm_mix': gain((DEPTH, D)),
        'norm_ffn': gain((DEPTH, D)),
        'gm_w_in': nrm((N_A_LAYERS, D, 2 * GM_WIDTH), D ** -0.5),
        'gm_b_in': nrm((N_A_LAYERS, 2 * GM_WIDTH), 0.02),
        'gm_v_gain': gain((N_A_LAYERS, GM_WIDTH)),
        'gm_w_s': nrm((N_A_LAYERS, GM_GROUPS, CHUNK, CHUNK), CHUNK ** -0.5),
        'gm_b_s': gain((N_A_LAYERS, GM_GROUPS, CHUNK)),
        'gm_w_out': nrm((N_A_LAYERS, GM_WIDTH, D), GM_WIDTH ** -0.5),
        'at_w_qkv': nrm((N_B_LAYERS, D, qkv_w), D ** -0.5),
        'at_q_gain': gain((N_B_LAYERS, HEAD_DIM)),
        'at_k_gain': gain((N_B_LAYERS, HEAD_DIM)),
        'at_w_o': nrm((N_B_LAYERS, N_Q_HEADS * HEAD_DIM, D), (N_Q_HEADS * HEAD_DIM) ** -0.5),
        'moe_router_w': nrm((DEPTH, D, N_EXPERTS), D ** -0.5),
        'moe_router_b': nrm((DEPTH, N_EXPERTS), 0.01),
        'moe_w_gu': nrm((DEPTH, N_EXPERTS, D, 2 * D_EXPERT), D ** -0.5),
        'moe_b_gu': nrm((DEPTH, N_EXPERTS, 2 * D_EXPERT), 0.02),
        'moe_w_down': nrm((DEPTH, N_EXPERTS, D_EXPERT, D), D_EXPERT ** -0.5),
        'moe_b_down': nrm((DEPTH, N_EXPERTS, D), 0.02),
    }


def reference(x, c, ctx, c_ctx, ada_w, ada_b, norm_mix, norm_ffn,
              gm_w_in, gm_b_in, gm_v_gain, gm_w_s, gm_b_s, gm_w_out,
              at_w_qkv, at_q_gain, at_k_gain, at_w_o,
              moe_router_w, moe_router_b, moe_w_gu, moe_b_gu, moe_w_down, moe_b_down):
    B_, n_lat, D = x.shape
    n_ctx = ctx.shape[1]
    rows = n_lat // GRID_W
    cos, sin = axial_rope_tables(rows)
    silu_c = jax.nn.silu(c)
    silu_cc = jax.nn.silu(c_ctx)
    x_lat, x_ctx = x, ctx
    for i in range(DEPTH):
        last = i == DEPTH - 1
        mod_l = jnp.split(silu_c @ ada_w[i] + ada_b[i], N_MOD, axis=-1)
        mod_l = [m[:, None, :] for m in mod_l]
        mod_c = jnp.split(silu_cc @ ada_w[i] + ada_b[i], N_MOD, axis=-1)
        sh1, sc1, g1, sh2, sc2, g2 = mod_l
        csh1, csc1, cg1, csh2, csc2, cg2 = mod_c
        h_l = modulate(rmsnorm(x_lat, norm_mix[i]), sh1, sc1)
        h_c = modulate(rmsnorm(x_ctx, norm_mix[i]), csh1, csc1)
        if i % N_MIXERS == 0:
            a = i // N_MIXERS
            y_l = chunk_gmlp(h_l, gm_w_in[a], gm_b_in[a], gm_v_gain[a], gm_w_s[a], gm_b_s[a], gm_w_out[a])
            y_c = None if last else chunk_gmlp(h_c, gm_w_in[a], gm_b_in[a], gm_v_gain[a],
                                               gm_w_s[a], gm_b_s[a], gm_w_out[a])
        else:
            b = i // N_MIXERS
            y_l, y_c = gqa_mixer(h_l, h_c, at_w_qkv[b], at_q_gain[b], at_k_gain[b], at_w_o[b],
                                 cos, sin, not last)
        x_lat = x_lat + g1 * y_l
        if not last:
            x_ctx = x_ctx + cg1 * y_c
        f_l = modulate(rmsnorm(x_lat, norm_ffn[i]), sh2, sc2).reshape(B_ * n_lat, D)
        if last:
            tokens = f_l
        else:
            f_c = modulate(rmsnorm(x_ctx, norm_ffn[i]), csh2, csc2).reshape(B_ * n_ctx, D)
            tokens = jnp.concatenate([f_c, f_l], axis=0)
        y = moe_ffn(tokens, moe_router_w[i], moe_router_b[i], moe_w_gu[i], moe_b_gu[i],
                    moe_w_down[i], moe_b_down[i])
        if last:
            x_lat = x_lat + g2 * y.reshape(B_, n_lat, D)
        else:
            x_ctx = x_ctx + cg2 * y[:B_ * n_ctx].reshape(B_, n_ctx, D)
            x_lat = x_lat + g2 * y[B_ * n_ctx:].reshape(B_, n_lat, D)
    return x_lat
```

```python
import functools

import jax
import jax.numpy as jnp
from jax import lax
from jax.experimental import pallas as pl
from jax.experimental.pallas import tpu as pltpu

F32 = jnp.float32
BF16 = jnp.bfloat16
HIGHEST = lax.Precision.HIGHEST

GRID_W = 64
N_MOD = 6
NORM_EPS = 1e-6
CHUNK = 128
GM_GROUPS = 8
HEAD_DIM = 128
N_KV_HEADS = 2
AXIS_DIM = HEAD_DIM // 2
ROPE_THETA = 10000.0
TOP_K = 4
SWIGLU_LIMIT = 7.0
SWIGLU_ALPHA = 1.702

TM = 256
MOD_ROWS = 16
V7X_VMEM_LIMIT = 56 * 1024 * 1024


def _cparams(sem, vmem=V7X_VMEM_LIMIT):
    return pltpu.CompilerParams(dimension_semantics=sem, vmem_limit_bytes=vmem)


def _const_spec(shape):
    nd = len(shape)
    return pl.BlockSpec(shape, lambda *_: (0,) * nd, pipeline_mode=pl.Buffered(1))


def _sigmoid(x):
    return 1.0 / (1.0 + jnp.exp(-x))


def _rms(x):
    return x * lax.rsqrt(jnp.mean(x * x, axis=-1, keepdims=True) + NORM_EPS)


def _ada_kernel(s_ref, w_ref, b_ref, o_ref):
    s = s_ref[...]
    s = s * _sigmoid(s)
    o_ref[0] = jnp.dot(s, w_ref[0], precision=HIGHEST, preferred_element_type=F32) + b_ref[0]


def _ada_table(c, c_ctx, ada_w, ada_b):
    depth, d, n = ada_w.shape
    b = c.shape[0]
    s = jnp.concatenate([c, c_ctx[None, :], jnp.zeros((MOD_ROWS - b - 1, d), F32)], axis=0)
    tn = 1536
    return pl.pallas_call(
        _ada_kernel,
        grid=(depth, n // tn),
        in_specs=[
            pl.BlockSpec((MOD_ROWS, d), lambda i, j: (0, 0)),
            pl.BlockSpec((1, d, tn), lambda i, j: (i, 0, j)),
            pl.BlockSpec((1, 1, tn), lambda i, j: (i, 0, j)),
        ],
        out_specs=pl.BlockSpec((1, MOD_ROWS, tn), lambda i, j: (i, 0, j)),
        out_shape=jax.ShapeDtypeStruct((depth, MOD_ROWS, n), F32),
        compiler_params=_cparams(("arbitrary", "arbitrary")),
        name="ada_table",
    )(s, ada_w, ada_b.reshape(depth, 1, n))


def _mod_slices(mod_ref, row, d, first):
    return [mod_ref[pl.ds(row, 1), pl.ds((first + k) * d, d)] for k in range(3)]


def _router_epilogue(step, x_new, sh2, sc2, nf_ref, rwt_ref, rb_ref,
                     f_ref, idx_ref, gate_ref, rank_ref, cnt_ref, base_ref):
    tm = x_new.shape[0]
    n_exp = rwt_ref.shape[0]
    f = _rms(x_new) * nf_ref[...] * (1.0 + sc2) + sh2
    f_ref[...] = f

    logits = lax.dot_general(rwt_ref[...], f, (((1,), (1,)), ((), ())),
                             precision=HIGHEST, preferred_element_type=F32) + rb_ref[...]
    eid = lax.broadcasted_iota(jnp.int32, (n_exp, tm), 0).astype(F32)

    @pl.when(step == 0)
    def _():
        base_ref[...] = jnp.zeros_like(base_ref)

    r_io = lax.broadcasted_iota(jnp.int32, (tm, tm), 0)
    c_io = lax.broadcasted_iota(jnp.int32, (tm, tm), 1)
    before = jnp.where(r_io < c_io, 1.0, 0.0).astype(BF16)
    ones = jnp.ones((tm, tm), BF16)

    vals, idxs, ranks = [], [], []
    l = logits
    for _ in range(TOP_K):
        m = jnp.max(l, axis=0, keepdims=True)
        sel = jnp.min(jnp.where(l == m, eid, float(n_exp)), axis=0, keepdims=True)
        hit = eid == sel
        l = jnp.where(hit, -jnp.inf, l)
        oh = jnp.where(hit, 1.0, 0.0).astype(BF16)
        prefix = jnp.dot(oh, before, preferred_element_type=F32)
        base = base_ref[...]
        ranks.append(jnp.sum(jnp.where(hit, base + prefix, 0.0), axis=0, keepdims=True))
        base_ref[...] = base + jnp.dot(oh, ones, preferred_element_type=F32)
        vals.append(m)
        idxs.append(sel)
    es = [jnp.exp(v - vals[0]) for v in vals]
    tot = es[0] + es[1] + es[2] + es[3]
    gate_ref[...] = jnp.concatenate([e / tot for e in es], axis=0)
    idx_ref[...] = jnp.concatenate(idxs, axis=0).astype(jnp.int32)
    rank_ref[...] = jnp.concatenate(ranks, axis=0).astype(jnp.int32)
    cnt_ref[...] = base_ref[:, :128].astype(jnp.int32)


def _router_out(t, d, n_exp):
    shapes = (
        jax.ShapeDtypeStruct((t, d), F32),
        jax.ShapeDtypeStruct((TOP_K, t), jnp.int32),
        jax.ShapeDtypeStruct((TOP_K, t), F32),
        jax.ShapeDtypeStruct((TOP_K, t), jnp.int32),
        jax.ShapeDtypeStruct((n_exp, 128), jnp.int32),
    )
    return shapes


def _router_out_specs(tile_of, d, n_exp):
    return (
        pl.BlockSpec((TM, d), lambda *g: (tile_of(*g), 0)),
        pl.BlockSpec((TOP_K, TM), lambda *g: (0, tile_of(*g))),
        pl.BlockSpec((TOP_K, TM), lambda *g: (0, tile_of(*g))),
        pl.BlockSpec((TOP_K, TM), lambda *g: (0, tile_of(*g))),
        pl.BlockSpec((n_exp, 128), lambda *g: (0, 0)),
    )


def _gmlp_kernel(n_ctx_tiles, tiles_per_sample, n_samples,
                 x_ref, mod_ref, nm_ref, win_ref, bin_ref, vg_ref, ws_ref, bs_ref, wout_ref,
                 nf_ref, rwt_ref, rb_ref,
                 xo_ref, f_ref, idx_ref, gate_ref, rank_ref, cnt_ref, base_ref):
    i = pl.program_id(0)
    d = x_ref.shape[1]
    gw = wout_ref.shape[0]
    gc = gw // GM_GROUPS
    row = jnp.where(i < n_ctx_tiles, n_samples, (i - n_ctx_tiles) // tiles_per_sample)
    sh1, sc1, g1 = _mod_slices(mod_ref, row, d, 0)
    sh2, sc2, _ = _mod_slices(mod_ref, row, d, 3)

    x = x_ref[...]
    h = _rms(x) * nm_ref[...] * (1.0 + sc1) + sh1
    uv = jnp.dot(h.astype(BF16), win_ref[...], preferred_element_type=F32) + bin_ref[...]
    uv = 0.5 * uv * (1.0 + lax.erf(uv * (2.0 ** -0.5)))
    u = uv[:, :gw]
    v = (_rms(uv[:, gw:]) * vg_ref[...]).astype(BF16)
    parts = []
    for c in range(x.shape[0] // CHUNK):
        vc = v[c * CHUNK:(c + 1) * CHUNK, :]
        s = jnp.concatenate(
            [jnp.dot(ws_ref[g], vc[:, g * gc:(g + 1) * gc], preferred_element_type=F32)
             for g in range(GM_GROUPS)], axis=1) + bs_ref[...]
        parts.append((u[c * CHUNK:(c + 1) * CHUNK, :] * s).astype(BF16))
    z = jnp.concatenate(parts, axis=0)
    y = jnp.dot(z, wout_ref[...], preferred_element_type=F32)
    x_new = x + g1 * y
    xo_ref[...] = x_new
    _router_epilogue(i, x_new, sh2, sc2, nf_ref, rwt_ref, rb_ref,
                     f_ref, idx_ref, gate_ref, rank_ref, cnt_ref, base_ref)


def _gmlp_layer(x_all, mod, n_ctx_tiles, tiles_per_sample, n_samples,
                nm, w_in, b_in, v_gain, w_s, b_s, w_out, nf, rw, rb):
    t, d = x_all.shape
    gw = w_out.shape[0]
    n_exp = rw.shape[1]
    gc = gw // GM_GROUPS
    bs_full = jnp.repeat(b_s.T, gc, axis=1)
    tile = lambda i: i
    outs = pl.pallas_call(
        functools.partial(_gmlp_kernel, n_ctx_tiles, tiles_per_sample, n_samples),
        grid=(t // TM,),
        in_specs=[
            pl.BlockSpec((TM, d), lambda i: (i, 0)),
            _const_spec((MOD_ROWS, N_MOD * d)),
            _const_spec((1, d)),
            _const_spec((d, 2 * gw)),
            _const_spec((1, 2 * gw)),
            _const_spec((1, gw)),
            _const_spec((GM_GROUPS, CHUNK, CHUNK)),
            _const_spec((CHUNK, gw)),
            _const_spec((gw, d)),
            _const_spec((1, d)),
            _const_spec((n_exp, d)),
            _const_spec((n_exp, 1)),
        ],
        out_specs=(pl.BlockSpec((TM, d), lambda i: (i, 0)),) + _router_out_specs(tile, d, n_exp),
        out_shape=(jax.ShapeDtypeStruct((t, d), F32),) + _router_out(t, d, n_exp),
        scratch_shapes=[pltpu.VMEM((n_exp, TM), F32)],
        compiler_params=_cparams(("arbitrary",)),
        name="gmlp_mixer",
    )(x_all, mod, nm.reshape(1, d), w_in.astype(BF16), b_in.reshape(1, -1), v_gain.reshape(1, gw),
      w_s.astype(BF16), bs_full, w_out.astype(BF16), nf.reshape(1, d), rw.T, rb.reshape(n_exp, 1))
    return outs


def _qkv_kernel(n_ctx_tiles, tiles_per_sample, n_samples,
                x_ref, mod_ref, nm_ref, w_ref, qg_ref, kg_ref, cos_ref, sin_ref,
                q_ref, k_ref, v_ref):
    i = pl.program_id(0)
    d = x_ref.shape[1]
    nq = q_ref.shape[1]
    nkv = k_ref.shape[2]
    row = jnp.where(i < n_ctx_tiles, n_samples, (i - n_ctx_tiles) // tiles_per_sample)
    sh1, sc1, _ = _mod_slices(mod_ref, row, d, 0)
    h = _rms(x_ref[...]) * nm_ref[...] * (1.0 + sc1) + sh1
    qkv = jnp.dot(h.astype(BF16), w_ref[...], preferred_element_type=F32)
    cos = cos_ref[...]
    sin = sin_ref[...]
    half = AXIS_DIM // 2
    lane = lax.broadcasted_iota(jnp.int32, (x_ref.shape[0], HEAD_DIM), 1)
    first_half = (lane % AXIS_DIM) < half

    def head(xh, gain):
        xh = _rms(xh) * gain
        partner = jnp.where(first_half, pltpu.roll(xh, HEAD_DIM - half, 1), pltpu.roll(xh, half, 1))
        return xh * cos + partner * sin

    q = [head(qkv[:, j * HEAD_DIM:(j + 1) * HEAD_DIM], qg_ref[...]) for j in range(nq // HEAD_DIM)]
    k = [head(qkv[:, nq + j * HEAD_DIM:nq + (j + 1) * HEAD_DIM], kg_ref[...])
         for j in range(nkv // HEAD_DIM)]
    q_ref[...] = jnp.concatenate(q, axis=1).astype(BF16)
    k_ref[0] = jnp.concatenate(k, axis=1).astype(BF16)
    v_ref[0] = qkv[:, nq + nkv:].astype(BF16)


def _rope_tables(n_lat):
    rows = n_lat // GRID_W
    row = jnp.repeat(jnp.arange(rows, dtype=jnp.int32), GRID_W).astype(F32)
    col = jnp.tile(jnp.arange(GRID_W, dtype=jnp.int32), rows).astype(F32)
    inv_freq = 1.0 / (ROPE_THETA ** (jnp.arange(0, AXIS_DIM, 2, dtype=F32) / AXIS_DIM))
    ang_r = row[:, None] * inv_freq
    ang_c = col[:, None] * inv_freq
    cos = jnp.concatenate([jnp.cos(ang_r)] * 2 + [jnp.cos(ang_c)] * 2, axis=1)
    sin = jnp.concatenate([-jnp.sin(ang_r), jnp.sin(ang_r), -jnp.sin(ang_c), jnp.sin(ang_c)], axis=1)
    cos = jnp.concatenate([jnp.ones((TM, HEAD_DIM), F32), cos], axis=0)
    sin = jnp.concatenate([jnp.zeros((TM, HEAD_DIM), F32), sin], axis=0)
    return cos, sin


def _qkv_layer(x_all, mod, n_ctx_tiles, tiles_per_sample, n_samples, n_ctx, n_lat,
               nm, w_qkv, q_gain, k_gain):
    t, d = x_all.shape
    nqkv = w_qkv.shape[1]
    nkv = N_KV_HEADS * HEAD_DIM
    nq = nqkv - 2 * nkv
    cos, sin = _rope_tables(n_lat)
    ctx_blocks = n_ctx // TM
    tps = tiles_per_sample

    def pos_block(i):
        return jnp.where(i < n_ctx_tiles, 0, 1 + (i - n_ctx_tiles) % tps)

    def kv_map(i):
        lat = i >= n_ctx_tiles
        b = jnp.where(lat, (i - n_ctx_tiles) // tps, i // ctx_blocks)
        j = jnp.where(lat, ctx_blocks + (i - n_ctx_tiles) % tps, i % ctx_blocks)
        return (b, j, 0)

    return pl.pallas_call(
        functools.partial(_qkv_kernel, n_ctx_tiles, tiles_per_sample, n_samples),
        grid=(t // TM,),
        in_specs=[
            pl.BlockSpec((TM, d), lambda i: (i, 0)),
            _const_spec((MOD_ROWS, N_MOD * d)),
            _const_spec((1, d)),
            _const_spec((d, nqkv)),
            _const_spec((1, HEAD_DIM)),
            _const_spec((1, HEAD_DIM)),
            pl.BlockSpec((TM, HEAD_DIM), lambda i: (pos_block(i), 0)),
            pl.BlockSpec((TM, HEAD_DIM), lambda i: (pos_block(i), 0)),
        ],
        out_specs=(
            pl.BlockSpec((TM, nq), lambda i: (i, 0)),
            pl.BlockSpec((1, TM, nkv), kv_map),
            pl.BlockSpec((1, TM, nkv), kv_map),
        ),
        out_shape=(
            jax.ShapeDtypeStruct((t, nq), BF16),
            jax.ShapeDtypeStruct((n_samples, n_ctx + n_lat, nkv), BF16),
            jax.ShapeDtypeStruct((n_samples, n_ctx + n_lat, nkv), BF16),
        ),
        compiler_params=_cparams(("arbitrary",)),
        name="qkv_rope",
    )(x_all, mod, nm.reshape(1, d), w_qkv.astype(BF16), q_gain.reshape(1, HEAD_DIM),
      k_gain.reshape(1, HEAD_DIM), cos, sin)


def _attn_kernel(tiles_per_sample,
                 x_ref, q_ref, k_ref, v_ref, mod_ref, wo_ref, nf_ref, rwt_ref, rb_ref,
                 xo_ref, f_ref, idx_ref, gate_ref, rank_ref, cnt_ref, base_ref):
    b = pl.program_id(0)
    j = pl.program_id(1)
    d = x_ref.shape[1]
    n_heads = q_ref.shape[1] // HEAD_DIM
    group = n_heads // N_KV_HEADS
    _, _, g1 = _mod_slices(mod_ref, b, d, 0)
    sh2, sc2, _ = _mod_slices(mod_ref, b, d, 3)
    scale = HEAD_DIM ** -0.5
    outs = []
    for hd in range(n_heads):
        g = hd // group
        qh = q_ref[:, hd * HEAD_DIM:(hd + 1) * HEAD_DIM]
        kg = k_ref[0, :, g * HEAD_DIM:(g + 1) * HEAD_DIM]
        vg = v_ref[0, :, g * HEAD_DIM:(g + 1) * HEAD_DIM]
        s = lax.dot_general(qh, kg, (((1,), (1,)), ((), ())), preferred_element_type=F32) * scale
        p = jnp.exp(s - jnp.max(s, axis=-1, keepdims=True))
        den = jnp.sum(p, axis=-1, keepdims=True)
        o = jnp.dot(p.astype(BF16), vg, preferred_element_type=F32) / den
        outs.append(o.astype(BF16))
    o_all = jnp.concatenate(outs, axis=1)
    y = jnp.dot(o_all, wo_ref[...], preferred_element_type=F32)
    x_new = x_ref[...] + g1 * y
    xo_ref[...] = x_new
    _router_epilogue(b * tiles_per_sample + j, x_new, sh2, sc2, nf_ref, rwt_ref, rb_ref,
                     f_ref, idx_ref, gate_ref, rank_ref, cnt_ref, base_ref)


def _attn_layer(x_all, q, k_all, v_all, mod, n_ctx_tiles, tiles_per_sample, n_samples,
                w_o, nf, rw, rb):
    d = x_all.shape[1]
    t_lat = n_samples * tiles_per_sample * TM
    n_exp = rw.shape[1]
    nq = q.shape[1]
    lk, nkv = k_all.shape[1], k_all.shape[2]
    tps = tiles_per_sample
    lat_tile = lambda b, j: n_ctx_tiles + b * tps + j
    tile = lambda b, j: b * tps + j
    return pl.pallas_call(
        functools.partial(_attn_kernel, tiles_per_sample),
        grid=(n_samples, tps),
        in_specs=[
            pl.BlockSpec((TM, d), lambda b, j: (lat_tile(b, j), 0)),
            pl.BlockSpec((TM, nq), lambda b, j: (lat_tile(b, j), 0)),
            pl.BlockSpec((1, lk, nkv), lambda b, j: (b, 0, 0)),
            pl.BlockSpec((1, lk, nkv), lambda b, j: (b, 0, 0)),
            _const_spec((MOD_ROWS, N_MOD * d)),
            _const_spec((nq, d)),
            _const_spec((1, d)),
            _const_spec((n_exp, d)),
            _const_spec((n_exp, 1)),
        ],
        out_specs=(pl.BlockSpec((TM, d), lambda b, j: (tile(b, j), 0)),) + _router_out_specs(tile, d, n_exp),
        out_shape=(jax.ShapeDtypeStruct((t_lat, d), F32),) + _router_out(t_lat, d, n_exp),
        scratch_shapes=[pltpu.VMEM((n_exp, TM), F32)],
        compiler_params=_cparams(("arbitrary", "arbitrary")),
        name="attn_mixer",
    )(x_all, q, k_all, v_all, mod, w_o.astype(BF16), nf.reshape(1, d), rw.T, rb.reshape(n_exp, 1))


def _dispatch_kernel(dest_ref, f_hbm, xs_hbm, sem):
    i = pl.program_id(0)
    tm = dest_ref.shape[1]

    def copy(t, k):
        return pltpu.make_async_copy(f_hbm.at[pl.ds(i * tm + t, 1)],
                                     xs_hbm.at[pl.ds(dest_ref[k, t], 1)], sem)

    def start(t, c):
        for k in range(TOP_K):
            copy(t, k).start()
        return c

    def wait(t, c):
        for k in range(TOP_K):
            copy(t, k).wait()
        return c

    lax.fori_loop(0, tm, start, 0)
    lax.fori_loop(0, tm, wait, 0)


def _dispatch(f, dest):
    t, d = f.shape
    return pl.pallas_call(
        _dispatch_kernel,
        grid=(t // TM,),
        in_specs=[
            pl.BlockSpec((TOP_K, TM), lambda i: (0, i), memory_space=pltpu.SMEM),
            pl.BlockSpec(memory_space=pl.ANY),
        ],
        out_specs=pl.BlockSpec(memory_space=pl.ANY),
        out_shape=jax.ShapeDtypeStruct((t * TOP_K, d), F32),
        scratch_shapes=[pltpu.SemaphoreType.DMA],
        compiler_params=_cparams(("arbitrary",)),
        name="moe_dispatch",
    )(dest, f)


def _expert_kernel(blk_ref, exp_ref, lo_ref, hi_ref,
                   xs_ref, wgu_ref, bgu_ref, wd_ref, bd_ref, o_ref, wgu_b, wd_b):
    w = pl.program_id(0)
    tm, d = xs_ref.shape
    de = wd_ref.shape[1]
    prev = jnp.maximum(w - 1, 0)
    lo = lo_ref[w]
    hi = hi_ref[w]
    blk = blk_ref[w]
    new_expert = (w == 0) | (exp_ref[w] != exp_ref[prev])
    first_visit = (w == 0) | (blk != blk_ref[prev])

    @pl.when(new_expert)
    def _():
        wgu_b[...] = wgu_ref[0].astype(BF16)
        wd_b[...] = wd_ref[0].astype(BF16)

    @pl.when(first_visit & (hi <= lo))
    def _():
        o_ref[...] = jnp.zeros_like(o_ref)

    @pl.when(hi > lo)
    def _():
        gu = jnp.dot(xs_ref[...].astype(BF16), wgu_b[...], preferred_element_type=F32) + bgu_ref[0]
        g = jnp.minimum(gu[:, :de], SWIGLU_LIMIT)
        u = jnp.clip(gu[:, de:], -SWIGLU_LIMIT, SWIGLU_LIMIT)
        act = g * _sigmoid(SWIGLU_ALPHA * g) * (u + 1.0)
        y = jnp.dot(act.astype(BF16), wd_b[...], preferred_element_type=F32) + bd_ref[0]
        rows = blk * tm + lax.broadcasted_iota(jnp.int32, (tm, d), 0)
        mine = (rows >= lo) & (rows < hi)

        @pl.when(first_visit)
        def _():
            o_ref[...] = jnp.where(mine, y, 0.0)

        @pl.when(jnp.logical_not(first_visit))
        def _():
            o_ref[...] = jnp.where(mine, y, o_ref[...])


def _experts(xs, work, w_gu, b_gu, w_down, b_down):
    n, d = xs.shape
    n_exp, _, de2 = w_gu.shape
    de = w_down.shape[1]
    n_work = work[0].shape[0]
    grid_spec = pltpu.PrefetchScalarGridSpec(
        num_scalar_prefetch=4,
        grid=(n_work,),
        in_specs=[
            pl.BlockSpec((TM, d), lambda w, blk, ex, lo, hi: (blk[w], 0)),
            pl.BlockSpec((1, d, de2), lambda w, blk, ex, lo, hi: (ex[w], 0, 0)),
            pl.BlockSpec((1, 1, de2), lambda w, blk, ex, lo, hi: (ex[w], 0, 0)),
            pl.BlockSpec((1, de, d), lambda w, blk, ex, lo, hi: (ex[w], 0, 0)),
            pl.BlockSpec((1, 1, d), lambda w, blk, ex, lo, hi: (ex[w], 0, 0)),
        ],
        out_specs=pl.BlockSpec((TM, d), lambda w, blk, ex, lo, hi: (blk[w], 0)),
        scratch_shapes=[pltpu.VMEM((d, de2), BF16), pltpu.VMEM((de, d), BF16)],
    )
    return pl.pallas_call(
        _expert_kernel,
        grid_spec=grid_spec,
        out_shape=jax.ShapeDtypeStruct((n, d), F32),
        compiler_params=_cparams(("arbitrary",)),
        name="moe_experts",
    )(*work, xs, w_gu, b_gu.reshape(n_exp, 1, de2), w_down, b_down.reshape(n_exp, 1, d))


def _combine_kernel(row_of_tile, dest_ref, gate_ref, x_ref, mod_ref, yb_hbm, o_ref, buf, sem):
    i = pl.program_id(0)
    tm, d = x_ref.shape

    def copy(t, k):
        return pltpu.make_async_copy(yb_hbm.at[pl.ds(dest_ref[k, t], 1)],
                                     buf.at[k, pl.ds(t, 1)], sem)

    def start(t, c):
        for k in range(TOP_K):
            copy(t, k).start()
        return c

    def wait(t, c):
        for k in range(TOP_K):
            copy(t, k).wait()
        return c

    lax.fori_loop(0, tm, start, 0)
    lax.fori_loop(0, tm, wait, 0)
    g2 = mod_ref[pl.ds(row_of_tile(i), 1), pl.ds(5 * d, d)]
    gates = gate_ref[...]
    y = buf[0] * gates[:, 0:1]
    for k in range(1, TOP_K):
        y = y + buf[k] * gates[:, k:k + 1]
    o_ref[...] = x_ref[...] + g2 * y


def _combine(x_new, yb, dest, gates_t, mod, row_of_tile):
    t, d = x_new.shape
    return pl.pallas_call(
        functools.partial(_combine_kernel, row_of_tile),
        grid=(t // TM,),
        in_specs=[
            pl.BlockSpec((TOP_K, TM), lambda i: (0, i), memory_space=pltpu.SMEM),
            pl.BlockSpec((TM, TOP_K), lambda i: (i, 0)),
            pl.BlockSpec((TM, d), lambda i: (i, 0)),
            _const_spec((MOD_ROWS, N_MOD * d)),
            pl.BlockSpec(memory_space=pl.ANY),
        ],
        out_specs=pl.BlockSpec((TM, d), lambda i: (i, 0)),
        out_shape=jax.ShapeDtypeStruct((t, d), F32),
        scratch_shapes=[pltpu.VMEM((TOP_K, TM, d), F32), pltpu.SemaphoreType.DMA],
        compiler_params=_cparams(("arbitrary",)),
        name="moe_combine",
    )(dest, gates_t, x_new, mod, yb)


def _work_items(counts, n_assign):
    n_exp = counts.shape[0]
    off_end = jnp.cumsum(counts)
    n_blocks = n_assign // TM
    bounds = jnp.sort(jnp.concatenate([jnp.arange(n_blocks, dtype=jnp.int32) * TM,
                                       off_end[:-1].astype(jnp.int32)]))
    lo = bounds
    hi = jnp.concatenate([bounds[1:], jnp.full((1,), n_assign, jnp.int32)])
    blk = jnp.minimum(lo // TM, n_blocks - 1)
    ex = jnp.minimum(jnp.searchsorted(off_end, lo, side="right"), n_exp - 1).astype(jnp.int32)
    return blk, ex, lo, hi


def _moe(x_new, f, idx, gates, rank, cnt, mod, row_of_tile, w_gu, b_gu, w_down, b_down):
    t = f.shape[0]
    counts = cnt[:, 0]
    off_start = jnp.cumsum(counts) - counts
    dest = jnp.take(off_start, idx) + rank
    xs = _dispatch(f, dest)
    yb = _experts(xs, _work_items(counts, t * TOP_K), w_gu, b_gu, w_down, b_down)
    return _combine(x_new, yb, dest, gates.T, mod, row_of_tile)


def kernel(x, c, ctx, c_ctx, ada_w, ada_b, norm_mix, norm_ffn, gm_w_in, gm_b_in, gm_v_gain, gm_w_s,
           gm_b_s, gm_w_out, at_w_qkv, at_q_gain, at_k_gain, at_w_o, moe_router_w, moe_router_b,
           moe_w_gu, moe_b_gu, moe_w_down, moe_b_down):
    n_samples, n_lat, d = x.shape
    n_ctx = ctx.shape[1]
    assert n_lat % TM == 0 and n_ctx % TM == 0 and TM % CHUNK == 0 and n_samples < MOD_ROWS
    n_ctx_tiles = n_samples * n_ctx // TM
    tps = n_lat // TM

    mods = _ada_table(c, c_ctx, ada_w, ada_b)
    x_all = jnp.concatenate([ctx.reshape(-1, d), x.reshape(-1, d)], axis=0)

    def row_all(i):
        return jnp.where(i < n_ctx_tiles, n_samples, (i - n_ctx_tiles) // tps)

    x_new, f, idx, gates, rank, cnt = _gmlp_layer(
        x_all, mods[0], n_ctx_tiles, tps, n_samples, norm_mix[0], gm_w_in[0], gm_b_in[0],
        gm_v_gain[0], gm_w_s[0], gm_b_s[0], gm_w_out[0], norm_ffn[0], moe_router_w[0],
        moe_router_b[0])
    x_all = _moe(x_new, f, idx, gates, rank, cnt, mods[0], row_all,
                 moe_w_gu[0], moe_b_gu[0], moe_w_down[0], moe_b_down[0])

    q, k_all, v_all = _qkv_layer(x_all, mods[1], n_ctx_tiles, tps, n_samples, n_ctx, n_lat,
                                 norm_mix[1], at_w_qkv[0], at_q_gain[0], at_k_gain[0])
    x_new, f, idx, gates, rank, cnt = _attn_layer(
        x_all, q, k_all, v_all, mods[1], n_ctx_tiles, tps, n_samples, at_w_o[0], norm_ffn[1],
        moe_router_w[1], moe_router_b[1])
    out = _moe(x_new, f, idx, gates, rank, cnt, mods[1], lambda i: i // tps,
               moe_w_gu[1], moe_b_gu[1], moe_w_down[1], moe_b_down[1])
    return out.reshape(n_samples, n_lat, d)
```

```python
import functools

import jax
import jax.numpy as jnp
from jax import lax
from jax.experimental import pallas as pl
from jax.experimental.pallas import tpu as pltpu

F32 = jnp.float32
BF16 = jnp.bfloat16
HIGHEST = lax.Precision.HIGHEST

GRID_W = 64
N_MOD = 6
NORM_EPS = 1e-6
CHUNK = 128
GM_GROUPS = 8
HEAD_DIM = 128
N_KV_HEADS = 2
AXIS_DIM = HEAD_DIM // 2
ROPE_THETA = 10000.0
TOP_K = 4
SWIGLU_LIMIT = 7.0
SWIGLU_ALPHA = 1.702

TM = 256
MOD_ROWS = 16
V7X_VMEM_LIMIT = 56 * 1024 * 1024


def _cparams(sem, vmem=V7X_VMEM_LIMIT):
    return pltpu.CompilerParams(dimension_semantics=sem, vmem_limit_bytes=vmem)


def _const_spec(shape):
    nd = len(shape)
    return pl.BlockSpec(shape, lambda *_: (0,) * nd, pipeline_mode=pl.Buffered(1))


def _sigmoid(x):
    return 1.0 / (1.0 + jnp.exp(-x))


def _rms(x):
    return x * lax.rsqrt(jnp.mean(x * x, axis=-1, keepdims=True) + NORM_EPS)


def _ada_kernel(s_ref, w_ref, b_ref, o_ref):
    s = s_ref[...]
    s = s * _sigmoid(s)
    o_ref[0] = jnp.dot(s, w_ref[0], precision=HIGHEST, preferred_element_type=F32) + b_ref[0]


def _ada_table(c, c_ctx, ada_w, ada_b):
    depth, d, n = ada_w.shape
    b = c.shape[0]
    s = jnp.concatenate([c, c_ctx[None, :], jnp.zeros((MOD_ROWS - b - 1, d), F32)], axis=0)
    tn = 1536
    return pl.pallas_call(
        _ada_kernel,
        grid=(depth, n // tn),
        in_specs=[
            pl.BlockSpec((MOD_ROWS, d), lambda i, j: (0, 0)),
            pl.BlockSpec((1, d, tn), lambda i, j: (i, 0, j)),
            pl.BlockSpec((1, 1, tn), lambda i, j: (i, 0, j)),
        ],
        out_specs=pl.BlockSpec((1, MOD_ROWS, tn), lambda i, j: (i, 0, j)),
        out_shape=jax.ShapeDtypeStruct((depth, MOD_ROWS, n), F32),
        compiler_params=_cparams(("arbitrary", "arbitrary")),
        name="ada_table",
    )(s, ada_w, ada_b.reshape(depth, 1, n))


def _mod_slices(mod_ref, row, d, first):
    return [mod_ref[pl.ds(row, 1), pl.ds((first + k) * d, d)] for k in range(3)]


def _router_epilogue(step, x_new, sh2, sc2, nf_ref, rwt_ref, rb_ref,
                     f_ref, idx_ref, gate_ref, rank_ref, cnt_ref, base_ref):
    tm = x_new.shape[0]
    n_exp = rwt_ref.shape[0]
    f = _rms(x_new) * nf_ref[...] * (1.0 + sc2) + sh2
    f_ref[...] = f

    logits = lax.dot_general(rwt_ref[...], f, (((1,), (1,)), ((), ())),
                             precision=HIGHEST, preferred_element_type=F32) + rb_ref[...]
    eid = lax.broadcasted_iota(jnp.int32, (n_exp, tm), 0).astype(F32)

    @pl.when(step == 0)
    def _():
        base_ref[...] = jnp.zeros_like(base_ref)

    r_io = lax.broadcasted_iota(jnp.int32, (tm, tm), 0)
    c_io = lax.broadcasted_iota(jnp.int32, (tm, tm), 1)
    before = jnp.where(r_io < c_io, 1.0, 0.0).astype(BF16)
    ones = jnp.ones((tm, tm), BF16)

    vals, idxs, ranks = [], [], []
    l = logits
    for _ in range(TOP_K):
        m = jnp.max(l, axis=0, keepdims=True)
        sel = jnp.min(jnp.where(l == m, eid, float(n_exp)), axis=0, keepdims=True)
        hit = eid == sel
        l = jnp.where(hit, -jnp.inf, l)
        oh = jnp.where(hit, 1.0, 0.0).astype(BF16)
        prefix = jnp.dot(oh, before, preferred_element_type=F32)
        base = base_ref[...]
        ranks.append(jnp.sum(jnp.where(hit, base + prefix, 0.0), axis=0, keepdims=True))
        base_ref[...] = base + jnp.dot(oh, ones, preferred_element_type=F32)
        vals.append(m)
        idxs.append(sel)
    es = [jnp.exp(v - vals[0]) for v in vals]
    tot = es[0] + es[1] + es[2] + es[3]
    zero = jnp.zeros_like(tot)
    gate_ref[...] = jnp.concatenate([e / tot for e in es] + [zero] * (8 - TOP_K), axis=0)
    idx_ref[...] = jnp.concatenate(idxs, axis=0).astype(jnp.int32)
    rank_ref[...] = jnp.concatenate(ranks, axis=0).astype(jnp.int32)
    cnt_ref[...] = base_ref[:, :128].astype(jnp.int32)


def _router_out(t, d, n_exp):
    shapes = (
        jax.ShapeDtypeStruct((t, d), F32),
        jax.ShapeDtypeStruct((TOP_K, t), jnp.int32),
        jax.ShapeDtypeStruct((8, t), F32),
        jax.ShapeDtypeStruct((TOP_K, t), jnp.int32),
        jax.ShapeDtypeStruct((n_exp, 128), jnp.int32),
    )
    return shapes


def _router_out_specs(tile_of, d, n_exp):
    return (
        pl.BlockSpec((TM, d), lambda *g: (tile_of(*g), 0)),
        pl.BlockSpec((TOP_K, TM), lambda *g: (0, tile_of(*g))),
        pl.BlockSpec((8, TM), lambda *g: (0, tile_of(*g))),
        pl.BlockSpec((TOP_K, TM), lambda *g: (0, tile_of(*g))),
        pl.BlockSpec((n_exp, 128), lambda *g: (0, 0)),
    )


def _gmlp_kernel(n_ctx_tiles, tiles_per_sample, n_samples,
                 x_ref, mod_ref, nm_ref, win_ref, bin_ref, vg_ref, ws_ref, bs_ref, wout_ref,
                 nf_ref, rwt_ref, rb_ref,
                 xo_ref, f_ref, idx_ref, gate_ref, rank_ref, cnt_ref, base_ref):
    i = pl.program_id(0)
    d = x_ref.shape[1]
    gw = wout_ref.shape[0]
    gc = gw // GM_GROUPS
    row = jnp.where(i < n_ctx_tiles, n_samples, (i - n_ctx_tiles) // tiles_per_sample)
    sh1, sc1, g1 = _mod_slices(mod_ref, row, d, 0)
    sh2, sc2, _ = _mod_slices(mod_ref, row, d, 3)

    x = x_ref[...]
    h = _rms(x) * nm_ref[...] * (1.0 + sc1) + sh1
    uv = jnp.dot(h.astype(BF16), win_ref[...], preferred_element_type=F32) + bin_ref[...]
    uv = 0.5 * uv * (1.0 + lax.erf(uv * (2.0 ** -0.5)))
    u = uv[:, :gw]
    v = (_rms(uv[:, gw:]) * vg_ref[...]).astype(BF16)
    parts = []
    for c in range(x.shape[0] // CHUNK):
        vc = v[c * CHUNK:(c + 1) * CHUNK, :]
        s = jnp.concatenate(
            [jnp.dot(ws_ref[g], vc[:, g * gc:(g + 1) * gc], preferred_element_type=F32)
             for g in range(GM_GROUPS)], axis=1) + bs_ref[...]
        parts.append((u[c * CHUNK:(c + 1) * CHUNK, :] * s).astype(BF16))
    z = jnp.concatenate(parts, axis=0)
    y = jnp.dot(z, wout_ref[...], preferred_element_type=F32)
    x_new = x + g1 * y
    xo_ref[...] = x_new
    _router_epilogue(i, x_new, sh2, sc2, nf_ref, rwt_ref, rb_ref,
                     f_ref, idx_ref, gate_ref, rank_ref, cnt_ref, base_ref)


def _gmlp_layer(x_all, mod, n_ctx_tiles, tiles_per_sample, n_samples,
                nm, w_in, b_in, v_gain, w_s, b_s, w_out, nf, rw, rb):
    t, d = x_all.shape
    gw = w_out.shape[0]
    n_exp = rw.shape[1]
    gc = gw // GM_GROUPS
    bs_full = jnp.repeat(b_s.T, gc, axis=1)
    tile = lambda i: i
    outs = pl.pallas_call(
        functools.partial(_gmlp_kernel, n_ctx_tiles, tiles_per_sample, n_samples),
        grid=(t // TM,),
        in_specs=[
            pl.BlockSpec((TM, d), lambda i: (i, 0)),
            _const_spec((MOD_ROWS, N_MOD * d)),
            _const_spec((1, d)),
            _const_spec((d, 2 * gw)),
            _const_spec((1, 2 * gw)),
            _const_spec((1, gw)),
            _const_spec((GM_GROUPS, CHUNK, CHUNK)),
            _const_spec((CHUNK, gw)),
            _const_spec((gw, d)),
            _const_spec((1, d)),
            _const_spec((n_exp, d)),
            _const_spec((n_exp, 1)),
        ],
        out_specs=(pl.BlockSpec((TM, d), lambda i: (i, 0)),) + _router_out_specs(tile, d, n_exp),
        out_shape=(jax.ShapeDtypeStruct((t, d), F32),) + _router_out(t, d, n_exp),
        scratch_shapes=[pltpu.VMEM((n_exp, TM), F32)],
        compiler_params=_cparams(("arbitrary",)),
        name="gmlp_mixer",
    )(x_all, mod, nm.reshape(1, d), w_in.astype(BF16), b_in.reshape(1, -1), v_gain.reshape(1, gw),
      w_s.astype(BF16), bs_full, w_out.astype(BF16), nf.reshape(1, d), rw.T, rb.reshape(n_exp, 1))
    return outs


def _qkv_kernel(n_ctx_tiles, tiles_per_sample, n_samples,
                x_ref, mod_ref, nm_ref, w_ref, qg_ref, kg_ref, cos_ref, sin_ref,
                q_ref, k_ref, v_ref):
    i = pl.program_id(0)
    d = x_ref.shape[1]
    nq = q_ref.shape[1]
    nkv = k_ref.shape[2]
    row = jnp.where(i < n_ctx_tiles, n_samples, (i - n_ctx_tiles) // tiles_per_sample)
    sh1, sc1, _ = _mod_slices(mod_ref, row, d, 0)
    h = _rms(x_ref[...]) * nm_ref[...] * (1.0 + sc1) + sh1
    qkv = jnp.dot(h.astype(BF16), w_ref[...], preferred_element_type=F32)
    cos = cos_ref[...]
    sin = sin_ref[...]
    half = AXIS_DIM // 2
    lane = lax.broadcasted_iota(jnp.int32, (x_ref.shape[0], HEAD_DIM), 1)
    first_half = (lane % AXIS_DIM) < half

    def head(xh, gain):
        xh = _rms(xh) * gain
        partner = jnp.where(first_half, pltpu.roll(xh, HEAD_DIM - half, 1), pltpu.roll(xh, half, 1))
        return xh * cos + partner * sin

    q = [head(qkv[:, j * HEAD_DIM:(j + 1) * HEAD_DIM], qg_ref[...]) for j in range(nq // HEAD_DIM)]
    k = [head(qkv[:, nq + j * HEAD_DIM:nq + (j + 1) * HEAD_DIM], kg_ref[...])
         for j in range(nkv // HEAD_DIM)]
    q_ref[...] = jnp.concatenate(q, axis=1).astype(BF16)
    k_ref[0] = jnp.concatenate(k, axis=1).astype(BF16)
    v_ref[0] = qkv[:, nq + nkv:].astype(BF16)


def _rope_tables(n_lat):
    rows = n_lat // GRID_W
    row = jnp.repeat(jnp.arange(rows, dtype=jnp.int32), GRID_W).astype(F32)
    col = jnp.tile(jnp.arange(GRID_W, dtype=jnp.int32), rows).astype(F32)
    inv_freq = 1.0 / (ROPE_THETA ** (jnp.arange(0, AXIS_DIM, 2, dtype=F32) / AXIS_DIM))
    ang_r = row[:, None] * inv_freq
    ang_c = col[:, None] * inv_freq
    cos = jnp.concatenate([jnp.cos(ang_r)] * 2 + [jnp.cos(ang_c)] * 2, axis=1)
    sin = jnp.concatenate([-jnp.sin(ang_r), jnp.sin(ang_r), -jnp.sin(ang_c), jnp.sin(ang_c)], axis=1)
    cos = jnp.concatenate([jnp.ones((TM, HEAD_DIM), F32), cos], axis=0)
    sin = jnp.concatenate([jnp.zeros((TM, HEAD_DIM), F32), sin], axis=0)
    return cos, sin


def _qkv_layer(x_all, mod, n_ctx_tiles, tiles_per_sample, n_samples, n_ctx, n_lat,
               nm, w_qkv, q_gain, k_gain):
    t, d = x_all.shape
    nqkv = w_qkv.shape[1]
    nkv = N_KV_HEADS * HEAD_DIM
    nq = nqkv - 2 * nkv
    cos, sin = _rope_tables(n_lat)
    ctx_blocks = n_ctx // TM
    tps = tiles_per_sample

    def pos_block(i):
        return jnp.where(i < n_ctx_tiles, 0, 1 + (i - n_ctx_tiles) % tps)

    def kv_map(i):
        lat = i >= n_ctx_tiles
        b = jnp.where(lat, (i - n_ctx_tiles) // tps, i // ctx_blocks)
        j = jnp.where(lat, ctx_blocks + (i - n_ctx_tiles) % tps, i % ctx_blocks)
        return (b, j, 0)

    return pl.pallas_call(
        functools.partial(_qkv_kernel, n_ctx_tiles, tiles_per_sample, n_samples),
        grid=(t // TM,),
        in_specs=[
            pl.BlockSpec((TM, d), lambda i: (i, 0)),
            _const_spec((MOD_ROWS, N_MOD * d)),
            _const_spec((1, d)),
            _const_spec((d, nqkv)),
            _const_spec((1, HEAD_DIM)),
            _const_spec((1, HEAD_DIM)),
            pl.BlockSpec((TM, HEAD_DIM), lambda i: (pos_block(i), 0)),
            pl.BlockSpec((TM, HEAD_DIM), lambda i: (pos_block(i), 0)),
        ],
        out_specs=(
            pl.BlockSpec((TM, nq), lambda i: (i, 0)),
            pl.BlockSpec((1, TM, nkv), kv_map),
            pl.BlockSpec((1, TM, nkv), kv_map),
        ),
        out_shape=(
            jax.ShapeDtypeStruct((t, nq), BF16),
            jax.ShapeDtypeStruct((n_samples, n_ctx + n_lat, nkv), BF16),
            jax.ShapeDtypeStruct((n_samples, n_ctx + n_lat, nkv), BF16),
        ),
        compiler_params=_cparams(("arbitrary",)),
        name="qkv_rope",
    )(x_all, mod, nm.reshape(1, d), w_qkv.astype(BF16), q_gain.reshape(1, HEAD_DIM),
      k_gain.reshape(1, HEAD_DIM), cos, sin)


def _attn_kernel(tiles_per_sample,
                 x_ref, q_ref, k_ref, v_ref, mod_ref, wo_ref, nf_ref, rwt_ref, rb_ref,
                 xo_ref, f_ref, idx_ref, gate_ref, rank_ref, cnt_ref, base_ref):
    b = pl.program_id(0)
    j = pl.program_id(1)
    d = x_ref.shape[1]
    n_heads = q_ref.shape[1] // HEAD_DIM
    group = n_heads // N_KV_HEADS
    _, _, g1 = _mod_slices(mod_ref, b, d, 0)
    sh2, sc2, _ = _mod_slices(mod_ref, b, d, 3)
    scale = HEAD_DIM ** -0.5
    outs = []
    for hd in range(n_heads):
        g = hd // group
        qh = q_ref[:, hd * HEAD_DIM:(hd + 1) * HEAD_DIM]
        kg = k_ref[0, :, g * HEAD_DIM:(g + 1) * HEAD_DIM]
        vg = v_ref[0, :, g * HEAD_DIM:(g + 1) * HEAD_DIM]
        s = lax.dot_general(qh, kg, (((1,), (1,)), ((), ())), preferred_element_type=F32) * scale
        p = jnp.exp(s - jnp.max(s, axis=-1, keepdims=True))
        den = jnp.sum(p, axis=-1, keepdims=True)
        o = jnp.dot(p.astype(BF16), vg, preferred_element_type=F32) / den
        outs.append(o.astype(BF16))
    o_all = jnp.concatenate(outs, axis=1)
    y = jnp.dot(o_all, wo_ref[...], preferred_element_type=F32)
    x_new = x_ref[...] + g1 * y
    xo_ref[...] = x_new
    _router_epilogue(b * tiles_per_sample + j, x_new, sh2, sc2, nf_ref, rwt_ref, rb_ref,
                     f_ref, idx_ref, gate_ref, rank_ref, cnt_ref, base_ref)


def _attn_layer(x_all, q, k_all, v_all, mod, n_ctx_tiles, tiles_per_sample, n_samples,
                w_o, nf, rw, rb):
    d = x_all.shape[1]
    t_lat = n_samples * tiles_per_sample * TM
    n_exp = rw.shape[1]
    nq = q.shape[1]
    lk, nkv = k_all.shape[1], k_all.shape[2]
    tps = tiles_per_sample
    lat_tile = lambda b, j: n_ctx_tiles + b * tps + j
    tile = lambda b, j: b * tps + j
    return pl.pallas_call(
        functools.partial(_attn_kernel, tiles_per_sample),
        grid=(n_samples, tps),
        in_specs=[
            pl.BlockSpec((TM, d), lambda b, j: (lat_tile(b, j), 0)),
            pl.BlockSpec((TM, nq), lambda b, j: (lat_tile(b, j), 0)),
            pl.BlockSpec((1, lk, nkv), lambda b, j: (b, 0, 0)),
            pl.BlockSpec((1, lk, nkv), lambda b, j: (b, 0, 0)),
            _const_spec((MOD_ROWS, N_MOD * d)),
            _const_spec((nq, d)),
            _const_spec((1, d)),
            _const_spec((n_exp, d)),
            _const_spec((n_exp, 1)),
        ],
        out_specs=(pl.BlockSpec((TM, d), lambda b, j: (tile(b, j), 0)),) + _router_out_specs(tile, d, n_exp),
        out_shape=(jax.ShapeDtypeStruct((t_lat, d), F32),) + _router_out(t_lat, d, n_exp),
        scratch_shapes=[pltpu.VMEM((n_exp, TM), F32)],
        compiler_params=_cparams(("arbitrary", "arbitrary")),
        name="attn_mixer",
    )(x_all, q, k_all, v_all, mod, w_o.astype(BF16), nf.reshape(1, d), rw.T, rb.reshape(n_exp, 1))


def _dispatch_kernel(off_ref, idx_ref, rank_ref, f_ref, xs_hbm, dest_ref, sem):
    tm = f_ref.shape[0]

    def start(t, c):
        for k in range(TOP_K):
            dst = off_ref[idx_ref[k, t]] + rank_ref[k, t]
            dest_ref[k, t] = dst
            pltpu.make_async_copy(f_ref.at[pl.ds(t, 1)], xs_hbm.at[pl.ds(dst, 1)], sem).start()
        return c

    lax.fori_loop(0, tm, start, 0)
    for k in range(TOP_K):
        pltpu.make_async_copy(f_ref, xs_hbm.at[pl.ds(0, tm)], sem).wait()


def _dispatch(f, idx, rank, off_start):
    t, d = f.shape
    smem_tile = pl.BlockSpec((TOP_K, TM), lambda i, off: (0, i), memory_space=pltpu.SMEM)
    grid_spec = pltpu.PrefetchScalarGridSpec(
        num_scalar_prefetch=1,
        grid=(t // TM,),
        in_specs=[smem_tile, smem_tile, pl.BlockSpec((TM, d), lambda i, off: (i, 0))],
        out_specs=(pl.BlockSpec(memory_space=pl.ANY), smem_tile),
        scratch_shapes=[pltpu.SemaphoreType.DMA],
    )
    return pl.pallas_call(
        _dispatch_kernel,
        grid_spec=grid_spec,
        out_shape=(jax.ShapeDtypeStruct((t * TOP_K, d), F32),
                   jax.ShapeDtypeStruct((TOP_K, t), jnp.int32)),
        compiler_params=_cparams(("arbitrary",)),
        name="moe_dispatch",
    )(off_start, idx, rank, f)


def _expert_kernel(blk_ref, exp_ref, lo_ref, hi_ref,
                   xs_ref, wgu_ref, bgu_ref, wd_ref, bd_ref, o_ref, wgu_b, wd_b):
    w = pl.program_id(0)
    tm, d = xs_ref.shape
    de = wd_ref.shape[2]
    prev = jnp.maximum(w - 1, 0)
    lo = lo_ref[w]
    hi = hi_ref[w]
    blk = blk_ref[w]
    new_expert = (w == 0) | (exp_ref[w] != exp_ref[prev])
    first_visit = (w == 0) | (blk != blk_ref[prev])

    @pl.when(new_expert)
    def _():
        wgu_b[...] = wgu_ref[0, 0].astype(BF16)
        wd_b[...] = wd_ref[0, 0].astype(BF16)

    @pl.when(first_visit & (hi <= lo))
    def _():
        o_ref[...] = jnp.zeros_like(o_ref)

    @pl.when(hi > lo)
    def _():
        gu = jnp.dot(xs_ref[...].astype(BF16), wgu_b[...], preferred_element_type=F32) + bgu_ref[0]
        g = jnp.minimum(gu[:, :de], SWIGLU_LIMIT)
        u = jnp.clip(gu[:, de:], -SWIGLU_LIMIT, SWIGLU_LIMIT)
        act = g * _sigmoid(SWIGLU_ALPHA * g) * (u + 1.0)
        y = jnp.dot(act.astype(BF16), wd_b[...], preferred_element_type=F32) + bd_ref[0]
        rows = blk * tm + lax.broadcasted_iota(jnp.int32, (tm, d), 0)
        mine = (rows >= lo) & (rows < hi)

        @pl.when(first_visit)
        def _():
            o_ref[...] = jnp.where(mine, y, 0.0)

        @pl.when(jnp.logical_not(first_visit))
        def _():
            o_ref[...] = jnp.where(mine, y, o_ref[...])


def _experts(xs, work, layer, w_gu, b_gu, w_down, b_down):
    n, d = xs.shape
    _, n_exp, _, de2 = w_gu.shape
    de = w_down.shape[2]
    n_work = work[0].shape[0]
    grid_spec = pltpu.PrefetchScalarGridSpec(
        num_scalar_prefetch=4,
        grid=(n_work,),
        in_specs=[
            pl.BlockSpec((TM, d), lambda w, blk, ex, lo, hi: (blk[w], 0)),
            pl.BlockSpec((1, 1, d, de2), lambda w, blk, ex, lo, hi: (layer, ex[w], 0, 0)),
            pl.BlockSpec((1, 1, de2), lambda w, blk, ex, lo, hi: (ex[w], 0, 0)),
            pl.BlockSpec((1, 1, de, d), lambda w, blk, ex, lo, hi: (layer, ex[w], 0, 0)),
            pl.BlockSpec((1, 1, d), lambda w, blk, ex, lo, hi: (ex[w], 0, 0)),
        ],
        out_specs=pl.BlockSpec((TM, d), lambda w, blk, ex, lo, hi: (blk[w], 0)),
        scratch_shapes=[pltpu.VMEM((d, de2), BF16), pltpu.VMEM((de, d), BF16)],
    )
    return pl.pallas_call(
        _expert_kernel,
        grid_spec=grid_spec,
        out_shape=jax.ShapeDtypeStruct((n, d), F32),
        compiler_params=_cparams(("arbitrary",)),
        name="moe_experts",
    )(*work, xs, w_gu, b_gu[layer].reshape(n_exp, 1, de2), w_down, b_down[layer].reshape(n_exp, 1, d))


def _combine_kernel(row_of_tile, dest_ref, gate_ref, x_ref, mod_ref, yb_hbm, o_ref, buf, sem):
    i = pl.program_id(0)
    tm, d = x_ref.shape

    def start(t, c):
        for k in range(TOP_K):
            pltpu.make_async_copy(yb_hbm.at[pl.ds(dest_ref[k, t], 1)],
                                  buf.at[k, pl.ds(t, 1)], sem).start()
        return c

    lax.fori_loop(0, tm, start, 0)
    g2 = mod_ref[pl.ds(row_of_tile(i), 1), pl.ds(5 * d, d)]
    gates = jnp.concatenate([gate_ref[...], jnp.zeros((HEAD_DIM - 8, tm), F32)], axis=0).T
    for k in range(TOP_K):
        pltpu.make_async_copy(yb_hbm.at[pl.ds(0, tm)], buf.at[k], sem).wait()
    y = buf[0] * gates[:, 0:1]
    for k in range(1, TOP_K):
        y = y + buf[k] * gates[:, k:k + 1]
    o_ref[...] = x_ref[...] + g2 * y


def _combine(x_new, yb, dest, gates, mod, row_of_tile):
    t, d = x_new.shape
    return pl.pallas_call(
        functools.partial(_combine_kernel, row_of_tile),
        grid=(t // TM,),
        in_specs=[
            pl.BlockSpec((TOP_K, TM), lambda i: (0, i), memory_space=pltpu.SMEM),
            pl.BlockSpec((8, TM), lambda i: (0, i)),
            pl.BlockSpec((TM, d), lambda i: (i, 0)),
            _const_spec((MOD_ROWS, N_MOD * d)),
            pl.BlockSpec(memory_space=pl.ANY),
        ],
        out_specs=pl.BlockSpec((TM, d), lambda i: (i, 0)),
        out_shape=jax.ShapeDtypeStruct((t, d), F32),
        scratch_shapes=[pltpu.VMEM((TOP_K, TM, d), F32), pltpu.SemaphoreType.DMA],
        compiler_params=_cparams(("arbitrary",)),
        name="moe_combine",
    )(dest, gates, x_new, mod, yb)


def _work_items(counts, n_assign):
    n_exp = counts.shape[0]
    n_blocks = n_assign // TM
    n_work = n_blocks + n_exp - 1
    off_end = jnp.cumsum(counts)
    off_start = off_end - counts
    first_blk = off_start // TM
    n_items = jnp.where(counts > 0, (off_end - 1) // TM - first_blk + 1, 0)
    item_end = jnp.cumsum(n_items)
    item_start = item_end - n_items
    w = jnp.arange(n_work, dtype=jnp.int32)
    ex = jnp.minimum(jnp.sum(w[:, None] >= item_end[None, :], axis=1), n_exp - 1).astype(jnp.int32)
    used = w < item_end[-1]
    blk = jnp.where(used, first_blk[ex] + w - item_start[ex], n_blocks - 1).astype(jnp.int32)
    lo = jnp.where(used, jnp.maximum(off_start[ex], blk * TM), 0).astype(jnp.int32)
    hi = jnp.where(used, jnp.minimum(off_end[ex], (blk + 1) * TM), 0).astype(jnp.int32)
    return (blk, ex, lo, hi), off_start.astype(jnp.int32)


def _moe(x_new, f, idx, gates, rank, cnt, mod, row_of_tile, layer, w_gu, b_gu, w_down, b_down):
    t = f.shape[0]
    work, off_start = _work_items(cnt[:, 0], t * TOP_K)
    xs, dest = _dispatch(f, idx, rank, off_start)
    yb = _experts(xs, work, layer, w_gu, b_gu, w_down, b_down)
    return _combine(x_new, yb, dest, gates, mod, row_of_tile)


def kernel(x, c, ctx, c_ctx, ada_w, ada_b, norm_mix, norm_ffn, gm_w_in, gm_b_in, gm_v_gain, gm_w_s,
           gm_b_s, gm_w_out, at_w_qkv, at_q_gain, at_k_gain, at_w_o, moe_router_w, moe_router_b,
           moe_w_gu, moe_b_gu, moe_w_down, moe_b_down):
    n_samples, n_lat, d = x.shape
    n_ctx = ctx.shape[1]
    assert n_lat % TM == 0 and n_ctx % TM == 0 and TM % CHUNK == 0 and n_samples < MOD_ROWS
    n_ctx_tiles = n_samples * n_ctx // TM
    tps = n_lat // TM

    mods = _ada_table(c, c_ctx, ada_w, ada_b)
    x_all = jnp.concatenate([ctx.reshape(-1, d), x.reshape(-1, d)], axis=0)

    def row_all(i):
        return jnp.where(i < n_ctx_tiles, n_samples, (i - n_ctx_tiles) // tps)

    x_new, f, idx, gates, rank, cnt = _gmlp_layer(
        x_all, mods[0], n_ctx_tiles, tps, n_samples, norm_mix[0], gm_w_in[0], gm_b_in[0],
        gm_v_gain[0], gm_w_s[0], gm_b_s[0], gm_w_out[0], norm_ffn[0], moe_router_w[0],
        moe_router_b[0])
    x_all = _moe(x_new, f, idx, gates, rank, cnt, mods[0], row_all,
                 0, moe_w_gu, moe_b_gu, moe_w_down, moe_b_down)

    q, k_all, v_all = _qkv_layer(x_all, mods[1], n_ctx_tiles, tps, n_samples, n_ctx, n_lat,
                                 norm_mix[1], at_w_qkv[0], at_q_gain[0], at_k_gain[0])
    x_new, f, idx, gates, rank, cnt = _attn_layer(
        x_all, q, k_all, v_all, mods[1], n_ctx_tiles, tps, n_samples, at_w_o[0], norm_ffn[1],
        moe_router_w[1], moe_router_b[1])
    out = _moe(x_new, f, idx, gates, rank, cnt, mods[1], lambda i: i // tps,
               1, moe_w_gu, moe_b_gu, moe_w_down, moe_b_down)
    return out.reshape(n_samples, n_lat, d)
```

```python
import functools

import jax
import jax.numpy as jnp
from jax import lax
from jax.experimental import pallas as pl
from jax.experimental.pallas import tpu as pltpu

F32 = jnp.float32
BF16 = jnp.bfloat16
HIGHEST = lax.Precision.HIGHEST

GRID_W = 64
N_MOD = 6
NORM_EPS = 1e-6
CHUNK = 128
GM_GROUPS = 8
HEAD_DIM = 128
N_KV_HEADS = 2
AXIS_DIM = HEAD_DIM // 2
ROPE_THETA = 10000.0
TOP_K = 4
SWIGLU_LIMIT = 7.0
SWIGLU_ALPHA = 1.702

TM = 256
TG = 512
TE = 512
MOD_ROWS = 16
V7X_VMEM_LIMIT = 56 * 1024 * 1024


def _cparams(sem, vmem=V7X_VMEM_LIMIT):
    return pltpu.CompilerParams(dimension_semantics=sem, vmem_limit_bytes=vmem)


def _const_spec(shape):
    nd = len(shape)
    return pl.BlockSpec(shape, lambda *_: (0,) * nd, pipeline_mode=pl.Buffered(1))


def _sigmoid(x):
    return 1.0 / (1.0 + jnp.exp(-x))


def _rms(x):
    return x * lax.rsqrt(jnp.mean(x * x, axis=-1, keepdims=True) + NORM_EPS)


def _ada_kernel(s_ref, w_ref, b_ref, o_ref):
    s = s_ref[...]
    s = s * _sigmoid(s)
    o_ref[0] = jnp.dot(s, w_ref[0], precision=HIGHEST, preferred_element_type=F32) + b_ref[0]


def _ada_table(c, c_ctx, ada_w, ada_b):
    depth, d, n = ada_w.shape
    b = c.shape[0]
    s = jnp.concatenate([c, c_ctx[None, :], jnp.zeros((MOD_ROWS - b - 1, d), F32)], axis=0)
    tn = 1536
    return pl.pallas_call(
        _ada_kernel,
        grid=(depth, n // tn),
        in_specs=[
            pl.BlockSpec((MOD_ROWS, d), lambda i, j: (0, 0)),
            pl.BlockSpec((1, d, tn), lambda i, j: (i, 0, j)),
            pl.BlockSpec((1, 1, tn), lambda i, j: (i, 0, j)),
        ],
        out_specs=pl.BlockSpec((1, MOD_ROWS, tn), lambda i, j: (i, 0, j)),
        out_shape=jax.ShapeDtypeStruct((depth, MOD_ROWS, n), F32),
        compiler_params=_cparams(("arbitrary", "arbitrary")),
        name="ada_table",
    )(s, ada_w, ada_b.reshape(depth, 1, n))


def _mod_slices(mod_ref, row, d, first):
    return [mod_ref[pl.ds(row, 1), pl.ds((first + k) * d, d)] for k in range(3)]


def _router_epilogue(step, x_new, sh2, sc2, nf_ref, rwt_ref, rb_ref,
                     f_ref, idx_ref, gate_ref, rank_ref, cnt_ref, base_ref):
    tm = x_new.shape[0]
    n_exp = rwt_ref.shape[0]
    f = _rms(x_new) * nf_ref[...] * (1.0 + sc2) + sh2
    f_ref[...] = f

    logits = lax.dot_general(rwt_ref[...], f, (((1,), (1,)), ((), ())),
                             precision=HIGHEST, preferred_element_type=F32) + rb_ref[...]
    eid = lax.broadcasted_iota(jnp.int32, (n_exp, tm), 0).astype(F32)

    @pl.when(step == 0)
    def _():
        base_ref[...] = jnp.zeros_like(base_ref)

    r_io = lax.broadcasted_iota(jnp.int32, (tm, tm), 0)
    c_io = lax.broadcasted_iota(jnp.int32, (tm, tm), 1)
    before = jnp.where(r_io < c_io, 1.0, 0.0).astype(BF16)
    ones = jnp.ones((tm, tm), BF16)

    vals, idxs, ranks = [], [], []
    l = logits
    for _ in range(TOP_K):
        m = jnp.max(l, axis=0, keepdims=True)
        sel = jnp.min(jnp.where(l == m, eid, float(n_exp)), axis=0, keepdims=True)
        hit = eid == sel
        l = jnp.where(hit, -jnp.inf, l)
        oh = jnp.where(hit, 1.0, 0.0).astype(BF16)
        prefix = jnp.dot(oh, before, preferred_element_type=F32)
        base = base_ref[...]
        ranks.append(jnp.sum(jnp.where(hit, base + prefix, 0.0), axis=0, keepdims=True))
        base_ref[...] = base + jnp.dot(oh, ones, preferred_element_type=F32)
        vals.append(m)
        idxs.append(sel)
    es = [jnp.exp(v - vals[0]) for v in vals]
    tot = es[0] + es[1] + es[2] + es[3]
    zero = jnp.zeros_like(tot)
    gate_ref[...] = jnp.concatenate([e / tot for e in es] + [zero] * (8 - TOP_K), axis=0)
    idx_ref[...] = jnp.concatenate(idxs, axis=0).astype(jnp.int32)
    rank_ref[...] = jnp.concatenate(ranks, axis=0).astype(jnp.int32)
    cnt_ref[...] = base_ref[:, :128].astype(jnp.int32)


def _router_out(t, d, n_exp):
    shapes = (
        jax.ShapeDtypeStruct((t, d), F32),
        jax.ShapeDtypeStruct((TOP_K, t), jnp.int32),
        jax.ShapeDtypeStruct((8, t), F32),
        jax.ShapeDtypeStruct((TOP_K, t), jnp.int32),
        jax.ShapeDtypeStruct((n_exp, 128), jnp.int32),
    )
    return shapes


def _router_out_specs(tile_of, tm, d, n_exp):
    return (
        pl.BlockSpec((tm, d), lambda *g: (tile_of(*g), 0)),
        pl.BlockSpec((TOP_K, tm), lambda *g: (0, tile_of(*g))),
        pl.BlockSpec((8, tm), lambda *g: (0, tile_of(*g))),
        pl.BlockSpec((TOP_K, tm), lambda *g: (0, tile_of(*g))),
        pl.BlockSpec((n_exp, 128), lambda *g: (0, 0)),
    )


def _gmlp_kernel(n_ctx_tiles, tiles_per_sample, n_samples,
                 x_ref, mod_ref, nm_ref, win_ref, bin_ref, vg_ref, ws_ref, bs_ref, wout_ref,
                 nf_ref, rwt_ref, rb_ref,
                 xo_ref, f_ref, idx_ref, gate_ref, rank_ref, cnt_ref, base_ref):
    i = pl.program_id(0)
    d = x_ref.shape[1]
    gw = wout_ref.shape[0]
    gc = gw // GM_GROUPS
    row = jnp.where(i < n_ctx_tiles, n_samples, (i - n_ctx_tiles) // tiles_per_sample)
    sh1, sc1, g1 = _mod_slices(mod_ref, row, d, 0)
    sh2, sc2, _ = _mod_slices(mod_ref, row, d, 3)

    x = x_ref[...]
    h = _rms(x) * nm_ref[...] * (1.0 + sc1) + sh1
    uv = jnp.dot(h.astype(BF16), win_ref[...], preferred_element_type=F32) + bin_ref[...]
    uv = 0.5 * uv * (1.0 + lax.erf(uv * (2.0 ** -0.5)))
    u = uv[:, :gw]
    v = (_rms(uv[:, gw:]) * vg_ref[...]).astype(BF16)
    parts = []
    for c in range(x.shape[0] // CHUNK):
        vc = v[c * CHUNK:(c + 1) * CHUNK, :]
        s = jnp.concatenate(
            [jnp.dot(ws_ref[g], vc[:, g * gc:(g + 1) * gc], preferred_element_type=F32)
             for g in range(GM_GROUPS)], axis=1) + bs_ref[...]
        parts.append((u[c * CHUNK:(c + 1) * CHUNK, :] * s).astype(BF16))
    z = jnp.concatenate(parts, axis=0)
    y = jnp.dot(z, wout_ref[...], preferred_element_type=F32)
    x_new = x + g1 * y
    xo_ref[...] = x_new
    _router_epilogue(i, x_new, sh2, sc2, nf_ref, rwt_ref, rb_ref,
                     f_ref, idx_ref, gate_ref, rank_ref, cnt_ref, base_ref)


def _gmlp_layer(x_all, mod, n_ctx_tiles, tiles_per_sample, n_samples,
                nm, w_in, b_in, v_gain, w_s, b_s, w_out, nf, rw, rb):
    t, d = x_all.shape
    gw = w_out.shape[0]
    n_exp = rw.shape[1]
    gc = gw // GM_GROUPS
    bs_full = jnp.repeat(b_s.T, gc, axis=1)
    tile = lambda i: i
    outs = pl.pallas_call(
        functools.partial(_gmlp_kernel, n_ctx_tiles, tiles_per_sample, n_samples),
        grid=(t // TG,),
        in_specs=[
            pl.BlockSpec((TG, d), lambda i: (i, 0)),
            _const_spec((MOD_ROWS, N_MOD * d)),
            _const_spec((1, d)),
            _const_spec((d, 2 * gw)),
            _const_spec((1, 2 * gw)),
            _const_spec((1, gw)),
            _const_spec((GM_GROUPS, CHUNK, CHUNK)),
            _const_spec((CHUNK, gw)),
            _const_spec((gw, d)),
            _const_spec((1, d)),
            _const_spec((n_exp, d)),
            _const_spec((n_exp, 1)),
        ],
        out_specs=(pl.BlockSpec((TG, d), lambda i: (i, 0)),) + _router_out_specs(tile, TG, d, n_exp),
        out_shape=(jax.ShapeDtypeStruct((t, d), F32),) + _router_out(t, d, n_exp),
        scratch_shapes=[pltpu.VMEM((n_exp, TG), F32)],
        compiler_params=_cparams(("arbitrary",)),
        name="gmlp_mixer",
    )(x_all, mod, nm.reshape(1, d), w_in.astype(BF16), b_in.reshape(1, -1), v_gain.reshape(1, gw),
      w_s.astype(BF16), bs_full, w_out.astype(BF16), nf.reshape(1, d), rw.T, rb.reshape(n_exp, 1))
    return outs


def _qkv_kernel(n_ctx_tiles, tiles_per_sample, n_samples,
                x_ref, mod_ref, nm_ref, w_ref, qg_ref, kg_ref, cos_ref, sin_ref,
                q_ref, k_ref, v_ref):
    i = pl.program_id(0)
    d = x_ref.shape[1]
    nq = q_ref.shape[1]
    nkv = k_ref.shape[2]
    row = jnp.where(i < n_ctx_tiles, n_samples, (i - n_ctx_tiles) // tiles_per_sample)
    sh1, sc1, _ = _mod_slices(mod_ref, row, d, 0)
    h = _rms(x_ref[...]) * nm_ref[...] * (1.0 + sc1) + sh1
    qkv = jnp.dot(h.astype(BF16), w_ref[...], preferred_element_type=F32)
    cos = cos_ref[...]
    sin = sin_ref[...]
    half = AXIS_DIM // 2
    lane = lax.broadcasted_iota(jnp.int32, (x_ref.shape[0], HEAD_DIM), 1)
    first_half = (lane % AXIS_DIM) < half

    def head(xh, gain):
        xh = _rms(xh) * gain
        partner = jnp.where(first_half, pltpu.roll(xh, HEAD_DIM - half, 1), pltpu.roll(xh, half, 1))
        return xh * cos + partner * sin

    q = [head(qkv[:, j * HEAD_DIM:(j + 1) * HEAD_DIM], qg_ref[...]) for j in range(nq // HEAD_DIM)]
    k = [head(qkv[:, nq + j * HEAD_DIM:nq + (j + 1) * HEAD_DIM], kg_ref[...])
         for j in range(nkv // HEAD_DIM)]
    q_ref[...] = jnp.concatenate(q, axis=1).astype(BF16)
    k_ref[0] = jnp.concatenate(k, axis=1).astype(BF16)
    v_ref[0] = qkv[:, nq + nkv:].astype(BF16)


def _rope_tables(n_lat):
    rows = n_lat // GRID_W
    row = jnp.repeat(jnp.arange(rows, dtype=jnp.int32), GRID_W).astype(F32)
    col = jnp.tile(jnp.arange(GRID_W, dtype=jnp.int32), rows).astype(F32)
    inv_freq = 1.0 / (ROPE_THETA ** (jnp.arange(0, AXIS_DIM, 2, dtype=F32) / AXIS_DIM))
    ang_r = row[:, None] * inv_freq
    ang_c = col[:, None] * inv_freq
    cos = jnp.concatenate([jnp.cos(ang_r)] * 2 + [jnp.cos(ang_c)] * 2, axis=1)
    sin = jnp.concatenate([-jnp.sin(ang_r), jnp.sin(ang_r), -jnp.sin(ang_c), jnp.sin(ang_c)], axis=1)
    cos = jnp.concatenate([jnp.ones((TM, HEAD_DIM), F32), cos], axis=0)
    sin = jnp.concatenate([jnp.zeros((TM, HEAD_DIM), F32), sin], axis=0)
    return cos, sin


def _qkv_layer(x_all, mod, n_ctx_tiles, tiles_per_sample, n_samples, n_ctx, n_lat,
               nm, w_qkv, q_gain, k_gain):
    t, d = x_all.shape
    nqkv = w_qkv.shape[1]
    nkv = N_KV_HEADS * HEAD_DIM
    nq = nqkv - 2 * nkv
    cos, sin = _rope_tables(n_lat)
    ctx_blocks = n_ctx // TM
    tps = tiles_per_sample

    def pos_block(i):
        return jnp.where(i < n_ctx_tiles, 0, 1 + (i - n_ctx_tiles) % tps)

    def kv_map(i):
        lat = i >= n_ctx_tiles
        b = jnp.where(lat, (i - n_ctx_tiles) // tps, i // ctx_blocks)
        j = jnp.where(lat, ctx_blocks + (i - n_ctx_tiles) % tps, i % ctx_blocks)
        return (b, j, 0)

    return pl.pallas_call(
        functools.partial(_qkv_kernel, n_ctx_tiles, tiles_per_sample, n_samples),
        grid=(t // TM,),
        in_specs=[
            pl.BlockSpec((TM, d), lambda i: (i, 0)),
            _const_spec((MOD_ROWS, N_MOD * d)),
            _const_spec((1, d)),
            _const_spec((d, nqkv)),
            _const_spec((1, HEAD_DIM)),
            _const_spec((1, HEAD_DIM)),
            pl.BlockSpec((TM, HEAD_DIM), lambda i: (pos_block(i), 0)),
            pl.BlockSpec((TM, HEAD_DIM), lambda i: (pos_block(i), 0)),
        ],
        out_specs=(
            pl.BlockSpec((TM, nq), lambda i: (i, 0)),
            pl.BlockSpec((1, TM, nkv), kv_map),
            pl.BlockSpec((1, TM, nkv), kv_map),
        ),
        out_shape=(
            jax.ShapeDtypeStruct((t, nq), BF16),
            jax.ShapeDtypeStruct((n_samples, n_ctx + n_lat, nkv), BF16),
            jax.ShapeDtypeStruct((n_samples, n_ctx + n_lat, nkv), BF16),
        ),
        compiler_params=_cparams(("arbitrary",)),
        name="qkv_rope",
    )(x_all, mod, nm.reshape(1, d), w_qkv.astype(BF16), q_gain.reshape(1, HEAD_DIM),
      k_gain.reshape(1, HEAD_DIM), cos, sin)


def _attn_kernel(tiles_per_sample,
                 x_ref, q_ref, k_ref, v_ref, mod_ref, wo_ref, nf_ref, rwt_ref, rb_ref,
                 xo_ref, f_ref, idx_ref, gate_ref, rank_ref, cnt_ref, base_ref):
    b = pl.program_id(0)
    j = pl.program_id(1)
    d = x_ref.shape[1]
    n_heads = q_ref.shape[1] // HEAD_DIM
    group = n_heads // N_KV_HEADS
    _, _, g1 = _mod_slices(mod_ref, b, d, 0)
    sh2, sc2, _ = _mod_slices(mod_ref, b, d, 3)
    tq = q_ref.shape[0]
    exp2_scale = (HEAD_DIM ** -0.5) * 1.4426950408889634
    outs = []
    for g in range(N_KV_HEADS):
        qg = jnp.concatenate([q_ref[:, h * HEAD_DIM:(h + 1) * HEAD_DIM]
                              for h in range(g * group, (g + 1) * group)], axis=0)
        kg = k_ref[0, :, g * HEAD_DIM:(g + 1) * HEAD_DIM]
        vg = v_ref[0, :, g * HEAD_DIM:(g + 1) * HEAD_DIM]
        s = lax.dot_general(qg, kg, (((1,), (1,)), ((), ())), preferred_element_type=F32)
        p = jnp.exp2((s - jnp.max(s, axis=-1, keepdims=True)) * exp2_scale)
        den = jnp.sum(p, axis=-1, keepdims=True)
        o = jnp.dot(p.astype(BF16), vg, preferred_element_type=F32) / den
        outs += [o[h * tq:(h + 1) * tq, :].astype(BF16) for h in range(group)]
    o_all = jnp.concatenate(outs, axis=1)
    y = jnp.dot(o_all, wo_ref[...], preferred_element_type=F32)
    x_new = x_ref[...] + g1 * y
    xo_ref[...] = x_new
    _router_epilogue(b * tiles_per_sample + j, x_new, sh2, sc2, nf_ref, rwt_ref, rb_ref,
                     f_ref, idx_ref, gate_ref, rank_ref, cnt_ref, base_ref)


def _attn_layer(x_all, q, k_all, v_all, mod, n_ctx_tiles, tiles_per_sample, n_samples,
                w_o, nf, rw, rb):
    d = x_all.shape[1]
    t_lat = n_samples * tiles_per_sample * TM
    n_exp = rw.shape[1]
    nq = q.shape[1]
    lk, nkv = k_all.shape[1], k_all.shape[2]
    tps = tiles_per_sample
    lat_tile = lambda b, j: n_ctx_tiles + b * tps + j
    tile = lambda b, j: b * tps + j
    return pl.pallas_call(
        functools.partial(_attn_kernel, tiles_per_sample),
        grid=(n_samples, tps),
        in_specs=[
            pl.BlockSpec((TM, d), lambda b, j: (lat_tile(b, j), 0)),
            pl.BlockSpec((TM, nq), lambda b, j: (lat_tile(b, j), 0)),
            pl.BlockSpec((1, lk, nkv), lambda b, j: (b, 0, 0)),
            pl.BlockSpec((1, lk, nkv), lambda b, j: (b, 0, 0)),
            _const_spec((MOD_ROWS, N_MOD * d)),
            _const_spec((nq, d)),
            _const_spec((1, d)),
            _const_spec((n_exp, d)),
            _const_spec((n_exp, 1)),
        ],
        out_specs=(pl.BlockSpec((TM, d), lambda b, j: (tile(b, j), 0)),)
        + _router_out_specs(tile, TM, d, n_exp),
        out_shape=(jax.ShapeDtypeStruct((t_lat, d), F32),) + _router_out(t_lat, d, n_exp),
        scratch_shapes=[pltpu.VMEM((n_exp, TM), F32)],
        compiler_params=_cparams(("arbitrary", "arbitrary")),
        name="attn_mixer",
    )(x_all, q, k_all, v_all, mod, w_o.astype(BF16), nf.reshape(1, d), rw.T, rb.reshape(n_exp, 1))


def _dispatch_kernel(off_ref, idx_ref, rank_ref, f_ref, xs_hbm, dest_ref, sem):
    tm = f_ref.shape[0]

    def start(t, c):
        for k in range(TOP_K):
            dst = off_ref[idx_ref[k, t]] + rank_ref[k, t]
            dest_ref[k, t] = dst
            pltpu.make_async_copy(f_ref.at[pl.ds(t, 1)], xs_hbm.at[pl.ds(dst, 1)], sem).start()
        return c

    lax.fori_loop(0, tm, start, 0)
    for k in range(TOP_K):
        pltpu.make_async_copy(f_ref, xs_hbm.at[pl.ds(0, tm)], sem).wait()


def _dispatch(f, idx, rank, off_start):
    t, d = f.shape
    smem_tile = pl.BlockSpec((TOP_K, TM), lambda i, off: (0, i), memory_space=pltpu.SMEM)
    grid_spec = pltpu.PrefetchScalarGridSpec(
        num_scalar_prefetch=1,
        grid=(t // TM,),
        in_specs=[smem_tile, smem_tile, pl.BlockSpec((TM, d), lambda i, off: (i, 0))],
        out_specs=(pl.BlockSpec(memory_space=pl.ANY), smem_tile),
        scratch_shapes=[pltpu.SemaphoreType.DMA],
    )
    return pl.pallas_call(
        _dispatch_kernel,
        grid_spec=grid_spec,
        out_shape=(jax.ShapeDtypeStruct((t * TOP_K, d), F32),
                   jax.ShapeDtypeStruct((TOP_K, t), jnp.int32)),
        compiler_params=_cparams(("arbitrary",)),
        name="moe_dispatch",
    )(off_start, idx, rank, f)


def _expert_kernel(blk_ref, exp_ref, lo_ref, hi_ref,
                   xs_ref, wgu_ref, bgu_ref, wd_ref, bd_ref, o_ref, wgu_b, wd_b):
    w = pl.program_id(0)
    tm, d = xs_ref.shape
    de = wd_ref.shape[2]
    prev = jnp.maximum(w - 1, 0)
    lo = lo_ref[w]
    hi = hi_ref[w]
    blk = blk_ref[w]
    new_expert = (w == 0) | (exp_ref[w] != exp_ref[prev])
    first_visit = (w == 0) | (blk != blk_ref[prev])

    @pl.when(new_expert)
    def _():
        wgu_b[...] = wgu_ref[0, 0].astype(BF16)
        wd_b[...] = wd_ref[0, 0].astype(BF16)

    @pl.when(first_visit & (hi <= lo))
    def _():
        o_ref[...] = jnp.zeros_like(o_ref)

    @pl.when(hi > lo)
    def _():
        gu = jnp.dot(xs_ref[...].astype(BF16), wgu_b[...], preferred_element_type=F32) + bgu_ref[0]
        g = jnp.minimum(gu[:, :de], SWIGLU_LIMIT)
        u = jnp.clip(gu[:, de:], -SWIGLU_LIMIT, SWIGLU_LIMIT)
        act = g * _sigmoid(SWIGLU_ALPHA * g) * (u + 1.0)
        y = jnp.dot(act.astype(BF16), wd_b[...], preferred_element_type=F32) + bd_ref[0]
        rows = blk * tm + lax.broadcasted_iota(jnp.int32, (tm, d), 0)
        mine = (rows >= lo) & (rows < hi)

        @pl.when(first_visit)
        def _():
            o_ref[...] = jnp.where(mine, y, 0.0)

        @pl.when(jnp.logical_not(first_visit))
        def _():
            o_ref[...] = jnp.where(mine, y, o_ref[...])


def _experts(xs, work, layer, w_gu, b_gu, w_down, b_down):
    n, d = xs.shape
    _, n_exp, _, de2 = w_gu.shape
    de = w_down.shape[2]
    n_work = work[0].shape[0]
    grid_spec = pltpu.PrefetchScalarGridSpec(
        num_scalar_prefetch=4,
        grid=(n_work,),
        in_specs=[
            pl.BlockSpec((TE, d), lambda w, blk, ex, lo, hi: (blk[w], 0)),
            pl.BlockSpec((1, 1, d, de2), lambda w, blk, ex, lo, hi: (layer, ex[w], 0, 0)),
            pl.BlockSpec((1, 1, de2), lambda w, blk, ex, lo, hi: (ex[w], 0, 0)),
            pl.BlockSpec((1, 1, de, d), lambda w, blk, ex, lo, hi: (layer, ex[w], 0, 0)),
            pl.BlockSpec((1, 1, d), lambda w, blk, ex, lo, hi: (ex[w], 0, 0)),
        ],
        out_specs=pl.BlockSpec((TE, d), lambda w, blk, ex, lo, hi: (blk[w], 0)),
        scratch_shapes=[pltpu.VMEM((d, de2), BF16), pltpu.VMEM((de, d), BF16)],
    )
    return pl.pallas_call(
        _expert_kernel,
        grid_spec=grid_spec,
        out_shape=jax.ShapeDtypeStruct((n, d), F32),
        compiler_params=_cparams(("arbitrary",)),
        name="moe_experts",
    )(*work, xs, w_gu, b_gu[layer].reshape(n_exp, 1, de2), w_down, b_down[layer].reshape(n_exp, 1, d))


def _combine_kernel(row_of_tile, dest_ref, gate_ref, x_ref, mod_ref, yb_hbm, o_ref, buf, sem):
    i = pl.program_id(0)
    tm, d = x_ref.shape

    def start(t, c):
        for k in range(TOP_K):
            pltpu.make_async_copy(yb_hbm.at[pl.ds(dest_ref[k, t], 1)],
                                  buf.at[k, pl.ds(t, 1)], sem).start()
        return c

    lax.fori_loop(0, tm, start, 0)
    g2 = mod_ref[pl.ds(row_of_tile(i), 1), pl.ds(5 * d, d)]
    gates = jnp.concatenate([gate_ref[...], jnp.zeros((HEAD_DIM - 8, tm), F32)], axis=0).T
    for k in range(TOP_K):
        pltpu.make_async_copy(yb_hbm.at[pl.ds(0, tm)], buf.at[k], sem).wait()
    y = buf[0] * gates[:, 0:1]
    for k in range(1, TOP_K):
        y = y + buf[k] * gates[:, k:k + 1]
    o_ref[...] = x_ref[...] + g2 * y


def _combine(x_new, yb, dest, gates, mod, row_of_tile):
    t, d = x_new.shape
    return pl.pallas_call(
        functools.partial(_combine_kernel, row_of_tile),
        grid=(t // TM,),
        in_specs=[
            pl.BlockSpec((TOP_K, TM), lambda i: (0, i), memory_space=pltpu.SMEM),
            pl.BlockSpec((8, TM), lambda i: (0, i)),
            pl.BlockSpec((TM, d), lambda i: (i, 0)),
            _const_spec((MOD_ROWS, N_MOD * d)),
            pl.BlockSpec(memory_space=pl.ANY),
        ],
        out_specs=pl.BlockSpec((TM, d), lambda i: (i, 0)),
        out_shape=jax.ShapeDtypeStruct((t, d), F32),
        scratch_shapes=[pltpu.VMEM((TOP_K, TM, d), F32), pltpu.SemaphoreType.DMA],
        compiler_params=_cparams(("arbitrary",)),
        name="moe_combine",
    )(dest, gates, x_new, mod, yb)


def _work_items(counts, n_assign):
    n_exp = counts.shape[0]
    n_blocks = n_assign // TE
    n_work = n_blocks + n_exp - 1
    off_end = jnp.cumsum(counts)
    off_start = off_end - counts
    first_blk = off_start // TE
    n_items = jnp.where(counts > 0, (off_end - 1) // TE - first_blk + 1, 0)
    item_end = jnp.cumsum(n_items)
    item_start = item_end - n_items
    w = jnp.arange(n_work, dtype=jnp.int32)
    ex = jnp.minimum(jnp.sum(w[:, None] >= item_end[None, :], axis=1), n_exp - 1).astype(jnp.int32)
    used = w < item_end[-1]
    blk = jnp.where(used, first_blk[ex] + w - item_start[ex], n_blocks - 1).astype(jnp.int32)
    lo = jnp.where(used, jnp.maximum(off_start[ex], blk * TE), 0).astype(jnp.int32)
    hi = jnp.where(used, jnp.minimum(off_end[ex], (blk + 1) * TE), 0).astype(jnp.int32)
    return (blk, ex, lo, hi), off_start.astype(jnp.int32)


def _moe(x_new, f, idx, gates, rank, cnt, mod, row_of_tile, layer, w_gu, b_gu, w_down, b_down):
    t = f.shape[0]
    work, off_start = _work_items(cnt[:, 0], t * TOP_K)
    xs, dest = _dispatch(f, idx, rank, off_start)
    yb = _experts(xs, work, layer, w_gu, b_gu, w_down, b_down)
    return _combine(x_new, yb, dest, gates, mod, row_of_tile)


def kernel(x, c, ctx, c_ctx, ada_w, ada_b, norm_mix, norm_ffn, gm_w_in, gm_b_in, gm_v_gain, gm_w_s,
           gm_b_s, gm_w_out, at_w_qkv, at_q_gain, at_k_gain, at_w_o, moe_router_w, moe_router_b,
           moe_w_gu, moe_b_gu, moe_w_down, moe_b_down):
    n_samples, n_lat, d = x.shape
    n_ctx = ctx.shape[1]
    assert n_lat % TM == 0 and n_ctx % TM == 0 and n_samples < MOD_ROWS
    assert n_lat % TG == 0 and (n_samples * n_ctx) % TG == 0 and TG % CHUNK == 0
    assert (n_samples * n_lat * TOP_K) % TE == 0 and (n_samples * n_ctx * TOP_K) % TE == 0
    n_ctx_tiles = n_samples * n_ctx // TM
    tps = n_lat // TM

    mods = _ada_table(c, c_ctx, ada_w, ada_b)
    x_all = jnp.concatenate([ctx.reshape(-1, d), x.reshape(-1, d)], axis=0)

    def row_all(i):
        return jnp.where(i < n_ctx_tiles, n_samples, (i - n_ctx_tiles) // tps)

    x_new, f, idx, gates, rank, cnt = _gmlp_layer(
        x_all, mods[0], n_samples * n_ctx // TG, n_lat // TG, n_samples, norm_mix[0], gm_w_in[0], gm_b_in[0],
        gm_v_gain[0], gm_w_s[0], gm_b_s[0], gm_w_out[0], norm_ffn[0], moe_router_w[0],
        moe_router_b[0])
    x_all = _moe(x_new, f, idx, gates, rank, cnt, mods[0], row_all,
                 0, moe_w_gu, moe_b_gu, moe_w_down, moe_b_down)

    q, k_all, v_all = _qkv_layer(x_all, mods[1], n_ctx_tiles, tps, n_samples, n_ctx, n_lat,
                                 norm_mix[1], at_w_qkv[0], at_q_gain[0], at_k_gain[0])
    x_new, f, idx, gates, rank, cnt = _attn_layer(
        x_all, q, k_all, v_all, mods[1], n_ctx_tiles, tps, n_samples, at_w_o[0], norm_ffn[1],
        moe_router_w[1], moe_router_b[1])
    out = _moe(x_new, f, idx, gates, rank, cnt, mods[1], lambda i: i // tps,
               1, moe_w_gu, moe_b_gu, moe_w_down, moe_b_down)
    return out.reshape(n_samples, n_lat, d)
```

```python
import functools

import jax
import jax.numpy as jnp
from jax import lax
from jax.experimental import pallas as pl
from jax.experimental.pallas import tpu as pltpu

F32 = jnp.float32
BF16 = jnp.bfloat16
HIGHEST = lax.Precision.HIGHEST

GRID_W = 64
N_MOD = 6
NORM_EPS = 1e-6
CHUNK = 128
GM_GROUPS = 8
HEAD_DIM = 128
N_KV_HEADS = 2
AXIS_DIM = HEAD_DIM // 2
ROPE_THETA = 10000.0
TOP_K = 4
SWIGLU_LIMIT = 7.0
SWIGLU_ALPHA = 1.702

TM = 256
TG = 512
TE = 512
MOD_ROWS = 16
V7X_VMEM_LIMIT = 56 * 1024 * 1024


def _cparams(sem, vmem=V7X_VMEM_LIMIT):
    return pltpu.CompilerParams(dimension_semantics=sem, vmem_limit_bytes=vmem)


def _const_spec(shape):
    nd = len(shape)
    return pl.BlockSpec(shape, lambda *_: (0,) * nd, pipeline_mode=pl.Buffered(1))


LANES = 128
SUBLANES = 8


def _load_tile_rows(ref, n):
    return jnp.concatenate([ref[pl.ds(s, n, stride=SUBLANES), :] for s in range(SUBLANES)], axis=1)


def _store_tile_rows(ref, val):
    n = val.shape[0]
    for s in range(SUBLANES):
        ref[pl.ds(s, n, stride=SUBLANES), :] = val[:, s * LANES:(s + 1) * LANES]


def _sigmoid(x):
    return 1.0 / (1.0 + jnp.exp(-x))


def _rms(x):
    return x * lax.rsqrt(jnp.mean(x * x, axis=-1, keepdims=True) + NORM_EPS)


def _ada_kernel(s_ref, w_ref, b_ref, o_ref):
    s = s_ref[...]
    s = s * _sigmoid(s)
    o_ref[0] = jnp.dot(s, w_ref[0], precision=HIGHEST, preferred_element_type=F32) + b_ref[0]


def _ada_table(c, c_ctx, ada_w, ada_b):
    depth, d, n = ada_w.shape
    b = c.shape[0]
    s = jnp.concatenate([c, c_ctx[None, :], jnp.zeros((MOD_ROWS - b - 1, d), F32)], axis=0)
    tn = 1536
    return pl.pallas_call(
        _ada_kernel,
        grid=(depth, n // tn),
        in_specs=[
            pl.BlockSpec((MOD_ROWS, d), lambda i, j: (0, 0)),
            pl.BlockSpec((1, d, tn), lambda i, j: (i, 0, j)),
            pl.BlockSpec((1, 1, tn), lambda i, j: (i, 0, j)),
        ],
        out_specs=pl.BlockSpec((1, MOD_ROWS, tn), lambda i, j: (i, 0, j)),
        out_shape=jax.ShapeDtypeStruct((depth, MOD_ROWS, n), F32),
        compiler_params=_cparams(("arbitrary", "arbitrary")),
        name="ada_table",
    )(s, ada_w, ada_b.reshape(depth, 1, n))


def _mod_slices(mod_ref, row, d, first):
    return [mod_ref[pl.ds(row, 1), pl.ds((first + k) * d, d)] for k in range(3)]


def _router_epilogue(step, x_new, sh2, sc2, nf_ref, rwt_ref, rb_ref,
                     f_ref, idx_ref, gate_ref, rank_ref, cnt_ref, base_ref):
    tm = x_new.shape[0]
    n_exp = rwt_ref.shape[0]
    f = _rms(x_new) * nf_ref[...] * (1.0 + sc2) + sh2
    _store_tile_rows(f_ref, f)

    logits = lax.dot_general(rwt_ref[...], f, (((1,), (1,)), ((), ())),
                             precision=HIGHEST, preferred_element_type=F32) + rb_ref[...]
    eid = lax.broadcasted_iota(jnp.int32, (n_exp, tm), 0).astype(F32)

    @pl.when(step == 0)
    def _():
        base_ref[...] = jnp.zeros_like(base_ref)

    r_io = lax.broadcasted_iota(jnp.int32, (tm, tm), 0)
    c_io = lax.broadcasted_iota(jnp.int32, (tm, tm), 1)
    before = jnp.where(r_io < c_io, 1.0, 0.0).astype(BF16)
    ones = jnp.ones((tm, tm), BF16)

    vals, idxs, ranks = [], [], []
    l = logits
    for _ in range(TOP_K):
        m = jnp.max(l, axis=0, keepdims=True)
        sel = jnp.min(jnp.where(l == m, eid, float(n_exp)), axis=0, keepdims=True)
        hit = eid == sel
        l = jnp.where(hit, -jnp.inf, l)
        oh = jnp.where(hit, 1.0, 0.0).astype(BF16)
        prefix = jnp.dot(oh, before, preferred_element_type=F32)
        base = base_ref[...]
        ranks.append(jnp.sum(jnp.where(hit, base + prefix, 0.0), axis=0, keepdims=True))
        base_ref[...] = base + jnp.dot(oh, ones, preferred_element_type=F32)
        vals.append(m)
        idxs.append(sel)
    es = [jnp.exp(v - vals[0]) for v in vals]
    tot = es[0] + es[1] + es[2] + es[3]
    zero = jnp.zeros_like(tot)
    gate_ref[...] = jnp.concatenate([e / tot for e in es] + [zero] * (8 - TOP_K), axis=0)
    idx_ref[...] = jnp.concatenate(idxs, axis=0).astype(jnp.int32)
    rank_ref[...] = jnp.concatenate(ranks, axis=0).astype(jnp.int32)
    cnt_ref[...] = base_ref[:, :128].astype(jnp.int32)


def _router_out(t, d, n_exp):
    shapes = (
        jax.ShapeDtypeStruct((t * SUBLANES, LANES), F32),
        jax.ShapeDtypeStruct((TOP_K, t), jnp.int32),
        jax.ShapeDtypeStruct((8, t), F32),
        jax.ShapeDtypeStruct((TOP_K, t), jnp.int32),
        jax.ShapeDtypeStruct((n_exp, 128), jnp.int32),
    )
    return shapes


def _router_out_specs(tile_of, tm, d, n_exp):
    return (
        pl.BlockSpec((tm * SUBLANES, LANES), lambda *g: (tile_of(*g), 0)),
        pl.BlockSpec((TOP_K, tm), lambda *g: (0, tile_of(*g))),
        pl.BlockSpec((8, tm), lambda *g: (0, tile_of(*g))),
        pl.BlockSpec((TOP_K, tm), lambda *g: (0, tile_of(*g))),
        pl.BlockSpec((n_exp, 128), lambda *g: (0, 0)),
    )


def _gmlp_kernel(n_ctx_tiles, tiles_per_sample, n_samples,
                 x_ref, mod_ref, nm_ref, win_ref, bin_ref, vg_ref, ws_ref, bs_ref, wout_ref,
                 nf_ref, rwt_ref, rb_ref,
                 xo_ref, f_ref, idx_ref, gate_ref, rank_ref, cnt_ref, base_ref):
    i = pl.program_id(0)
    d = x_ref.shape[1]
    gw = wout_ref.shape[0]
    gc = gw // GM_GROUPS
    row = jnp.where(i < n_ctx_tiles, n_samples, (i - n_ctx_tiles) // tiles_per_sample)
    sh1, sc1, g1 = _mod_slices(mod_ref, row, d, 0)
    sh2, sc2, _ = _mod_slices(mod_ref, row, d, 3)

    x = x_ref[...]
    h = _rms(x) * nm_ref[...] * (1.0 + sc1) + sh1
    uv = jnp.dot(h.astype(BF16), win_ref[...], preferred_element_type=F32) + bin_ref[...]
    uv = 0.5 * uv * (1.0 + lax.erf(uv * (2.0 ** -0.5)))
    u = uv[:, :gw]
    v = (_rms(uv[:, gw:]) * vg_ref[...]).astype(BF16)
    parts = []
    for c in range(x.shape[0] // CHUNK):
        vc = v[c * CHUNK:(c + 1) * CHUNK, :]
        s = jnp.concatenate(
            [jnp.dot(ws_ref[g], vc[:, g * gc:(g + 1) * gc], preferred_element_type=F32)
             for g in range(GM_GROUPS)], axis=1) + bs_ref[...]
        parts.append((u[c * CHUNK:(c + 1) * CHUNK, :] * s).astype(BF16))
    z = jnp.concatenate(parts, axis=0)
    y = jnp.dot(z, wout_ref[...], preferred_element_type=F32)
    x_new = x + g1 * y
    xo_ref[...] = x_new
    _router_epilogue(i, x_new, sh2, sc2, nf_ref, rwt_ref, rb_ref,
                     f_ref, idx_ref, gate_ref, rank_ref, cnt_ref, base_ref)


def _gmlp_layer(x_all, mod, n_ctx_tiles, tiles_per_sample, n_samples,
                nm, w_in, b_in, v_gain, w_s, b_s, w_out, nf, rw, rb):
    t, d = x_all.shape
    gw = w_out.shape[0]
    n_exp = rw.shape[1]
    gc = gw // GM_GROUPS
    bs_full = jnp.repeat(b_s.T, gc, axis=1)
    tile = lambda i: i
    outs = pl.pallas_call(
        functools.partial(_gmlp_kernel, n_ctx_tiles, tiles_per_sample, n_samples),
        grid=(t // TG,),
        in_specs=[
            pl.BlockSpec((TG, d), lambda i: (i, 0)),
            _const_spec((MOD_ROWS, N_MOD * d)),
            _const_spec((1, d)),
            _const_spec((d, 2 * gw)),
            _const_spec((1, 2 * gw)),
            _const_spec((1, gw)),
            _const_spec((GM_GROUPS, CHUNK, CHUNK)),
            _const_spec((CHUNK, gw)),
            _const_spec((gw, d)),
            _const_spec((1, d)),
            _const_spec((n_exp, d)),
            _const_spec((n_exp, 1)),
        ],
        out_specs=(pl.BlockSpec((TG, d), lambda i: (i, 0)),) + _router_out_specs(tile, TG, d, n_exp),
        out_shape=(jax.ShapeDtypeStruct((t, d), F32),) + _router_out(t, d, n_exp),
        scratch_shapes=[pltpu.VMEM((n_exp, TG), F32)],
        compiler_params=_cparams(("arbitrary",)),
        name="gmlp_mixer",
    )(x_all, mod, nm.reshape(1, d), w_in.astype(BF16), b_in.reshape(1, -1), v_gain.reshape(1, gw),
      w_s.astype(BF16), bs_full, w_out.astype(BF16), nf.reshape(1, d), rw.T, rb.reshape(n_exp, 1))
    return outs


def _qkv_kernel(n_ctx_tiles, tiles_per_sample, n_samples,
                x_ref, mod_ref, nm_ref, w_ref, qg_ref, kg_ref, cos_ref, sin_ref,
                q_ref, k_ref, v_ref):
    i = pl.program_id(0)
    d = x_ref.shape[1]
    nq = q_ref.shape[1]
    nkv = k_ref.shape[2]
    row = jnp.where(i < n_ctx_tiles, n_samples, (i - n_ctx_tiles) // tiles_per_sample)
    sh1, sc1, _ = _mod_slices(mod_ref, row, d, 0)
    h = _rms(x_ref[...]) * nm_ref[...] * (1.0 + sc1) + sh1
    qkv = jnp.dot(h.astype(BF16), w_ref[...], preferred_element_type=F32)
    cos = cos_ref[...]
    sin = sin_ref[...]
    half = AXIS_DIM // 2
    lane = lax.broadcasted_iota(jnp.int32, (x_ref.shape[0], HEAD_DIM), 1)
    first_half = (lane % AXIS_DIM) < half

    def head(xh, gain):
        xh = _rms(xh) * gain
        partner = jnp.where(first_half, pltpu.roll(xh, HEAD_DIM - half, 1), pltpu.roll(xh, half, 1))
        return xh * cos + partner * sin

    q = [head(qkv[:, j * HEAD_DIM:(j + 1) * HEAD_DIM], qg_ref[...]) for j in range(nq // HEAD_DIM)]
    k = [head(qkv[:, nq + j * HEAD_DIM:nq + (j + 1) * HEAD_DIM], kg_ref[...])
         for j in range(nkv // HEAD_DIM)]
    q_ref[...] = jnp.concatenate(q, axis=1).astype(BF16)
    k_ref[0] = jnp.concatenate(k, axis=1).astype(BF16)
    v_ref[0] = qkv[:, nq + nkv:].astype(BF16)


def _rope_tables(n_lat):
    rows = n_lat // GRID_W
    row = jnp.repeat(jnp.arange(rows, dtype=jnp.int32), GRID_W).astype(F32)
    col = jnp.tile(jnp.arange(GRID_W, dtype=jnp.int32), rows).astype(F32)
    inv_freq = 1.0 / (ROPE_THETA ** (jnp.arange(0, AXIS_DIM, 2, dtype=F32) / AXIS_DIM))
    ang_r = row[:, None] * inv_freq
    ang_c = col[:, None] * inv_freq
    cos = jnp.concatenate([jnp.cos(ang_r)] * 2 + [jnp.cos(ang_c)] * 2, axis=1)
    sin = jnp.concatenate([-jnp.sin(ang_r), jnp.sin(ang_r), -jnp.sin(ang_c), jnp.sin(ang_c)], axis=1)
    cos = jnp.concatenate([jnp.ones((TM, HEAD_DIM), F32), cos], axis=0)
    sin = jnp.concatenate([jnp.zeros((TM, HEAD_DIM), F32), sin], axis=0)
    return cos, sin


def _qkv_layer(x_all, mod, n_ctx_tiles, tiles_per_sample, n_samples, n_ctx, n_lat,
               nm, w_qkv, q_gain, k_gain):
    t, d = x_all.shape
    nqkv = w_qkv.shape[1]
    nkv = N_KV_HEADS * HEAD_DIM
    nq = nqkv - 2 * nkv
    cos, sin = _rope_tables(n_lat)
    ctx_blocks = n_ctx // TM
    tps = tiles_per_sample

    def pos_block(i):
        return jnp.where(i < n_ctx_tiles, 0, 1 + (i - n_ctx_tiles) % tps)

    def kv_map(i):
        lat = i >= n_ctx_tiles
        b = jnp.where(lat, (i - n_ctx_tiles) // tps, i // ctx_blocks)
        j = jnp.where(lat, ctx_blocks + (i - n_ctx_tiles) % tps, i % ctx_blocks)
        return (b, j, 0)

    return pl.pallas_call(
        functools.partial(_qkv_kernel, n_ctx_tiles, tiles_per_sample, n_samples),
        grid=(t // TM,),
        in_specs=[
            pl.BlockSpec((TM, d), lambda i: (i, 0)),
            _const_spec((MOD_ROWS, N_MOD * d)),
            _const_spec((1, d)),
            _const_spec((d, nqkv)),
            _const_spec((1, HEAD_DIM)),
            _const_spec((1, HEAD_DIM)),
            pl.BlockSpec((TM, HEAD_DIM), lambda i: (pos_block(i), 0)),
            pl.BlockSpec((TM, HEAD_DIM), lambda i: (pos_block(i), 0)),
        ],
        out_specs=(
            pl.BlockSpec((TM, nq), lambda i: (i, 0)),
            pl.BlockSpec((1, TM, nkv), kv_map),
            pl.BlockSpec((1, TM, nkv), kv_map),
        ),
        out_shape=(
            jax.ShapeDtypeStruct((t, nq), BF16),
            jax.ShapeDtypeStruct((n_samples, n_ctx + n_lat, nkv), BF16),
            jax.ShapeDtypeStruct((n_samples, n_ctx + n_lat, nkv), BF16),
        ),
        compiler_params=_cparams(("arbitrary",)),
        name="qkv_rope",
    )(x_all, mod, nm.reshape(1, d), w_qkv.astype(BF16), q_gain.reshape(1, HEAD_DIM),
      k_gain.reshape(1, HEAD_DIM), cos, sin)


def _attn_kernel(tiles_per_sample,
                 x_ref, q_ref, k_ref, v_ref, mod_ref, wo_ref, nf_ref, rwt_ref, rb_ref,
                 xo_ref, f_ref, idx_ref, gate_ref, rank_ref, cnt_ref, base_ref):
    b = pl.program_id(0)
    j = pl.program_id(1)
    d = x_ref.shape[1]
    n_heads = q_ref.shape[1] // HEAD_DIM
    group = n_heads // N_KV_HEADS
    _, _, g1 = _mod_slices(mod_ref, b, d, 0)
    sh2, sc2, _ = _mod_slices(mod_ref, b, d, 3)
    tq = q_ref.shape[0]
    exp2_scale = (HEAD_DIM ** -0.5) * 1.4426950408889634
    outs = []
    for g in range(N_KV_HEADS):
        qg = jnp.concatenate([q_ref[:, h * HEAD_DIM:(h + 1) * HEAD_DIM]
                              for h in range(g * group, (g + 1) * group)], axis=0)
        kg = k_ref[0, :, g * HEAD_DIM:(g + 1) * HEAD_DIM]
        vg = v_ref[0, :, g * HEAD_DIM:(g + 1) * HEAD_DIM]
        s = lax.dot_general(qg, kg, (((1,), (1,)), ((), ())), preferred_element_type=F32)
        p = jnp.exp2((s - jnp.max(s, axis=-1, keepdims=True)) * exp2_scale)
        den = jnp.sum(p, axis=-1, keepdims=True)
        o = jnp.dot(p.astype(BF16), vg, preferred_element_type=F32) / den
        outs += [o[h * tq:(h + 1) * tq, :].astype(BF16) for h in range(group)]
    o_all = jnp.concatenate(outs, axis=1)
    y = jnp.dot(o_all, wo_ref[...], preferred_element_type=F32)
    x_new = x_ref[...] + g1 * y
    xo_ref[...] = x_new
    _router_epilogue(b * tiles_per_sample + j, x_new, sh2, sc2, nf_ref, rwt_ref, rb_ref,
                     f_ref, idx_ref, gate_ref, rank_ref, cnt_ref, base_ref)


def _attn_layer(x_all, q, k_all, v_all, mod, n_ctx_tiles, tiles_per_sample, n_samples,
                w_o, nf, rw, rb):
    d = x_all.shape[1]
    t_lat = n_samples * tiles_per_sample * TM
    n_exp = rw.shape[1]
    nq = q.shape[1]
    lk, nkv = k_all.shape[1], k_all.shape[2]
    tps = tiles_per_sample
    lat_tile = lambda b, j: n_ctx_tiles + b * tps + j
    tile = lambda b, j: b * tps + j
    return pl.pallas_call(
        functools.partial(_attn_kernel, tiles_per_sample),
        grid=(n_samples, tps),
        in_specs=[
            pl.BlockSpec((TM, d), lambda b, j: (lat_tile(b, j), 0)),
            pl.BlockSpec((TM, nq), lambda b, j: (lat_tile(b, j), 0)),
            pl.BlockSpec((1, lk, nkv), lambda b, j: (b, 0, 0)),
            pl.BlockSpec((1, lk, nkv), lambda b, j: (b, 0, 0)),
            _const_spec((MOD_ROWS, N_MOD * d)),
            _const_spec((nq, d)),
            _const_spec((1, d)),
            _const_spec((n_exp, d)),
            _const_spec((n_exp, 1)),
        ],
        out_specs=(pl.BlockSpec((TM, d), lambda b, j: (tile(b, j), 0)),)
        + _router_out_specs(tile, TM, d, n_exp),
        out_shape=(jax.ShapeDtypeStruct((t_lat, d), F32),) + _router_out(t_lat, d, n_exp),
        scratch_shapes=[pltpu.VMEM((n_exp, TM), F32)],
        compiler_params=_cparams(("arbitrary", "arbitrary")),
        name="attn_mixer",
    )(x_all, q, k_all, v_all, mod, w_o.astype(BF16), nf.reshape(1, d), rw.T, rb.reshape(n_exp, 1))


def _dispatch_kernel(off_ref, idx_ref, rank_ref, f_ref, xs_hbm, dest_ref, sem):
    tm = idx_ref.shape[1]

    def start(t, c):
        src = f_ref.at[pl.ds(pl.multiple_of(t * SUBLANES, SUBLANES), SUBLANES)]
        for k in range(TOP_K):
            dst = off_ref[idx_ref[k, t]] + rank_ref[k, t]
            dest_ref[k, t] = dst
            row = xs_hbm.at[pl.ds(pl.multiple_of(dst * SUBLANES, SUBLANES), SUBLANES)]
            pltpu.make_async_copy(src, row, sem).start(priority=k % 2)
        return c

    lax.fori_loop(0, tm, start, 0)
    for k in range(TOP_K):
        pltpu.make_async_copy(f_ref, xs_hbm.at[pl.ds(0, tm * SUBLANES)], sem).wait()


def _dispatch(f, idx, rank, off_start):
    t = idx.shape[1]
    smem_tile = pl.BlockSpec((TOP_K, TM), lambda i, off: (0, i), memory_space=pltpu.SMEM)
    grid_spec = pltpu.PrefetchScalarGridSpec(
        num_scalar_prefetch=1,
        grid=(t // TM,),
        in_specs=[smem_tile, smem_tile, pl.BlockSpec((TM * SUBLANES, LANES), lambda i, off: (i, 0))],
        out_specs=(pl.BlockSpec(memory_space=pl.ANY), smem_tile),
        scratch_shapes=[pltpu.SemaphoreType.DMA],
    )
    return pl.pallas_call(
        _dispatch_kernel,
        grid_spec=grid_spec,
        out_shape=(jax.ShapeDtypeStruct((t * TOP_K * SUBLANES, LANES), F32),
                   jax.ShapeDtypeStruct((TOP_K, t), jnp.int32)),
        compiler_params=_cparams(("arbitrary",)),
        name="moe_dispatch",
    )(off_start, idx, rank, f)


def _expert_kernel(blk_ref, exp_ref, lo_ref, hi_ref,
                   xs_ref, wgu_ref, bgu_ref, wd_ref, bd_ref, o_ref, wgu_b, wd_b):
    w = pl.program_id(0)
    tm = xs_ref.shape[0] // SUBLANES
    d = SUBLANES * LANES
    de = wd_ref.shape[2]
    prev = jnp.maximum(w - 1, 0)
    lo = lo_ref[w]
    hi = hi_ref[w]
    blk = blk_ref[w]
    new_expert = (w == 0) | (exp_ref[w] != exp_ref[prev])
    first_visit = (w == 0) | (blk != blk_ref[prev])

    @pl.when(new_expert)
    def _():
        wgu_b[...] = wgu_ref[0, 0].astype(BF16)
        wd_b[...] = wd_ref[0, 0].astype(BF16)

    @pl.when(first_visit & (hi <= lo))
    def _():
        o_ref[...] = jnp.zeros_like(o_ref)

    @pl.when(hi > lo)
    def _():
        x = _load_tile_rows(xs_ref, tm).astype(BF16)
        gu = jnp.dot(x, wgu_b[...], preferred_element_type=F32) + bgu_ref[0]
        g = jnp.minimum(gu[:, :de], SWIGLU_LIMIT)
        u = jnp.clip(gu[:, de:], -SWIGLU_LIMIT, SWIGLU_LIMIT)
        act = g * _sigmoid(SWIGLU_ALPHA * g) * (u + 1.0)
        y = jnp.dot(act.astype(BF16), wd_b[...], preferred_element_type=F32) + bd_ref[0]
        rows = blk * tm + lax.broadcasted_iota(jnp.int32, (tm, d), 0)
        mine = (rows >= lo) & (rows < hi)

        @pl.when(first_visit)
        def _():
            _store_tile_rows(o_ref, jnp.where(mine, y, 0.0))

        @pl.when(jnp.logical_not(first_visit))
        def _():
            _store_tile_rows(o_ref, jnp.where(mine, y, _load_tile_rows(o_ref, tm)))


def _experts(xs, work, layer, w_gu, b_gu, w_down, b_down):
    _, n_exp, d, de2 = w_gu.shape
    de = w_down.shape[2]
    n_work = work[0].shape[0]
    grid_spec = pltpu.PrefetchScalarGridSpec(
        num_scalar_prefetch=4,
        grid=(n_work,),
        in_specs=[
            pl.BlockSpec((TE * SUBLANES, LANES), lambda w, blk, ex, lo, hi: (blk[w], 0)),
            pl.BlockSpec((1, 1, d, de2), lambda w, blk, ex, lo, hi: (layer, ex[w], 0, 0)),
            pl.BlockSpec((1, 1, de2), lambda w, blk, ex, lo, hi: (ex[w], 0, 0)),
            pl.BlockSpec((1, 1, de, d), lambda w, blk, ex, lo, hi: (layer, ex[w], 0, 0)),
            pl.BlockSpec((1, 1, d), lambda w, blk, ex, lo, hi: (ex[w], 0, 0)),
        ],
        out_specs=pl.BlockSpec((TE * SUBLANES, LANES), lambda w, blk, ex, lo, hi: (blk[w], 0)),
        scratch_shapes=[pltpu.VMEM((d, de2), BF16), pltpu.VMEM((de, d), BF16)],
    )
    return pl.pallas_call(
        _expert_kernel,
        grid_spec=grid_spec,
        out_shape=jax.ShapeDtypeStruct(xs.shape, F32),
        compiler_params=_cparams(("arbitrary",)),
        name="moe_experts",
    )(*work, xs, w_gu, b_gu[layer].reshape(n_exp, 1, de2), w_down, b_down[layer].reshape(n_exp, 1, d))


def _combine_kernel(row_of_tile, dest_ref, gate_ref, x_ref, mod_ref, yb_hbm, o_ref, buf, sem):
    i = pl.program_id(0)
    tm, d = x_ref.shape

    def start(t, c):
        slot = pl.ds(pl.multiple_of(t * SUBLANES, SUBLANES), SUBLANES)
        for k in range(TOP_K):
            row = pl.ds(pl.multiple_of(dest_ref[k, t] * SUBLANES, SUBLANES), SUBLANES)
            pltpu.make_async_copy(yb_hbm.at[row], buf.at[k, slot], sem).start(priority=k % 2)
        return c

    lax.fori_loop(0, tm, start, 0)
    g2 = mod_ref[pl.ds(row_of_tile(i), 1), pl.ds(5 * d, d)]
    gates = jnp.concatenate([gate_ref[...], jnp.zeros((LANES - 8, tm), F32)], axis=0).T
    for k in range(TOP_K):
        pltpu.make_async_copy(yb_hbm.at[pl.ds(0, tm * SUBLANES)], buf.at[k], sem).wait()
    y = _load_tile_rows(buf.at[0], tm) * gates[:, 0:1]
    for k in range(1, TOP_K):
        y = y + _load_tile_rows(buf.at[k], tm) * gates[:, k:k + 1]
    o_ref[...] = x_ref[...] + g2 * y


def _combine(x_new, yb, dest, gates, mod, row_of_tile):
    t, d = x_new.shape
    return pl.pallas_call(
        functools.partial(_combine_kernel, row_of_tile),
        grid=(t // TM,),
        in_specs=[
            pl.BlockSpec((TOP_K, TM), lambda i: (0, i), memory_space=pltpu.SMEM),
            pl.BlockSpec((8, TM), lambda i: (0, i)),
            pl.BlockSpec((TM, d), lambda i: (i, 0)),
            _const_spec((MOD_ROWS, N_MOD * d)),
            pl.BlockSpec(memory_space=pl.ANY),
        ],
        out_specs=pl.BlockSpec((TM, d), lambda i: (i, 0)),
        out_shape=jax.ShapeDtypeStruct((t, d), F32),
        scratch_shapes=[pltpu.VMEM((TOP_K, TM * SUBLANES, LANES), F32), pltpu.SemaphoreType.DMA],
        compiler_params=_cparams(("arbitrary",)),
        name="moe_combine",
    )(dest, gates, x_new, mod, yb)


def _work_items(counts, n_assign):
    n_exp = counts.shape[0]
    n_blocks = n_assign // TE
    n_work = n_blocks + n_exp - 1
    off_end = jnp.cumsum(counts)
    off_start = off_end - counts
    first_blk = off_start // TE
    n_items = jnp.where(counts > 0, (off_end - 1) // TE - first_blk + 1, 0)
    item_end = jnp.cumsum(n_items)
    item_start = item_end - n_items
    w = jnp.arange(n_work, dtype=jnp.int32)
    ex = jnp.minimum(jnp.sum(w[:, None] >= item_end[None, :], axis=1), n_exp - 1).astype(jnp.int32)
    used = w < item_end[-1]
    blk = jnp.where(used, first_blk[ex] + w - item_start[ex], n_blocks - 1).astype(jnp.int32)
    lo = jnp.where(used, jnp.maximum(off_start[ex], blk * TE), 0).astype(jnp.int32)
    hi = jnp.where(used, jnp.minimum(off_end[ex], (blk + 1) * TE), 0).astype(jnp.int32)
    return (blk, ex, lo, hi), off_start.astype(jnp.int32)


def _moe(x_new, f, idx, gates, rank, cnt, mod, row_of_tile, layer, w_gu, b_gu, w_down, b_down):
    t = idx.shape[1]
    work, off_start = _work_items(cnt[:, 0], t * TOP_K)
    xs, dest = _dispatch(f, idx, rank, off_start)
    yb = _experts(xs, work, layer, w_gu, b_gu, w_down, b_down)
    return _combine(x_new, yb, dest, gates, mod, row_of_tile)


def kernel(x, c, ctx, c_ctx, ada_w, ada_b, norm_mix, norm_ffn, gm_w_in, gm_b_in, gm_v_gain, gm_w_s,
           gm_b_s, gm_w_out, at_w_qkv, at_q_gain, at_k_gain, at_w_o, moe_router_w, moe_router_b,
           moe_w_gu, moe_b_gu, moe_w_down, moe_b_down):
    n_samples, n_lat, d = x.shape
    n_ctx = ctx.shape[1]
    assert d == SUBLANES * LANES, "MoE row movement assumes one f32 tile per token row"
    assert n_lat % TM == 0 and n_ctx % TM == 0 and n_samples < MOD_ROWS
    assert n_lat % TG == 0 and (n_samples * n_ctx) % TG == 0 and TG % CHUNK == 0
    assert (n_samples * n_lat * TOP_K) % TE == 0 and (n_samples * n_ctx * TOP_K) % TE == 0
    n_ctx_tiles = n_samples * n_ctx // TM
    tps = n_lat // TM

    mods = _ada_table(c, c_ctx, ada_w, ada_b)
    x_all = jnp.concatenate([ctx.reshape(-1, d), x.reshape(-1, d)], axis=0)

    def row_all(i):
        return jnp.where(i < n_ctx_tiles, n_samples, (i - n_ctx_tiles) // tps)

    x_new, f, idx, gates, rank, cnt = _gmlp_layer(
        x_all, mods[0], n_samples * n_ctx // TG, n_lat // TG, n_samples, norm_mix[0], gm_w_in[0], gm_b_in[0],
        gm_v_gain[0], gm_w_s[0], gm_b_s[0], gm_w_out[0], norm_ffn[0], moe_router_w[0],
        moe_router_b[0])
    x_all = _moe(x_new, f, idx, gates, rank, cnt, mods[0], row_all,
                 0, moe_w_gu, moe_b_gu, moe_w_down, moe_b_down)

    q, k_all, v_all = _qkv_layer(x_all, mods[1], n_ctx_tiles, tps, n_samples, n_ctx, n_lat,
                                 norm_mix[1], at_w_qkv[0], at_q_gain[0], at_k_gain[0])
    x_new, f, idx, gates, rank, cnt = _attn_layer(
        x_all, q, k_all, v_all, mods[1], n_ctx_tiles, tps, n_samples, at_w_o[0], norm_ffn[1],
        moe_router_w[1], moe_router_b[1])
    out = _moe(x_new, f, idx, gates, rank, cnt, mods[1], lambda i: i // tps,
               1, moe_w_gu, moe_b_gu, moe_w_down, moe_b_down)
    return out.reshape(n_samples, n_lat, d)
```

```python
import functools

import jax
import jax.numpy as jnp
from jax import lax
from jax.experimental import pallas as pl
from jax.experimental.pallas import tpu as pltpu

F32 = jnp.float32
BF16 = jnp.bfloat16
HIGHEST = lax.Precision.HIGHEST

GRID_W = 64
N_MOD = 6
NORM_EPS = 1e-6
CHUNK = 128
GM_GROUPS = 8
HEAD_DIM = 128
N_KV_HEADS = 2
AXIS_DIM = HEAD_DIM // 2
ROPE_THETA = 10000.0
TOP_K = 4
SWIGLU_LIMIT = 7.0
SWIGLU_ALPHA = 1.702

TM = 256
TG = 512
TE = 512
EXPERT_COLS = 256
ATTN_HEADS_PER_DOT = 2
MOD_ROWS = 16
V7X_VMEM_LIMIT = 56 * 1024 * 1024


def _cparams(sem, vmem=V7X_VMEM_LIMIT):
    return pltpu.CompilerParams(dimension_semantics=sem, vmem_limit_bytes=vmem)


def _const_spec(shape):
    nd = len(shape)
    return pl.BlockSpec(shape, lambda *_: (0,) * nd, pipeline_mode=pl.Buffered(1))


LANES = 128
SUBLANES = 8


def _load_tile_rows(ref, n):
    return jnp.concatenate([ref[pl.ds(s, n, stride=SUBLANES), :] for s in range(SUBLANES)], axis=1)


def _store_tile_rows(ref, val):
    n = val.shape[0]
    for s in range(SUBLANES):
        ref[pl.ds(s, n, stride=SUBLANES), :] = val[:, s * LANES:(s + 1) * LANES]


def _sigmoid(x):
    return 1.0 / (1.0 + jnp.exp(-x))


def _rms(x):
    return x * lax.rsqrt(jnp.mean(x * x, axis=-1, keepdims=True) + NORM_EPS)


def _ada_kernel(s_ref, w_ref, b_ref, o_ref):
    s = s_ref[...]
    s = s * _sigmoid(s)
    o_ref[0] = jnp.dot(s, w_ref[0], precision=HIGHEST, preferred_element_type=F32) + b_ref[0]


def _ada_table(c, c_ctx, ada_w, ada_b):
    depth, d, n = ada_w.shape
    b = c.shape[0]
    s = jnp.concatenate([c, c_ctx[None, :], jnp.zeros((MOD_ROWS - b - 1, d), F32)], axis=0)
    tn = 1536
    return pl.pallas_call(
        _ada_kernel,
        grid=(depth, n // tn),
        in_specs=[
            pl.BlockSpec((MOD_ROWS, d), lambda i, j: (0, 0)),
            pl.BlockSpec((1, d, tn), lambda i, j: (i, 0, j)),
            pl.BlockSpec((1, 1, tn), lambda i, j: (i, 0, j)),
        ],
        out_specs=pl.BlockSpec((1, MOD_ROWS, tn), lambda i, j: (i, 0, j)),
        out_shape=jax.ShapeDtypeStruct((depth, MOD_ROWS, n), F32),
        compiler_params=_cparams(("arbitrary", "arbitrary")),
        name="ada_table",
    )(s, ada_w, ada_b.reshape(depth, 1, n))


def _mod_slices(mod_ref, row, d, first):
    return [mod_ref[pl.ds(row, 1), pl.ds((first + k) * d, d)] for k in range(3)]


def _router_epilogue(step, x_new, sh2, sc2, nf_ref, rwt_ref, rb_ref,
                     f_ref, idx_ref, gate_ref, rank_ref, cnt_ref, base_ref):
    tm = x_new.shape[0]
    n_exp = rwt_ref.shape[0] // 2
    f = _rms(x_new) * nf_ref[...] * (1.0 + sc2) + sh2
    _store_tile_rows(f_ref, f)

    nt = (((1,), (1,)), ((), ()))
    f_hi = f.astype(BF16)
    f_lo = (f - f_hi.astype(F32)).astype(BF16)
    l_hi = lax.dot_general(rwt_ref[...], f_hi, nt, preferred_element_type=F32)
    l_lo = lax.dot_general(rwt_ref[:n_exp, :], f_lo, nt, preferred_element_type=F32)
    logits = l_hi[:n_exp, :] + l_hi[n_exp:, :] + l_lo + rb_ref[...]
    eid = lax.broadcasted_iota(jnp.int32, (n_exp, tm), 0).astype(F32)

    @pl.when(step == 0)
    def _():
        base_ref[...] = jnp.zeros_like(base_ref)

    r_io = lax.broadcasted_iota(jnp.int32, (tm, tm), 0)
    c_io = lax.broadcasted_iota(jnp.int32, (tm, tm), 1)
    before = jnp.where(r_io < c_io, 1.0, 0.0).astype(BF16)
    ones = jnp.ones((tm, tm), BF16)

    vals, idxs, hits = [], [], []
    l = logits
    for _ in range(TOP_K):
        m = jnp.max(l, axis=0, keepdims=True)
        sel = jnp.min(jnp.where(l == m, eid, float(n_exp)), axis=0, keepdims=True)
        hit = eid == sel
        l = jnp.where(hit, -jnp.inf, l)
        vals.append(m)
        idxs.append(sel)
        hits.append(hit)
    onehot = jnp.concatenate([jnp.where(h, 1.0, 0.0) for h in hits], axis=0).astype(BF16)
    prefix = jnp.dot(onehot, before, preferred_element_type=F32)
    count = jnp.dot(onehot, ones, preferred_element_type=F32)
    base = base_ref[...]
    ranks = []
    for k, hit in enumerate(hits):
        pk = prefix[k * n_exp:(k + 1) * n_exp, :]
        ranks.append(jnp.sum(jnp.where(hit, base + pk, 0.0), axis=0, keepdims=True))
        base = base + count[k * n_exp:(k + 1) * n_exp, :]
    base_ref[...] = base
    es = [jnp.exp(v - vals[0]) for v in vals]
    tot = es[0] + es[1] + es[2] + es[3]
    zero = jnp.zeros_like(tot)
    gate_ref[...] = jnp.concatenate([e / tot for e in es] + [zero] * (8 - TOP_K), axis=0)
    idx_ref[...] = jnp.concatenate(idxs, axis=0).astype(jnp.int32)
    rank_ref[...] = jnp.concatenate(ranks, axis=0).astype(jnp.int32)
    cnt_ref[...] = base_ref[:, :128].astype(jnp.int32)


def _router_weights(rw):
    hi = rw.T.astype(BF16)
    lo = (rw.T - hi.astype(F32)).astype(BF16)
    return jnp.concatenate([hi, lo], axis=0)


def _router_out(t, d, n_exp):
    shapes = (
        jax.ShapeDtypeStruct((t * SUBLANES, LANES), F32),
        jax.ShapeDtypeStruct((TOP_K, t), jnp.int32),
        jax.ShapeDtypeStruct((8, t), F32),
        jax.ShapeDtypeStruct((TOP_K, t), jnp.int32),
        jax.ShapeDtypeStruct((n_exp, 128), jnp.int32),
    )
    return shapes


def _router_out_specs(tile_of, tm, d, n_exp):
    return (
        pl.BlockSpec((tm * SUBLANES, LANES), lambda *g: (tile_of(*g), 0)),
        pl.BlockSpec((TOP_K, tm), lambda *g: (0, tile_of(*g))),
        pl.BlockSpec((8, tm), lambda *g: (0, tile_of(*g))),
        pl.BlockSpec((TOP_K, tm), lambda *g: (0, tile_of(*g))),
        pl.BlockSpec((n_exp, 128), lambda *g: (0, 0)),
    )


def _gmlp_kernel(n_ctx_tiles, tiles_per_sample, n_samples,
                 x_ref, mod_ref, nm_ref, win_ref, bin_ref, vg_ref, ws_ref, bs_ref, wout_ref,
                 nf_ref, rwt_ref, rb_ref,
                 xo_ref, f_ref, idx_ref, gate_ref, rank_ref, cnt_ref, base_ref):
    i = pl.program_id(0)
    d = x_ref.shape[1]
    gw = wout_ref.shape[0]
    gc = gw // GM_GROUPS
    row = jnp.where(i < n_ctx_tiles, n_samples, (i - n_ctx_tiles) // tiles_per_sample)
    sh1, sc1, g1 = _mod_slices(mod_ref, row, d, 0)
    sh2, sc2, _ = _mod_slices(mod_ref, row, d, 3)

    x = x_ref[...]
    h = _rms(x) * nm_ref[...] * (1.0 + sc1) + sh1
    hb = h.astype(BF16)

    def proj(col):
        a = jnp.dot(hb, win_ref[:, col:col + gc], preferred_element_type=F32) + bin_ref[:, col:col + gc]
        return 0.5 * a * (1.0 + lax.erf(a * (2.0 ** -0.5)))

    u = [proj(g * gc) for g in range(GM_GROUPS)]
    v = [proj(gw + g * gc) for g in range(GM_GROUPS)]
    ssq = v[0] * v[0]
    for g in range(1, GM_GROUPS):
        ssq = ssq + v[g] * v[g]
    inv = lax.rsqrt(jnp.sum(ssq, axis=-1, keepdims=True) * (1.0 / gw) + NORM_EPS)
    cols = []
    for g in range(GM_GROUPS):
        vn = (v[g] * inv * vg_ref[:, g * gc:(g + 1) * gc]).astype(BF16)
        s = jnp.concatenate(
            [jnp.dot(ws_ref[g], vn[c * CHUNK:(c + 1) * CHUNK, :], preferred_element_type=F32)
             + bs_ref[:, g * gc:(g + 1) * gc] for c in range(x.shape[0] // CHUNK)], axis=0)
        cols.append((u[g] * s).astype(BF16))
    z = jnp.concatenate(cols, axis=1)
    y = jnp.dot(z, wout_ref[...], preferred_element_type=F32)
    x_new = x + g1 * y
    xo_ref[...] = x_new
    _router_epilogue(i, x_new, sh2, sc2, nf_ref, rwt_ref, rb_ref,
                     f_ref, idx_ref, gate_ref, rank_ref, cnt_ref, base_ref)


def _gmlp_layer(x_all, mod, n_ctx_tiles, tiles_per_sample, n_samples,
                nm, w_in, b_in, v_gain, w_s, b_s, w_out, nf, rw, rb):
    t, d = x_all.shape
    gw = w_out.shape[0]
    n_exp = rw.shape[1]
    gc = gw // GM_GROUPS
    bs_full = jnp.repeat(b_s.T, gc, axis=1)
    tile = lambda i: i
    outs = pl.pallas_call(
        functools.partial(_gmlp_kernel, n_ctx_tiles, tiles_per_sample, n_samples),
        grid=(t // TG,),
        in_specs=[
            pl.BlockSpec((TG, d), lambda i: (i, 0)),
            _const_spec((MOD_ROWS, N_MOD * d)),
            _const_spec((1, d)),
            _const_spec((d, 2 * gw)),
            _const_spec((1, 2 * gw)),
            _const_spec((1, gw)),
            _const_spec((GM_GROUPS, CHUNK, CHUNK)),
            _const_spec((CHUNK, gw)),
            _const_spec((gw, d)),
            _const_spec((1, d)),
            _const_spec((2 * n_exp, d)),
            _const_spec((n_exp, 1)),
        ],
        out_specs=(pl.BlockSpec((TG, d), lambda i: (i, 0)),) + _router_out_specs(tile, TG, d, n_exp),
        out_shape=(jax.ShapeDtypeStruct((t, d), F32),) + _router_out(t, d, n_exp),
        scratch_shapes=[pltpu.VMEM((n_exp, TG), F32)],
        compiler_params=_cparams(("arbitrary",)),
        name="gmlp_mixer",
    )(x_all, mod, nm.reshape(1, d), w_in.astype(BF16), b_in.reshape(1, -1), v_gain.reshape(1, gw),
      w_s.astype(BF16), bs_full, w_out.astype(BF16), nf.reshape(1, d), _router_weights(rw), rb.reshape(n_exp, 1))
    return outs


def _qkv_kernel(n_ctx_tiles, tiles_per_sample, n_samples,
                x_ref, mod_ref, nm_ref, w_ref, qg_ref, kg_ref, cos_ref, sin_ref,
                q_ref, k_ref, v_ref):
    i = pl.program_id(0)
    d = x_ref.shape[1]
    nq = q_ref.shape[1]
    nkv = k_ref.shape[2]
    row = jnp.where(i < n_ctx_tiles, n_samples, (i - n_ctx_tiles) // tiles_per_sample)
    sh1, sc1, _ = _mod_slices(mod_ref, row, d, 0)
    h = _rms(x_ref[...]) * nm_ref[...] * (1.0 + sc1) + sh1
    qkv = jnp.dot(h.astype(BF16), w_ref[...], preferred_element_type=F32)
    cos = cos_ref[...]
    sin = sin_ref[...]
    half = AXIS_DIM // 2
    lane = lax.broadcasted_iota(jnp.int32, (x_ref.shape[0], HEAD_DIM), 1)
    first_half = (lane % AXIS_DIM) < half

    def head(xh, gain):
        xh = _rms(xh) * gain
        partner = jnp.where(first_half, pltpu.roll(xh, HEAD_DIM - half, 1), pltpu.roll(xh, half, 1))
        return xh * cos + partner * sin

    q = [head(qkv[:, j * HEAD_DIM:(j + 1) * HEAD_DIM], qg_ref[...]) for j in range(nq // HEAD_DIM)]
    k = [head(qkv[:, nq + j * HEAD_DIM:nq + (j + 1) * HEAD_DIM], kg_ref[...])
         for j in range(nkv // HEAD_DIM)]
    q_ref[...] = jnp.concatenate(q, axis=1).astype(BF16)
    k_ref[0] = jnp.concatenate(k, axis=1).astype(BF16)
    v_ref[0] = qkv[:, nq + nkv:].astype(BF16)


def _rope_tables(n_lat):
    rows = n_lat // GRID_W
    row = jnp.repeat(jnp.arange(rows, dtype=jnp.int32), GRID_W).astype(F32)
    col = jnp.tile(jnp.arange(GRID_W, dtype=jnp.int32), rows).astype(F32)
    inv_freq = 1.0 / (ROPE_THETA ** (jnp.arange(0, AXIS_DIM, 2, dtype=F32) / AXIS_DIM))
    ang_r = row[:, None] * inv_freq
    ang_c = col[:, None] * inv_freq
    cos = jnp.concatenate([jnp.cos(ang_r)] * 2 + [jnp.cos(ang_c)] * 2, axis=1)
    sin = jnp.concatenate([-jnp.sin(ang_r), jnp.sin(ang_r), -jnp.sin(ang_c), jnp.sin(ang_c)], axis=1)
    cos = jnp.concatenate([jnp.ones((TM, HEAD_DIM), F32), cos], axis=0)
    sin = jnp.concatenate([jnp.zeros((TM, HEAD_DIM), F32), sin], axis=0)
    return cos, sin


def _qkv_layer(x_all, mod, n_ctx_tiles, tiles_per_sample, n_samples, n_ctx, n_lat,
               nm, w_qkv, q_gain, k_gain):
    t, d = x_all.shape
    nqkv = w_qkv.shape[1]
    nkv = N_KV_HEADS * HEAD_DIM
    nq = nqkv - 2 * nkv
    cos, sin = _rope_tables(n_lat)
    ctx_blocks = n_ctx // TM
    tps = tiles_per_sample

    def pos_block(i):
        return jnp.where(i < n_ctx_tiles, 0, 1 + (i - n_ctx_tiles) % tps)

    def kv_map(i):
        lat = i >= n_ctx_tiles
        b = jnp.where(lat, (i - n_ctx_tiles) // tps, i // ctx_blocks)
        j = jnp.where(lat, ctx_blocks + (i - n_ctx_tiles) % tps, i % ctx_blocks)
        return (b, j, 0)

    return pl.pallas_call(
        functools.partial(_qkv_kernel, n_ctx_tiles, tiles_per_sample, n_samples),
        grid=(t // TM,),
        in_specs=[
            pl.BlockSpec((TM, d), lambda i: (i, 0)),
            _const_spec((MOD_ROWS, N_MOD * d)),
            _const_spec((1, d)),
            _const_spec((d, nqkv)),
            _const_spec((1, HEAD_DIM)),
            _const_spec((1, HEAD_DIM)),
            pl.BlockSpec((TM, HEAD_DIM), lambda i: (pos_block(i), 0)),
            pl.BlockSpec((TM, HEAD_DIM), lambda i: (pos_block(i), 0)),
        ],
        out_specs=(
            pl.BlockSpec((TM, nq), lambda i: (i, 0)),
            pl.BlockSpec((1, TM, nkv), kv_map),
            pl.BlockSpec((1, TM, nkv), kv_map),
        ),
        out_shape=(
            jax.ShapeDtypeStruct((t, nq), BF16),
            jax.ShapeDtypeStruct((n_samples, n_ctx + n_lat, nkv), BF16),
            jax.ShapeDtypeStruct((n_samples, n_ctx + n_lat, nkv), BF16),
        ),
        compiler_params=_cparams(("arbitrary",)),
        name="qkv_rope",
    )(x_all, mod, nm.reshape(1, d), w_qkv.astype(BF16), q_gain.reshape(1, HEAD_DIM),
      k_gain.reshape(1, HEAD_DIM), cos, sin)


def _attn_kernel(tiles_per_sample,
                 x_ref, q_ref, k_ref, v_ref, mod_ref, wo_ref, nf_ref, rwt_ref, rb_ref,
                 xo_ref, f_ref, idx_ref, gate_ref, rank_ref, cnt_ref, base_ref):
    b = pl.program_id(0)
    j = pl.program_id(1)
    d = x_ref.shape[1]
    n_heads = q_ref.shape[1] // HEAD_DIM
    group = n_heads // N_KV_HEADS
    _, _, g1 = _mod_slices(mod_ref, b, d, 0)
    sh2, sc2, _ = _mod_slices(mod_ref, b, d, 3)
    tq = q_ref.shape[0]
    exp2_scale = (HEAD_DIM ** -0.5) * 1.4426950408889634
    outs = []
    hpd = ATTN_HEADS_PER_DOT
    for h0 in range(0, n_heads, hpd):
        g = h0 // group
        qg = jnp.concatenate([q_ref[:, h * HEAD_DIM:(h + 1) * HEAD_DIM]
                              for h in range(h0, h0 + hpd)], axis=0)
        kg = k_ref[0, :, g * HEAD_DIM:(g + 1) * HEAD_DIM]
        vg = v_ref[0, :, g * HEAD_DIM:(g + 1) * HEAD_DIM]
        s = lax.dot_general(qg, kg, (((1,), (1,)), ((), ())), preferred_element_type=F32)
        p = jnp.exp2((s - jnp.max(s, axis=-1, keepdims=True)) * exp2_scale)
        den = jnp.sum(p, axis=-1, keepdims=True)
        o = jnp.dot(p.astype(BF16), vg, preferred_element_type=F32) / den
        outs += [o[h * tq:(h + 1) * tq, :].astype(BF16) for h in range(hpd)]
    o_all = jnp.concatenate(outs, axis=1)
    y = jnp.dot(o_all, wo_ref[...], preferred_element_type=F32)
    x_new = x_ref[...] + g1 * y
    xo_ref[...] = x_new
    _router_epilogue(b * tiles_per_sample + j, x_new, sh2, sc2, nf_ref, rwt_ref, rb_ref,
                     f_ref, idx_ref, gate_ref, rank_ref, cnt_ref, base_ref)


def _attn_layer(x_all, q, k_all, v_all, mod, n_ctx_tiles, tiles_per_sample, n_samples,
                w_o, nf, rw, rb):
    d = x_all.shape[1]
    t_lat = n_samples * tiles_per_sample * TM
    n_exp = rw.shape[1]
    nq = q.shape[1]
    lk, nkv = k_all.shape[1], k_all.shape[2]
    tps = tiles_per_sample
    lat_tile = lambda b, j: n_ctx_tiles + b * tps + j
    tile = lambda b, j: b * tps + j
    return pl.pallas_call(
        functools.partial(_attn_kernel, tiles_per_sample),
        grid=(n_samples, tps),
        in_specs=[
            pl.BlockSpec((TM, d), lambda b, j: (lat_tile(b, j), 0)),
            pl.BlockSpec((TM, nq), lambda b, j: (lat_tile(b, j), 0)),
            pl.BlockSpec((1, lk, nkv), lambda b, j: (b, 0, 0)),
            pl.BlockSpec((1, lk, nkv), lambda b, j: (b, 0, 0)),
            _const_spec((MOD_ROWS, N_MOD * d)),
            _const_spec((nq, d)),
            _const_spec((1, d)),
            _const_spec((2 * n_exp, d)),
            _const_spec((n_exp, 1)),
        ],
        out_specs=(pl.BlockSpec((TM, d), lambda b, j: (tile(b, j), 0)),)
        + _router_out_specs(tile, TM, d, n_exp),
        out_shape=(jax.ShapeDtypeStruct((t_lat, d), F32),) + _router_out(t_lat, d, n_exp),
        scratch_shapes=[pltpu.VMEM((n_exp, TM), F32)],
        compiler_params=_cparams(("arbitrary", "arbitrary")),
        name="attn_mixer",
    )(x_all, q, k_all, v_all, mod, w_o.astype(BF16), nf.reshape(1, d), _router_weights(rw), rb.reshape(n_exp, 1))


def _dispatch_kernel(off_ref, idx_ref, rank_ref, f_ref, xs_hbm, dest_ref, sem):
    tm = idx_ref.shape[1]

    def start(t, c):
        src = f_ref.at[pl.ds(pl.multiple_of(t * SUBLANES, SUBLANES), SUBLANES)]
        for k in range(TOP_K):
            dst = off_ref[idx_ref[k, t]] + rank_ref[k, t]
            dest_ref[k, t] = dst
            row = xs_hbm.at[pl.ds(pl.multiple_of(dst * SUBLANES, SUBLANES), SUBLANES)]
            pltpu.make_async_copy(src, row, sem).start(priority=k % 2)
        return c

    lax.fori_loop(0, tm, start, 0)
    for k in range(TOP_K):
        pltpu.make_async_copy(f_ref, xs_hbm.at[pl.ds(0, tm * SUBLANES)], sem).wait()


def _dispatch(f, idx, rank, off_start):
    t = idx.shape[1]
    smem_tile = pl.BlockSpec((TOP_K, TM), lambda i, off: (0, i), memory_space=pltpu.SMEM)
    grid_spec = pltpu.PrefetchScalarGridSpec(
        num_scalar_prefetch=1,
        grid=(t // TM,),
        in_specs=[smem_tile, smem_tile, pl.BlockSpec((TM * SUBLANES, LANES), lambda i, off: (i, 0))],
        out_specs=(pl.BlockSpec(memory_space=pl.ANY), smem_tile),
        scratch_shapes=[pltpu.SemaphoreType.DMA],
    )
    return pl.pallas_call(
        _dispatch_kernel,
        grid_spec=grid_spec,
        out_shape=(jax.ShapeDtypeStruct((t * TOP_K * SUBLANES, LANES), F32),
                   jax.ShapeDtypeStruct((TOP_K, t), jnp.int32)),
        compiler_params=_cparams(("arbitrary",)),
        name="moe_dispatch",
    )(off_start, idx, rank, f)


def _expert_kernel(blk_ref, exp_ref, lo_ref, hi_ref,
                   xs_ref, wgu_ref, bgu_ref, wd_ref, bd_ref, o_ref, wgu_b, wd_b):
    w = pl.program_id(0)
    tm = xs_ref.shape[0] // SUBLANES
    d = SUBLANES * LANES
    de = wd_ref.shape[2]
    prev = jnp.maximum(w - 1, 0)
    lo = lo_ref[w]
    hi = hi_ref[w]
    blk = blk_ref[w]
    new_expert = (w == 0) | (exp_ref[w] != exp_ref[prev])
    first_visit = (w == 0) | (blk != blk_ref[prev])

    @pl.when(new_expert)
    def _():
        wgu_b[...] = wgu_ref[0, 0].astype(BF16)
        wd_b[...] = wd_ref[0, 0].astype(BF16)

    @pl.when(first_visit & (hi <= lo))
    def _():
        o_ref[...] = jnp.zeros_like(o_ref)

    @pl.when(hi > lo)
    def _():
        x = _load_tile_rows(xs_ref, tm).astype(BF16)
        bgu = bgu_ref[0]
        acts = []
        for c0 in range(0, de, EXPERT_COLS):
            g = jnp.dot(x, wgu_b[:, c0:c0 + EXPERT_COLS], preferred_element_type=F32)
            u = jnp.dot(x, wgu_b[:, de + c0:de + c0 + EXPERT_COLS], preferred_element_type=F32)
            g = jnp.minimum(g + bgu[:, c0:c0 + EXPERT_COLS], SWIGLU_LIMIT)
            u = jnp.clip(u + bgu[:, de + c0:de + c0 + EXPERT_COLS], -SWIGLU_LIMIT, SWIGLU_LIMIT)
            acts.append((g * _sigmoid(SWIGLU_ALPHA * g) * (u + 1.0)).astype(BF16))
        act = jnp.concatenate(acts, axis=1)
        y = jnp.dot(act, wd_b[...], preferred_element_type=F32) + bd_ref[0]
        rows = blk * tm + lax.broadcasted_iota(jnp.int32, (tm, d), 0)
        mine = (rows >= lo) & (rows < hi)

        @pl.when(first_visit)
        def _():
            _store_tile_rows(o_ref, jnp.where(mine, y, 0.0))

        @pl.when(jnp.logical_not(first_visit))
        def _():
            _store_tile_rows(o_ref, jnp.where(mine, y, _load_tile_rows(o_ref, tm)))


def _experts(xs, work, layer, w_gu, b_gu, w_down, b_down):
    _, n_exp, d, de2 = w_gu.shape
    de = w_down.shape[2]
    n_work = work[0].shape[0]
    grid_spec = pltpu.PrefetchScalarGridSpec(
        num_scalar_prefetch=4,
        grid=(n_work,),
        in_specs=[
            pl.BlockSpec((TE * SUBLANES, LANES), lambda w, blk, ex, lo, hi: (blk[w], 0)),
            pl.BlockSpec((1, 1, d, de2), lambda w, blk, ex, lo, hi: (layer, ex[w], 0, 0)),
            pl.BlockSpec((1, 1, de2), lambda w, blk, ex, lo, hi: (ex[w], 0, 0)),
            pl.BlockSpec((1, 1, de, d), lambda w, blk, ex, lo, hi: (layer, ex[w], 0, 0)),
            pl.BlockSpec((1, 1, d), lambda w, blk, ex, lo, hi: (ex[w], 0, 0)),
        ],
        out_specs=pl.BlockSpec((TE * SUBLANES, LANES), lambda w, blk, ex, lo, hi: (blk[w], 0)),
        scratch_shapes=[pltpu.VMEM((d, de2), BF16), pltpu.VMEM((de, d), BF16)],
    )
    return pl.pallas_call(
        _expert_kernel,
        grid_spec=grid_spec,
        out_shape=jax.ShapeDtypeStruct(xs.shape, F32),
        compiler_params=_cparams(("arbitrary",)),
        name="moe_experts",
    )(*work, xs, w_gu, b_gu[layer].reshape(n_exp, 1, de2), w_down, b_down[layer].reshape(n_exp, 1, d))


def _combine_kernel(row_of_tile, dest_ref, gate_ref, x_ref, mod_ref, yb_hbm, o_ref, buf, sem):
    i = pl.program_id(0)
    tm, d = x_ref.shape

    def start(t, c):
        slot = pl.ds(pl.multiple_of(t * SUBLANES, SUBLANES), SUBLANES)
        for k in range(TOP_K):
            row = pl.ds(pl.multiple_of(dest_ref[k, t] * SUBLANES, SUBLANES), SUBLANES)
            pltpu.make_async_copy(yb_hbm.at[row], buf.at[k, slot], sem).start(priority=k % 2)
        return c

    lax.fori_loop(0, tm, start, 0)
    g2 = mod_ref[pl.ds(row_of_tile(i), 1), pl.ds(5 * d, d)]
    gates = jnp.concatenate([gate_ref[...], jnp.zeros((LANES - 8, tm), F32)], axis=0).T
    for k in range(TOP_K):
        pltpu.make_async_copy(yb_hbm.at[pl.ds(0, tm * SUBLANES)], buf.at[k], sem).wait()
    y = _load_tile_rows(buf.at[0], tm) * gates[:, 0:1]
    for k in range(1, TOP_K):
        y = y + _load_tile_rows(buf.at[k], tm) * gates[:, k:k + 1]
    o_ref[...] = x_ref[...] + g2 * y


def _combine(x_new, yb, dest, gates, mod, row_of_tile):
    t, d = x_new.shape
    return pl.pallas_call(
        functools.partial(_combine_kernel, row_of_tile),
        grid=(t // TM,),
        in_specs=[
            pl.BlockSpec((TOP_K, TM), lambda i: (0, i), memory_space=pltpu.SMEM),
            pl.BlockSpec((8, TM), lambda i: (0, i)),
            pl.BlockSpec((TM, d), lambda i: (i, 0)),
            _const_spec((MOD_ROWS, N_MOD * d)),
            pl.BlockSpec(memory_space=pl.ANY),
        ],
        out_specs=pl.BlockSpec((TM, d), lambda i: (i, 0)),
        out_shape=jax.ShapeDtypeStruct((t, d), F32),
        scratch_shapes=[pltpu.VMEM((TOP_K, TM * SUBLANES, LANES), F32), pltpu.SemaphoreType.DMA],
        compiler_params=_cparams(("arbitrary",)),
        name="moe_combine",
    )(dest, gates, x_new, mod, yb)


def _work_items(counts, n_assign):
    n_exp = counts.shape[0]
    n_blocks = n_assign // TE
    n_work = n_blocks + n_exp - 1
    off_end = jnp.cumsum(counts)
    off_start = off_end - counts
    first_blk = off_start // TE
    n_items = jnp.where(counts > 0, (off_end - 1) // TE - first_blk + 1, 0)
    item_end = jnp.cumsum(n_items)
    item_start = item_end - n_items
    w = jnp.arange(n_work, dtype=jnp.int32)
    ex = jnp.minimum(jnp.sum(w[:, None] >= item_end[None, :], axis=1), n_exp - 1).astype(jnp.int32)
    used = w < item_end[-1]
    blk = jnp.where(used, first_blk[ex] + w - item_start[ex], n_blocks - 1).astype(jnp.int32)
    lo = jnp.where(used, jnp.maximum(off_start[ex], blk * TE), 0).astype(jnp.int32)
    hi = jnp.where(used, jnp.minimum(off_end[ex], (blk + 1) * TE), 0).astype(jnp.int32)
    return (blk, ex, lo, hi), off_start.astype(jnp.int32)


def _moe(x_new, f, idx, gates, rank, cnt, mod, row_of_tile, layer, w_gu, b_gu, w_down, b_down):
    t = idx.shape[1]
    work, off_start = _work_items(cnt[:, 0], t * TOP_K)
    xs, dest = _dispatch(f, idx, rank, off_start)
    yb = _experts(xs, work, layer, w_gu, b_gu, w_down, b_down)
    return _combine(x_new, yb, dest, gates, mod, row_of_tile)


def kernel(x, c, ctx, c_ctx, ada_w, ada_b, norm_mix, norm_ffn, gm_w_in, gm_b_in, gm_v_gain, gm_w_s,
           gm_b_s, gm_w_out, at_w_qkv, at_q_gain, at_k_gain, at_w_o, moe_router_w, moe_router_b,
           moe_w_gu, moe_b_gu, moe_w_down, moe_b_down):
    n_samples, n_lat, d = x.shape
    n_ctx = ctx.shape[1]
    assert d == SUBLANES * LANES, "MoE row movement assumes one f32 tile per token row"
    assert n_lat % TM == 0 and n_ctx % TM == 0 and n_samples < MOD_ROWS
    assert n_lat % TG == 0 and (n_samples * n_ctx) % TG == 0 and TG % CHUNK == 0
    assert (n_samples * n_lat * TOP_K) % TE == 0 and (n_samples * n_ctx * TOP_K) % TE == 0
    n_ctx_tiles = n_samples * n_ctx // TM
    tps = n_lat // TM

    mods = _ada_table(c, c_ctx, ada_w, ada_b)
    x_all = jnp.concatenate([ctx.reshape(-1, d), x.reshape(-1, d)], axis=0)

    def row_all(i):
        return jnp.where(i < n_ctx_tiles, n_samples, (i - n_ctx_tiles) // tps)

    x_new, f, idx, gates, rank, cnt = _gmlp_layer(
        x_all, mods[0], n_samples * n_ctx // TG, n_lat // TG, n_samples, norm_mix[0], gm_w_in[0], gm_b_in[0],
        gm_v_gain[0], gm_w_s[0], gm_b_s[0], gm_w_out[0], norm_ffn[0], moe_router_w[0],
        moe_router_b[0])
    x_all = _moe(x_new, f, idx, gates, rank, cnt, mods[0], row_all,
                 0, moe_w_gu, moe_b_gu, moe_w_down, moe_b_down)

    q, k_all, v_all = _qkv_layer(x_all, mods[1], n_ctx_tiles, tps, n_samples, n_ctx, n_lat,
                                 norm_mix[1], at_w_qkv[0], at_q_gain[0], at_k_gain[0])
    x_new, f, idx, gates, rank, cnt = _attn_layer(
        x_all, q, k_all, v_all, mods[1], n_ctx_tiles, tps, n_samples, at_w_o[0], norm_ffn[1],
        moe_router_w[1], moe_router_b[1])
    out = _moe(x_new, f, idx, gates, rank, cnt, mods[1], lambda i: i // tps,
               1, moe_w_gu, moe_b_gu, moe_w_down, moe_b_down)
    return out.reshape(n_samples, n_lat, d)
```

```python
import functools

import jax
import jax.numpy as jnp
from jax import lax
from jax.experimental import pallas as pl
from jax.experimental.pallas import tpu as pltpu

F32 = jnp.float32
BF16 = jnp.bfloat16
HIGHEST = lax.Precision.HIGHEST

GRID_W = 64
N_MOD = 6
NORM_EPS = 1e-6
CHUNK = 128
GM_GROUPS = 8
HEAD_DIM = 128
N_KV_HEADS = 2
AXIS_DIM = HEAD_DIM // 2
ROPE_THETA = 10000.0
TOP_K = 4
SWIGLU_LIMIT = 7.0
SWIGLU_ALPHA = 1.702

TM = 256
TG = 512
TA = 512
TE = 512
ATTN_HEADS_PER_DOT = 2
MOD_ROWS = 16
V7X_VMEM_LIMIT = 56 * 1024 * 1024


def _cparams(sem, vmem=V7X_VMEM_LIMIT):
    return pltpu.CompilerParams(dimension_semantics=sem, vmem_limit_bytes=vmem)


def _const_spec(shape):
    nd = len(shape)
    return pl.BlockSpec(shape, lambda *_: (0,) * nd, pipeline_mode=pl.Buffered(1))


LANES = 128
SUBLANES = 8


def _load_tile_rows(ref, n):
    return jnp.concatenate([ref[pl.ds(s, n, stride=SUBLANES), :] for s in range(SUBLANES)], axis=1)


def _store_tile_rows(ref, val):
    n = val.shape[0]
    for s in range(SUBLANES):
        ref[pl.ds(s, n, stride=SUBLANES), :] = val[:, s * LANES:(s + 1) * LANES]


def _sigmoid(x):
    return 1.0 / (1.0 + jnp.exp(-x))


def _rms(x):
    return x * lax.rsqrt(jnp.mean(x * x, axis=-1, keepdims=True) + NORM_EPS)


def _ada_kernel(s_ref, w_ref, b_ref, o_ref):
    s = s_ref[...]
    s = s * _sigmoid(s)
    o_ref[0] = jnp.dot(s, w_ref[0], precision=HIGHEST, preferred_element_type=F32) + b_ref[0]


def _ada_table(c, c_ctx, ada_w, ada_b):
    depth, d, n = ada_w.shape
    b = c.shape[0]
    s = jnp.concatenate([c, c_ctx[None, :], jnp.zeros((MOD_ROWS - b - 1, d), F32)], axis=0)
    tn = 1536
    return pl.pallas_call(
        _ada_kernel,
        grid=(depth, n // tn),
        in_specs=[
            pl.BlockSpec((MOD_ROWS, d), lambda i, j: (0, 0)),
            pl.BlockSpec((1, d, tn), lambda i, j: (i, 0, j)),
            pl.BlockSpec((1, 1, tn), lambda i, j: (i, 0, j)),
        ],
        out_specs=pl.BlockSpec((1, MOD_ROWS, tn), lambda i, j: (i, 0, j)),
        out_shape=jax.ShapeDtypeStruct((depth, MOD_ROWS, n), F32),
        compiler_params=_cparams(("arbitrary", "arbitrary")),
        name="ada_table",
    )(s, ada_w, ada_b.reshape(depth, 1, n))


def _mod_slices(mod_ref, row, d, first):
    return [mod_ref[pl.ds(row, 1), pl.ds((first + k) * d, d)] for k in range(3)]


def _router_epilogue(step, x_new, sh2, sc2, nf_ref, rwt_ref, rb_ref,
                     f_ref, idx_ref, gate_ref, rank_ref, cnt_ref, base_ref):
    tm = x_new.shape[0]
    n_exp = rwt_ref.shape[0] // 2
    f = _rms(x_new) * nf_ref[...] * (1.0 + sc2) + sh2
    _store_tile_rows(f_ref, f)

    nt = (((1,), (1,)), ((), ()))
    f_hi = f.astype(BF16)
    f_lo = (f - f_hi.astype(F32)).astype(BF16)
    l_hi = lax.dot_general(rwt_ref[...], f_hi, nt, preferred_element_type=F32)
    l_lo = lax.dot_general(rwt_ref[:n_exp, :], f_lo, nt, preferred_element_type=F32)
    logits = l_hi[:n_exp, :] + l_hi[n_exp:, :] + l_lo + rb_ref[...]
    eid = lax.broadcasted_iota(jnp.int32, (n_exp, tm), 0).astype(F32)

    @pl.when(step == 0)
    def _():
        base_ref[...] = jnp.zeros_like(base_ref)

    r_io = lax.broadcasted_iota(jnp.int32, (tm, tm), 0)
    c_io = lax.broadcasted_iota(jnp.int32, (tm, tm), 1)
    before = jnp.where(r_io < c_io, 1.0, 0.0).astype(BF16)
    ones = jnp.ones((tm, tm), BF16)

    vals, idxs, hits = [], [], []
    l = logits
    for _ in range(TOP_K):
        m = jnp.max(l, axis=0, keepdims=True)
        sel = jnp.min(jnp.where(l == m, eid, float(n_exp)), axis=0, keepdims=True)
        hit = eid == sel
        l = jnp.where(hit, -jnp.inf, l)
        vals.append(m)
        idxs.append(sel)
        hits.append(hit)
    onehot = jnp.concatenate([jnp.where(h, 1.0, 0.0) for h in hits], axis=0).astype(BF16)
    prefix = jnp.dot(onehot, before, preferred_element_type=F32)
    count = jnp.dot(onehot, ones, preferred_element_type=F32)
    base = base_ref[...]
    ranks = []
    for k, hit in enumerate(hits):
        pk = prefix[k * n_exp:(k + 1) * n_exp, :]
        ranks.append(jnp.sum(jnp.where(hit, base + pk, 0.0), axis=0, keepdims=True))
        base = base + count[k * n_exp:(k + 1) * n_exp, :]
    base_ref[...] = base
    es = [jnp.exp(v - vals[0]) for v in vals]
    tot = es[0] + es[1] + es[2] + es[3]
    zero = jnp.zeros_like(tot)
    gate_ref[...] = jnp.concatenate([e / tot for e in es] + [zero] * (8 - TOP_K), axis=0)
    idx_ref[...] = jnp.concatenate(idxs, axis=0).astype(jnp.int32)
    rank_ref[...] = jnp.concatenate(ranks, axis=0).astype(jnp.int32)
    cnt_ref[...] = base_ref[:, :128].astype(jnp.int32)


def _router_weights(rw):
    hi = rw.T.astype(BF16)
    lo = (rw.T - hi.astype(F32)).astype(BF16)
    return jnp.concatenate([hi, lo], axis=0)


def _router_out(t, d, n_exp):
    shapes = (
        jax.ShapeDtypeStruct((t * SUBLANES, LANES), F32),
        jax.ShapeDtypeStruct((TOP_K, t), jnp.int32),
        jax.ShapeDtypeStruct((8, t), F32),
        jax.ShapeDtypeStruct((TOP_K, t), jnp.int32),
        jax.ShapeDtypeStruct((n_exp, 128), jnp.int32),
    )
    return shapes


def _router_out_specs(tile_of, tm, d, n_exp):
    return (
        pl.BlockSpec((tm * SUBLANES, LANES), lambda *g: (tile_of(*g), 0)),
        pl.BlockSpec((TOP_K, tm), lambda *g: (0, tile_of(*g))),
        pl.BlockSpec((8, tm), lambda *g: (0, tile_of(*g))),
        pl.BlockSpec((TOP_K, tm), lambda *g: (0, tile_of(*g))),
        pl.BlockSpec((n_exp, 128), lambda *g: (0, 0)),
    )


def _gmlp_kernel(n_ctx_tiles, tiles_per_sample, n_samples,
                 x_ref, mod_ref, nm_ref, win_ref, bin_ref, vg_ref, ws_ref, bs_ref, wout_ref,
                 nf_ref, rwt_ref, rb_ref,
                 xo_ref, f_ref, idx_ref, gate_ref, rank_ref, cnt_ref, base_ref):
    i = pl.program_id(0)
    d = x_ref.shape[1]
    gw = wout_ref.shape[0]
    gc = gw // GM_GROUPS
    row = jnp.where(i < n_ctx_tiles, n_samples, (i - n_ctx_tiles) // tiles_per_sample)
    sh1, sc1, g1 = _mod_slices(mod_ref, row, d, 0)
    sh2, sc2, _ = _mod_slices(mod_ref, row, d, 3)

    x = x_ref[...]
    h = _rms(x) * nm_ref[...] * (1.0 + sc1) + sh1
    hb = h.astype(BF16)

    def proj(col):
        a = jnp.dot(hb, win_ref[:, col:col + gc], preferred_element_type=F32) + bin_ref[:, col:col + gc]
        return 0.5 * a * (1.0 + lax.erf(a * (2.0 ** -0.5)))

    u = [proj(g * gc) for g in range(GM_GROUPS)]
    v = [proj(gw + g * gc) for g in range(GM_GROUPS)]
    ssq = v[0] * v[0]
    for g in range(1, GM_GROUPS):
        ssq = ssq + v[g] * v[g]
    inv = lax.rsqrt(jnp.sum(ssq, axis=-1, keepdims=True) * (1.0 / gw) + NORM_EPS)
    cols = []
    for g in range(GM_GROUPS):
        vn = (v[g] * inv * vg_ref[:, g * gc:(g + 1) * gc]).astype(BF16)
        s = jnp.concatenate(
            [jnp.dot(ws_ref[g], vn[c * CHUNK:(c + 1) * CHUNK, :], preferred_element_type=F32)
             + bs_ref[:, g * gc:(g + 1) * gc] for c in range(x.shape[0] // CHUNK)], axis=0)
        cols.append((u[g] * s).astype(BF16))
    z = jnp.concatenate(cols, axis=1)
    y = jnp.dot(z, wout_ref[...], preferred_element_type=F32)
    x_new = x + g1 * y
    xo_ref[...] = x_new
    _router_epilogue(i, x_new, sh2, sc2, nf_ref, rwt_ref, rb_ref,
                     f_ref, idx_ref, gate_ref, rank_ref, cnt_ref, base_ref)


def _gmlp_layer(x_all, mod, n_ctx_tiles, tiles_per_sample, n_samples,
                nm, w_in, b_in, v_gain, w_s, b_s, w_out, nf, rw, rb):
    t, d = x_all.shape
    gw = w_out.shape[0]
    n_exp = rw.shape[1]
    gc = gw // GM_GROUPS
    bs_full = jnp.repeat(b_s.T, gc, axis=1)
    tile = lambda i: i
    outs = pl.pallas_call(
        functools.partial(_gmlp_kernel, n_ctx_tiles, tiles_per_sample, n_samples),
        grid=(t // TG,),
        in_specs=[
            pl.BlockSpec((TG, d), lambda i: (i, 0)),
            _const_spec((MOD_ROWS, N_MOD * d)),
            _const_spec((1, d)),
            _const_spec((d, 2 * gw)),
            _const_spec((1, 2 * gw)),
            _const_spec((1, gw)),
            _const_spec((GM_GROUPS, CHUNK, CHUNK)),
            _const_spec((CHUNK, gw)),
            _const_spec((gw, d)),
            _const_spec((1, d)),
            _const_spec((2 * n_exp, d)),
            _const_spec((n_exp, 1)),
        ],
        out_specs=(pl.BlockSpec((TG, d), lambda i: (i, 0)),) + _router_out_specs(tile, TG, d, n_exp),
        out_shape=(jax.ShapeDtypeStruct((t, d), F32),) + _router_out(t, d, n_exp),
        scratch_shapes=[pltpu.VMEM((n_exp, TG), F32)],
        compiler_params=_cparams(("arbitrary",)),
        name="gmlp_mixer",
    )(x_all, mod, nm.reshape(1, d), w_in.astype(BF16), b_in.reshape(1, -1), v_gain.reshape(1, gw),
      w_s.astype(BF16), bs_full, w_out.astype(BF16), nf.reshape(1, d), _router_weights(rw), rb.reshape(n_exp, 1))
    return outs


def _qkv_kernel(n_ctx_tiles, tiles_per_sample, n_samples,
                x_ref, mod_ref, nm_ref, w_ref, qg_ref, kg_ref, cos_ref, sin_ref,
                q_ref, k_ref, v_ref):
    i = pl.program_id(0)
    d = x_ref.shape[1]
    nq = q_ref.shape[1]
    nkv = k_ref.shape[2]
    row = jnp.where(i < n_ctx_tiles, n_samples, (i - n_ctx_tiles) // tiles_per_sample)
    sh1, sc1, _ = _mod_slices(mod_ref, row, d, 0)
    h = _rms(x_ref[...]) * nm_ref[...] * (1.0 + sc1) + sh1
    qkv = jnp.dot(h.astype(BF16), w_ref[...], preferred_element_type=F32)
    cos = cos_ref[...]
    sin = sin_ref[...]
    half = AXIS_DIM // 2
    lane = lax.broadcasted_iota(jnp.int32, (x_ref.shape[0], HEAD_DIM), 1)
    first_half = (lane % AXIS_DIM) < half

    def head(xh, gain):
        xh = _rms(xh) * gain
        partner = jnp.where(first_half, pltpu.roll(xh, HEAD_DIM - half, 1), pltpu.roll(xh, half, 1))
        return xh * cos + partner * sin

    q = [head(qkv[:, j * HEAD_DIM:(j + 1) * HEAD_DIM], qg_ref[...]) for j in range(nq // HEAD_DIM)]
    k = [head(qkv[:, nq + j * HEAD_DIM:nq + (j + 1) * HEAD_DIM], kg_ref[...])
         for j in range(nkv // HEAD_DIM)]
    q_ref[...] = jnp.concatenate(q, axis=1).astype(BF16)
    k_ref[0] = jnp.concatenate(k, axis=1).astype(BF16)
    v_ref[0] = qkv[:, nq + nkv:].astype(BF16)


def _rope_tables(n_lat):
    rows = n_lat // GRID_W
    row = jnp.repeat(jnp.arange(rows, dtype=jnp.int32), GRID_W).astype(F32)
    col = jnp.tile(jnp.arange(GRID_W, dtype=jnp.int32), rows).astype(F32)
    inv_freq = 1.0 / (ROPE_THETA ** (jnp.arange(0, AXIS_DIM, 2, dtype=F32) / AXIS_DIM))
    ang_r = row[:, None] * inv_freq
    ang_c = col[:, None] * inv_freq
    cos = jnp.concatenate([jnp.cos(ang_r)] * 2 + [jnp.cos(ang_c)] * 2, axis=1)
    sin = jnp.concatenate([-jnp.sin(ang_r), jnp.sin(ang_r), -jnp.sin(ang_c), jnp.sin(ang_c)], axis=1)
    cos = jnp.concatenate([jnp.ones((TM, HEAD_DIM), F32), cos], axis=0)
    sin = jnp.concatenate([jnp.zeros((TM, HEAD_DIM), F32), sin], axis=0)
    return cos, sin


def _qkv_layer(x_all, mod, n_ctx_tiles, tiles_per_sample, n_samples, n_ctx, n_lat,
               nm, w_qkv, q_gain, k_gain):
    t, d = x_all.shape
    nqkv = w_qkv.shape[1]
    nkv = N_KV_HEADS * HEAD_DIM
    nq = nqkv - 2 * nkv
    cos, sin = _rope_tables(n_lat)
    ctx_blocks = n_ctx // TM
    tps = tiles_per_sample

    def pos_block(i):
        return jnp.where(i < n_ctx_tiles, 0, 1 + (i - n_ctx_tiles) % tps)

    def kv_map(i):
        lat = i >= n_ctx_tiles
        b = jnp.where(lat, (i - n_ctx_tiles) // tps, i // ctx_blocks)
        j = jnp.where(lat, ctx_blocks + (i - n_ctx_tiles) % tps, i % ctx_blocks)
        return (b, j, 0)

    return pl.pallas_call(
        functools.partial(_qkv_kernel, n_ctx_tiles, tiles_per_sample, n_samples),
        grid=(t // TM,),
        in_specs=[
            pl.BlockSpec((TM, d), lambda i: (i, 0)),
            _const_spec((MOD_ROWS, N_MOD * d)),
            _const_spec((1, d)),
            _const_spec((d, nqkv)),
            _const_spec((1, HEAD_DIM)),
            _const_spec((1, HEAD_DIM)),
            pl.BlockSpec((TM, HEAD_DIM), lambda i: (pos_block(i), 0)),
            pl.BlockSpec((TM, HEAD_DIM), lambda i: (pos_block(i), 0)),
        ],
        out_specs=(
            pl.BlockSpec((TM, nq), lambda i: (i, 0)),
            pl.BlockSpec((1, TM, nkv), kv_map),
            pl.BlockSpec((1, TM, nkv), kv_map),
        ),
        out_shape=(
            jax.ShapeDtypeStruct((t, nq), BF16),
            jax.ShapeDtypeStruct((n_samples, n_ctx + n_lat, nkv), BF16),
            jax.ShapeDtypeStruct((n_samples, n_ctx + n_lat, nkv), BF16),
        ),
        compiler_params=_cparams(("arbitrary",)),
        name="qkv_rope",
    )(x_all, mod, nm.reshape(1, d), w_qkv.astype(BF16), q_gain.reshape(1, HEAD_DIM),
      k_gain.reshape(1, HEAD_DIM), cos, sin)


def _attn_kernel(tiles_per_sample,
                 x_ref, q_ref, k_ref, v_ref, mod_ref, wo_ref, nf_ref, rwt_ref, rb_ref,
                 xo_ref, f_ref, idx_ref, gate_ref, rank_ref, cnt_ref, base_ref):
    b = pl.program_id(0)
    j = pl.program_id(1)
    d = x_ref.shape[1]
    n_heads = q_ref.shape[1] // HEAD_DIM
    group = n_heads // N_KV_HEADS
    _, _, g1 = _mod_slices(mod_ref, b, d, 0)
    sh2, sc2, _ = _mod_slices(mod_ref, b, d, 3)
    tq = q_ref.shape[0]
    exp2_scale = (HEAD_DIM ** -0.5) * 1.4426950408889634
    outs = []
    hpd = ATTN_HEADS_PER_DOT
    for h0 in range(0, n_heads, hpd):
        g = h0 // group
        qg = jnp.concatenate([q_ref[:, h * HEAD_DIM:(h + 1) * HEAD_DIM]
                              for h in range(h0, h0 + hpd)], axis=0)
        kg = k_ref[0, :, g * HEAD_DIM:(g + 1) * HEAD_DIM]
        vg = v_ref[0, :, g * HEAD_DIM:(g + 1) * HEAD_DIM]
        s = lax.dot_general(qg, kg, (((1,), (1,)), ((), ())), preferred_element_type=F32)
        p = jnp.exp2((s - jnp.max(s, axis=-1, keepdims=True)) * exp2_scale)
        den = jnp.sum(p, axis=-1, keepdims=True)
        pb = p.astype(BF16)
        for h in range(hpd):
            rows = slice(h * tq, (h + 1) * tq)
            o = jnp.dot(pb[rows, :], vg, preferred_element_type=F32) / den[rows, :]
            outs.append(o.astype(BF16))
    o_all = jnp.concatenate(outs, axis=1)
    y = jnp.dot(o_all, wo_ref[...], preferred_element_type=F32)
    x_new = x_ref[...] + g1 * y
    xo_ref[...] = x_new
    _router_epilogue(b * tiles_per_sample + j, x_new, sh2, sc2, nf_ref, rwt_ref, rb_ref,
                     f_ref, idx_ref, gate_ref, rank_ref, cnt_ref, base_ref)


def _attn_layer(x_all, q, k_all, v_all, mod, n_ctx_tiles, tiles_per_sample, n_samples,
                w_o, nf, rw, rb):
    d = x_all.shape[1]
    t_lat = n_samples * tiles_per_sample * TA
    n_exp = rw.shape[1]
    nq = q.shape[1]
    lk, nkv = k_all.shape[1], k_all.shape[2]
    tps = tiles_per_sample
    lat_tile = lambda b, j: n_ctx_tiles + b * tps + j
    tile = lambda b, j: b * tps + j
    return pl.pallas_call(
        functools.partial(_attn_kernel, tiles_per_sample),
        grid=(n_samples, tps),
        in_specs=[
            pl.BlockSpec((TA, d), lambda b, j: (lat_tile(b, j), 0)),
            pl.BlockSpec((TA, nq), lambda b, j: (lat_tile(b, j), 0)),
            pl.BlockSpec((1, lk, nkv), lambda b, j: (b, 0, 0)),
            pl.BlockSpec((1, lk, nkv), lambda b, j: (b, 0, 0)),
            _const_spec((MOD_ROWS, N_MOD * d)),
            _const_spec((nq, d)),
            _const_spec((1, d)),
            _const_spec((2 * n_exp, d)),
            _const_spec((n_exp, 1)),
        ],
        out_specs=(pl.BlockSpec((TA, d), lambda b, j: (tile(b, j), 0)),)
        + _router_out_specs(tile, TA, d, n_exp),
        out_shape=(jax.ShapeDtypeStruct((t_lat, d), F32),) + _router_out(t_lat, d, n_exp),
        scratch_shapes=[pltpu.VMEM((n_exp, TA), F32)],
        compiler_params=_cparams(("arbitrary", "arbitrary")),
        name="attn_mixer",
    )(x_all, q, k_all, v_all, mod, w_o.astype(BF16), nf.reshape(1, d), _router_weights(rw), rb.reshape(n_exp, 1))


def _dispatch_kernel(off_ref, idx_ref, rank_ref, f_ref, xs_hbm, dest_ref, sem):
    tm = idx_ref.shape[1]

    def start(t, c):
        src = f_ref.at[pl.ds(pl.multiple_of(t * SUBLANES, SUBLANES), SUBLANES)]
        for k in range(TOP_K):
            dst = off_ref[idx_ref[k, t]] + rank_ref[k, t]
            dest_ref[k, t] = dst
            row = xs_hbm.at[pl.ds(pl.multiple_of(dst * SUBLANES, SUBLANES), SUBLANES)]
            pltpu.make_async_copy(src, row, sem).start(priority=k % 2)
        return c

    lax.fori_loop(0, tm, start, 0)
    for k in range(TOP_K):
        pltpu.make_async_copy(f_ref, xs_hbm.at[pl.ds(0, tm * SUBLANES)], sem).wait()


def _dispatch(f, idx, rank, off_start):
    t = idx.shape[1]
    smem_tile = pl.BlockSpec((TOP_K, TM), lambda i, off: (0, i), memory_space=pltpu.SMEM)
    grid_spec = pltpu.PrefetchScalarGridSpec(
        num_scalar_prefetch=1,
        grid=(t // TM,),
        in_specs=[smem_tile, smem_tile, pl.BlockSpec((TM * SUBLANES, LANES), lambda i, off: (i, 0))],
        out_specs=(pl.BlockSpec(memory_space=pl.ANY), smem_tile),
        scratch_shapes=[pltpu.SemaphoreType.DMA],
    )
    return pl.pallas_call(
        _dispatch_kernel,
        grid_spec=grid_spec,
        out_shape=(jax.ShapeDtypeStruct((t * TOP_K * SUBLANES, LANES), F32),
                   jax.ShapeDtypeStruct((TOP_K, t), jnp.int32)),
        compiler_params=_cparams(("arbitrary",)),
        name="moe_dispatch",
    )(off_start, idx, rank, f)


def _expert_kernel(blk_ref, exp_ref, lo_ref, hi_ref,
                   xs_ref, wgu_ref, bgu_ref, wd_ref, bd_ref, o_ref, wgu_b, wd_b):
    w = pl.program_id(0)
    tm = xs_ref.shape[0] // SUBLANES
    d = SUBLANES * LANES
    de = wd_ref.shape[2]
    prev = jnp.maximum(w - 1, 0)
    lo = lo_ref[w]
    hi = hi_ref[w]
    blk = blk_ref[w]
    new_expert = (w == 0) | (exp_ref[w] != exp_ref[prev])
    first_visit = (w == 0) | (blk != blk_ref[prev])

    @pl.when(new_expert)
    def _():
        wgu_b[...] = wgu_ref[0, 0].astype(BF16)
        wd_b[...] = wd_ref[0, 0].astype(BF16)

    @pl.when(first_visit & (hi <= lo))
    def _():
        o_ref[...] = jnp.zeros_like(o_ref)

    @pl.when(hi > lo)
    def _():
        x = _load_tile_rows(xs_ref, tm).astype(BF16)
        gu = jnp.dot(x, wgu_b[...], preferred_element_type=F32) + bgu_ref[0]
        g = jnp.minimum(gu[:, :de], SWIGLU_LIMIT)
        u = jnp.clip(gu[:, de:], -SWIGLU_LIMIT, SWIGLU_LIMIT)
        act = g * _sigmoid(SWIGLU_ALPHA * g) * (u + 1.0)
        y = jnp.dot(act.astype(BF16), wd_b[...], preferred_element_type=F32) + bd_ref[0]
        rows = blk * tm + lax.broadcasted_iota(jnp.int32, (tm, d), 0)
        mine = (rows >= lo) & (rows < hi)

        @pl.when(first_visit)
        def _():
            _store_tile_rows(o_ref, jnp.where(mine, y, 0.0))

        @pl.when(jnp.logical_not(first_visit))
        def _():
            _store_tile_rows(o_ref, jnp.where(mine, y, _load_tile_rows(o_ref, tm)))


def _experts(xs, work, layer, w_gu, b_gu, w_down, b_down):
    _, n_exp, d, de2 = w_gu.shape
    de = w_down.shape[2]
    n_work = work[0].shape[0]
    grid_spec = pltpu.PrefetchScalarGridSpec(
        num_scalar_prefetch=4,
        grid=(n_work,),
        in_specs=[
            pl.BlockSpec((TE * SUBLANES, LANES), lambda w, blk, ex, lo, hi: (blk[w], 0)),
            pl.BlockSpec((1, 1, d, de2), lambda w, blk, ex, lo, hi: (layer, ex[w], 0, 0)),
            pl.BlockSpec((1, 1, de2), lambda w, blk, ex, lo, hi: (ex[w], 0, 0)),
            pl.BlockSpec((1, 1, de, d), lambda w, blk, ex, lo, hi: (layer, ex[w], 0, 0)),
            pl.BlockSpec((1, 1, d), lambda w, blk, ex, lo, hi: (ex[w], 0, 0)),
        ],
        out_specs=pl.BlockSpec((TE * SUBLANES, LANES), lambda w, blk, ex, lo, hi: (blk[w], 0)),
        scratch_shapes=[pltpu.VMEM((d, de2), BF16), pltpu.VMEM((de, d), BF16)],
    )
    return pl.pallas_call(
        _expert_kernel,
        grid_spec=grid_spec,
        out_shape=jax.ShapeDtypeStruct(xs.shape, F32),
        compiler_params=_cparams(("arbitrary",)),
        name="moe_experts",
    )(*work, xs, w_gu, b_gu[layer].reshape(n_exp, 1, de2), w_down, b_down[layer].reshape(n_exp, 1, d))


def _combine_kernel(row_of_tile, dest_ref, next_ref, gate_ref, x_ref, mod_ref, yb_hbm, o_ref, buf, sem):
    i = pl.program_id(0)
    n_tiles = pl.num_programs(0)
    tm, d = x_ref.shape
    half = i % 2

    def fetch(idx_ref, h):
        def start(t, c):
            slot = pl.ds(pl.multiple_of(t * SUBLANES, SUBLANES), SUBLANES)
            for k in range(TOP_K):
                row = pl.ds(pl.multiple_of(idx_ref[k, t] * SUBLANES, SUBLANES), SUBLANES)
                pltpu.make_async_copy(yb_hbm.at[row], buf.at[h, k, slot], sem.at[h]).start(priority=k % 2)
            return c

        lax.fori_loop(0, tm, start, 0)

    @pl.when(i == 0)
    def _():
        fetch(dest_ref, 0)

    @pl.when(i + 1 < n_tiles)
    def _():
        fetch(next_ref, 1 - half)

    g2 = mod_ref[pl.ds(row_of_tile(i), 1), pl.ds(5 * d, d)]
    gates = jnp.concatenate([gate_ref[...], jnp.zeros((LANES - 8, tm), F32)], axis=0).T
    for k in range(TOP_K):
        pltpu.make_async_copy(yb_hbm.at[pl.ds(0, tm * SUBLANES)], buf.at[half, k], sem.at[half]).wait()
    y = _load_tile_rows(buf.at[half, 0], tm) * gates[:, 0:1]
    for k in range(1, TOP_K):
        y = y + _load_tile_rows(buf.at[half, k], tm) * gates[:, k:k + 1]
    o_ref[...] = x_ref[...] + g2 * y


def _combine(x_new, yb, dest, gates, mod, row_of_tile):
    t, d = x_new.shape
    last = t // TM - 1
    return pl.pallas_call(
        functools.partial(_combine_kernel, row_of_tile),
        grid=(t // TM,),
        in_specs=[
            pl.BlockSpec((TOP_K, TM), lambda i: (0, i), memory_space=pltpu.SMEM),
            pl.BlockSpec((TOP_K, TM), lambda i: (0, jnp.minimum(i + 1, last)), memory_space=pltpu.SMEM),
            pl.BlockSpec((8, TM), lambda i: (0, i)),
            pl.BlockSpec((TM, d), lambda i: (i, 0)),
            _const_spec((MOD_ROWS, N_MOD * d)),
            pl.BlockSpec(memory_space=pl.ANY),
        ],
        out_specs=pl.BlockSpec((TM, d), lambda i: (i, 0)),
        out_shape=jax.ShapeDtypeStruct((t, d), F32),
        scratch_shapes=[pltpu.VMEM((2, TOP_K, TM * SUBLANES, LANES), F32),
                        pltpu.SemaphoreType.DMA((2,))],
        compiler_params=_cparams(("arbitrary",)),
        name="moe_combine",
    )(dest, dest, gates, x_new, mod, yb)


def _work_items(counts, n_assign):
    n_exp = counts.shape[0]
    n_blocks = n_assign // TE
    n_work = n_blocks + n_exp - 1
    off_end = jnp.cumsum(counts)
    off_start = off_end - counts
    first_blk = off_start // TE
    n_items = jnp.where(counts > 0, (off_end - 1) // TE - first_blk + 1, 0)
    item_end = jnp.cumsum(n_items)
    item_start = item_end - n_items
    w = jnp.arange(n_work, dtype=jnp.int32)
    ex = jnp.minimum(jnp.sum(w[:, None] >= item_end[None, :], axis=1), n_exp - 1).astype(jnp.int32)
    used = w < item_end[-1]
    blk = jnp.where(used, first_blk[ex] + w - item_start[ex], n_blocks - 1).astype(jnp.int32)
    lo = jnp.where(used, jnp.maximum(off_start[ex], blk * TE), 0).astype(jnp.int32)
    hi = jnp.where(used, jnp.minimum(off_end[ex], (blk + 1) * TE), 0).astype(jnp.int32)
    return (blk, ex, lo, hi), off_start.astype(jnp.int32)


def _moe(x_new, f, idx, gates, rank, cnt, mod, row_of_tile, layer, w_gu, b_gu, w_down, b_down):
    t = idx.shape[1]
    work, off_start = _work_items(cnt[:, 0], t * TOP_K)
    xs, dest = _dispatch(f, idx, rank, off_start)
    yb = _experts(xs, work, layer, w_gu, b_gu, w_down, b_down)
    return _combine(x_new, yb, dest, gates, mod, row_of_tile)


def kernel(x, c, ctx, c_ctx, ada_w, ada_b, norm_mix, norm_ffn, gm_w_in, gm_b_in, gm_v_gain, gm_w_s,
           gm_b_s, gm_w_out, at_w_qkv, at_q_gain, at_k_gain, at_w_o, moe_router_w, moe_router_b,
           moe_w_gu, moe_b_gu, moe_w_down, moe_b_down):
    n_samples, n_lat, d = x.shape
    n_ctx = ctx.shape[1]
    assert d == SUBLANES * LANES, "MoE row movement assumes one f32 tile per token row"
    assert n_lat % TM == 0 and n_ctx % TM == 0 and n_samples < MOD_ROWS
    assert n_lat % TG == 0 and (n_samples * n_ctx) % TG == 0 and TG % CHUNK == 0
    assert n_lat % TA == 0 and (n_samples * n_ctx) % TA == 0
    assert (n_samples * n_lat * TOP_K) % TE == 0 and (n_samples * n_ctx * TOP_K) % TE == 0
    n_ctx_tiles = n_samples * n_ctx // TM
    tps = n_lat // TM

    mods = _ada_table(c, c_ctx, ada_w, ada_b)
    x_all = jnp.concatenate([ctx.reshape(-1, d), x.reshape(-1, d)], axis=0)

    def row_all(i):
        return jnp.where(i < n_ctx_tiles, n_samples, (i - n_ctx_tiles) // tps)

    x_new, f, idx, gates, rank, cnt = _gmlp_layer(
        x_all, mods[0], n_samples * n_ctx // TG, n_lat // TG, n_samples, norm_mix[0], gm_w_in[0], gm_b_in[0],
        gm_v_gain[0], gm_w_s[0], gm_b_s[0], gm_w_out[0], norm_ffn[0], moe_router_w[0],
        moe_router_b[0])
    x_all = _moe(x_new, f, idx, gates, rank, cnt, mods[0], row_all,
                 0, moe_w_gu, moe_b_gu, moe_w_down, moe_b_down)

    q, k_all, v_all = _qkv_layer(x_all, mods[1], n_ctx_tiles, tps, n_samples, n_ctx, n_lat,
                                 norm_mix[1], at_w_qkv[0], at_q_gain[0], at_k_gain[0])
    x_new, f, idx, gates, rank, cnt = _attn_layer(
        x_all, q, k_all, v_all, mods[1], n_samples * n_ctx // TA, n_lat // TA, n_samples, at_w_o[0], norm_ffn[1],
        moe_router_w[1], moe_router_b[1])
    out = _moe(x_new, f, idx, gates, rank, cnt, mods[1], lambda i: i // tps,
               1, moe_w_gu, moe_b_gu, moe_w_down, moe_b_down)
    return out.reshape(n_samples, n_lat, d)
```

```python
import functools

import jax
import jax.numpy as jnp
from jax import lax
from jax.experimental import pallas as pl
from jax.experimental.pallas import tpu as pltpu

F32 = jnp.float32
BF16 = jnp.bfloat16
HIGHEST = lax.Precision.HIGHEST

GRID_W = 64
N_MOD = 6
NORM_EPS = 1e-6
CHUNK = 128
GM_GROUPS = 8
HEAD_DIM = 128
N_KV_HEADS = 2
AXIS_DIM = HEAD_DIM // 2
ROPE_THETA = 10000.0
TOP_K = 4
SWIGLU_LIMIT = 7.0
SWIGLU_ALPHA = 1.702

TM = 256
TG = 512
TA = 512
TE = 512
ATTN_HEADS_PER_DOT = 2
MOD_ROWS = 16
V7X_VMEM_LIMIT = 56 * 1024 * 1024


def _cparams(sem, vmem=V7X_VMEM_LIMIT):
    return pltpu.CompilerParams(dimension_semantics=sem, vmem_limit_bytes=vmem)


def _const_spec(shape):
    nd = len(shape)
    return pl.BlockSpec(shape, lambda *_: (0,) * nd, pipeline_mode=pl.Buffered(1))


LANES = 128
SUBLANES = 8


def _load_tile_rows(ref, n):
    return jnp.concatenate([ref[pl.ds(s, n, stride=SUBLANES), :] for s in range(SUBLANES)], axis=1)


def _store_tile_rows(ref, val):
    n = val.shape[0]
    for s in range(SUBLANES):
        ref[pl.ds(s, n, stride=SUBLANES), :] = val[:, s * LANES:(s + 1) * LANES]


def _sigmoid(x):
    return 1.0 / (1.0 + jnp.exp(-x))


def _rms(x):
    return x * lax.rsqrt(jnp.mean(x * x, axis=-1, keepdims=True) + NORM_EPS)


def _ada_kernel(s_ref, w_ref, b_ref, o_ref):
    s = s_ref[...]
    s = s * _sigmoid(s)
    o_ref[0] = jnp.dot(s, w_ref[0], precision=HIGHEST, preferred_element_type=F32) + b_ref[0]


def _ada_table(c, c_ctx, ada_w, ada_b):
    depth, d, n = ada_w.shape
    b = c.shape[0]
    s = jnp.concatenate([c, c_ctx[None, :], jnp.zeros((MOD_ROWS - b - 1, d), F32)], axis=0)
    tn = 1536
    return pl.pallas_call(
        _ada_kernel,
        grid=(depth, n // tn),
        in_specs=[
            pl.BlockSpec((MOD_ROWS, d), lambda i, j: (0, 0)),
            pl.BlockSpec((1, d, tn), lambda i, j: (i, 0, j)),
            pl.BlockSpec((1, 1, tn), lambda i, j: (i, 0, j)),
        ],
        out_specs=pl.BlockSpec((1, MOD_ROWS, tn), lambda i, j: (i, 0, j)),
        out_shape=jax.ShapeDtypeStruct((depth, MOD_ROWS, n), F32),
        compiler_params=_cparams(("arbitrary", "arbitrary")),
        name="ada_table",
    )(s, ada_w, ada_b.reshape(depth, 1, n))


def _mod_slices(mod_ref, row, d, first):
    return [mod_ref[pl.ds(row, 1), pl.ds((first + k) * d, d)] for k in range(3)]


def _router_epilogue(step, x_new, sh2, sc2, nf_ref, rwt_ref, rb_ref,
                     f_ref, idx_ref, gate_ref, rank_ref, cnt_ref, base_ref):
    tm = x_new.shape[0]
    n_exp = rwt_ref.shape[0] // 2
    f = _rms(x_new) * nf_ref[...] * (1.0 + sc2) + sh2
    _store_tile_rows(f_ref, f)

    nt = (((1,), (1,)), ((), ()))
    f_hi = f.astype(BF16)
    f_lo = (f - f_hi.astype(F32)).astype(BF16)
    l_hi = lax.dot_general(rwt_ref[...], f_hi, nt, preferred_element_type=F32)
    l_lo = lax.dot_general(rwt_ref[:n_exp, :], f_lo, nt, preferred_element_type=F32)
    logits = l_hi[:n_exp, :] + l_hi[n_exp:, :] + l_lo + rb_ref[...]
    eid = lax.broadcasted_iota(jnp.int32, (n_exp, tm), 0).astype(F32)

    @pl.when(step == 0)
    def _():
        base_ref[...] = jnp.zeros_like(base_ref)

    r_io = lax.broadcasted_iota(jnp.int32, (tm, tm), 0)
    c_io = lax.broadcasted_iota(jnp.int32, (tm, tm), 1)
    before = jnp.where(r_io < c_io, 1.0, 0.0).astype(BF16)
    ones = jnp.ones((tm, tm), BF16)

    vals, idxs, hits = [], [], []
    l = logits
    for _ in range(TOP_K):
        m = jnp.max(l, axis=0, keepdims=True)
        sel = jnp.min(jnp.where(l == m, eid, float(n_exp)), axis=0, keepdims=True)
        hit = eid == sel
        l = jnp.where(hit, -jnp.inf, l)
        vals.append(m)
        idxs.append(sel)
        hits.append(hit)
    onehot = jnp.concatenate([jnp.where(h, 1.0, 0.0) for h in hits], axis=0).astype(BF16)
    prefix = jnp.dot(onehot, before, preferred_element_type=F32)
    count = jnp.dot(onehot, ones, preferred_element_type=F32)
    base = base_ref[...]
    ranks = []
    for k, hit in enumerate(hits):
        pk = prefix[k * n_exp:(k + 1) * n_exp, :]
        ranks.append(jnp.sum(jnp.where(hit, base + pk, 0.0), axis=0, keepdims=True))
        base = base + count[k * n_exp:(k + 1) * n_exp, :]
    base_ref[...] = base
    es = [jnp.exp(v - vals[0]) for v in vals]
    tot = es[0] + es[1] + es[2] + es[3]
    zero = jnp.zeros_like(tot)
    gate_ref[...] = jnp.concatenate([e / tot for e in es] + [zero] * (8 - TOP_K), axis=0)
    idx_ref[...] = jnp.concatenate(idxs, axis=0).astype(jnp.int32)
    rank_ref[...] = jnp.concatenate(ranks, axis=0).astype(jnp.int32)
    cnt_ref[...] = base_ref[:, :128].astype(jnp.int32)


def _router_weights(rw):
    hi = rw.T.astype(BF16)
    lo = (rw.T - hi.astype(F32)).astype(BF16)
    return jnp.concatenate([hi, lo], axis=0)


def _router_out(t, d, n_exp):
    shapes = (
        jax.ShapeDtypeStruct((t * SUBLANES, LANES), F32),
        jax.ShapeDtypeStruct((TOP_K, t), jnp.int32),
        jax.ShapeDtypeStruct((8, t), F32),
        jax.ShapeDtypeStruct((TOP_K, t), jnp.int32),
        jax.ShapeDtypeStruct((n_exp, 128), jnp.int32),
    )
    return shapes


def _router_out_specs(tile_of, tm, d, n_exp):
    return (
        pl.BlockSpec((tm * SUBLANES, LANES), lambda *g: (tile_of(*g), 0)),
        pl.BlockSpec((TOP_K, tm), lambda *g: (0, tile_of(*g))),
        pl.BlockSpec((8, tm), lambda *g: (0, tile_of(*g))),
        pl.BlockSpec((TOP_K, tm), lambda *g: (0, tile_of(*g))),
        pl.BlockSpec((n_exp, 128), lambda *g: (0, 0)),
    )


def _gmlp_kernel(n_ctx_tiles, tiles_per_sample, n_samples,
                 x_ref, mod_ref, nm_ref, win_ref, bin_ref, vg_ref, ws_ref, bs_ref, wout_ref,
                 nf_ref, rwt_ref, rb_ref,
                 xo_ref, f_ref, idx_ref, gate_ref, rank_ref, cnt_ref, base_ref):
    i = pl.program_id(0)
    d = x_ref.shape[1]
    gw = wout_ref.shape[0]
    gc = gw // GM_GROUPS
    row = jnp.where(i < n_ctx_tiles, n_samples, (i - n_ctx_tiles) // tiles_per_sample)
    sh1, sc1, g1 = _mod_slices(mod_ref, row, d, 0)
    sh2, sc2, _ = _mod_slices(mod_ref, row, d, 3)

    x = x_ref[...]
    h = _rms(x) * nm_ref[...] * (1.0 + sc1) + sh1
    hb = h.astype(BF16)

    def proj(col):
        a = jnp.dot(hb, win_ref[:, col:col + gc], preferred_element_type=F32) + bin_ref[:, col:col + gc]
        return 0.5 * a * (1.0 + lax.erf(a * (2.0 ** -0.5)))

    u = [proj(g * gc) for g in range(GM_GROUPS)]
    v = [proj(gw + g * gc) for g in range(GM_GROUPS)]
    ssq = v[0] * v[0]
    for g in range(1, GM_GROUPS):
        ssq = ssq + v[g] * v[g]
    inv = lax.rsqrt(jnp.sum(ssq, axis=-1, keepdims=True) * (1.0 / gw) + NORM_EPS)
    cols = []
    for g in range(GM_GROUPS):
        vn = (v[g] * inv * vg_ref[:, g * gc:(g + 1) * gc]).astype(BF16)
        s = jnp.concatenate(
            [jnp.dot(ws_ref[g], vn[c * CHUNK:(c + 1) * CHUNK, :], preferred_element_type=F32)
             + bs_ref[:, g * gc:(g + 1) * gc] for c in range(x.shape[0] // CHUNK)], axis=0)
        cols.append((u[g] * s).astype(BF16))
    z = jnp.concatenate(cols, axis=1)
    y = jnp.dot(z, wout_ref[...], preferred_element_type=F32)
    x_new = x + g1 * y
    xo_ref[...] = x_new
    _router_epilogue(i, x_new, sh2, sc2, nf_ref, rwt_ref, rb_ref,
                     f_ref, idx_ref, gate_ref, rank_ref, cnt_ref, base_ref)


def _gmlp_layer(x_all, mod, n_ctx_tiles, tiles_per_sample, n_samples,
                nm, w_in, b_in, v_gain, w_s, b_s, w_out, nf, rw, rb):
    t, d = x_all.shape
    gw = w_out.shape[0]
    n_exp = rw.shape[1]
    gc = gw // GM_GROUPS
    bs_full = jnp.repeat(b_s.T, gc, axis=1)
    tile = lambda i: i
    outs = pl.pallas_call(
        functools.partial(_gmlp_kernel, n_ctx_tiles, tiles_per_sample, n_samples),
        grid=(t // TG,),
        in_specs=[
            pl.BlockSpec((TG, d), lambda i: (i, 0)),
            _const_spec((MOD_ROWS, N_MOD * d)),
            _const_spec((1, d)),
            _const_spec((d, 2 * gw)),
            _const_spec((1, 2 * gw)),
            _const_spec((1, gw)),
            _const_spec((GM_GROUPS, CHUNK, CHUNK)),
            _const_spec((CHUNK, gw)),
            _const_spec((gw, d)),
            _const_spec((1, d)),
            _const_spec((2 * n_exp, d)),
            _const_spec((n_exp, 1)),
        ],
        out_specs=(pl.BlockSpec((TG, d), lambda i: (i, 0)),) + _router_out_specs(tile, TG, d, n_exp),
        out_shape=(jax.ShapeDtypeStruct((t, d), F32),) + _router_out(t, d, n_exp),
        scratch_shapes=[pltpu.VMEM((n_exp, TG), F32)],
        compiler_params=_cparams(("arbitrary",)),
        name="gmlp_mixer",
    )(x_all, mod, nm.reshape(1, d), w_in.astype(BF16), b_in.reshape(1, -1), v_gain.reshape(1, gw),
      w_s.astype(BF16), bs_full, w_out.astype(BF16), nf.reshape(1, d), _router_weights(rw), rb.reshape(n_exp, 1))
    return outs


def _qkv_kernel(n_ctx_tiles, tiles_per_sample, n_samples,
                x_ref, mod_ref, nm_ref, w_ref, qg_ref, kg_ref, cos_ref, sin_ref,
                q_ref, k_ref, v_ref):
    i = pl.program_id(0)
    d = x_ref.shape[1]
    nq = q_ref.shape[1]
    nkv = k_ref.shape[2]
    row = jnp.where(i < n_ctx_tiles, n_samples, (i - n_ctx_tiles) // tiles_per_sample)
    sh1, sc1, _ = _mod_slices(mod_ref, row, d, 0)
    h = _rms(x_ref[...]) * nm_ref[...] * (1.0 + sc1) + sh1
    qkv = jnp.dot(h.astype(BF16), w_ref[...], preferred_element_type=F32)
    cos = cos_ref[...]
    sin = sin_ref[...]
    half = AXIS_DIM // 2
    lane = lax.broadcasted_iota(jnp.int32, (x_ref.shape[0], HEAD_DIM), 1)
    first_half = (lane % AXIS_DIM) < half

    def head(xh, gain):
        xh = _rms(xh) * gain
        partner = jnp.where(first_half, pltpu.roll(xh, HEAD_DIM - half, 1), pltpu.roll(xh, half, 1))
        return xh * cos + partner * sin

    q = [head(qkv[:, j * HEAD_DIM:(j + 1) * HEAD_DIM], qg_ref[...]) for j in range(nq // HEAD_DIM)]
    k = [head(qkv[:, nq + j * HEAD_DIM:nq + (j + 1) * HEAD_DIM], kg_ref[...])
         for j in range(nkv // HEAD_DIM)]
    q_ref[...] = jnp.concatenate(q, axis=1).astype(BF16)
    k_ref[0] = jnp.concatenate(k, axis=1).astype(BF16)
    v_ref[0] = qkv[:, nq + nkv:].astype(BF16)


def _rope_tables(n_lat):
    rows = n_lat // GRID_W
    row = jnp.repeat(jnp.arange(rows, dtype=jnp.int32), GRID_W).astype(F32)
    col = jnp.tile(jnp.arange(GRID_W, dtype=jnp.int32), rows).astype(F32)
    inv_freq = 1.0 / (ROPE_THETA ** (jnp.arange(0, AXIS_DIM, 2, dtype=F32) / AXIS_DIM))
    ang_r = row[:, None] * inv_freq
    ang_c = col[:, None] * inv_freq
    cos = jnp.concatenate([jnp.cos(ang_r)] * 2 + [jnp.cos(ang_c)] * 2, axis=1)
    sin = jnp.concatenate([-jnp.sin(ang_r), jnp.sin(ang_r), -jnp.sin(ang_c), jnp.sin(ang_c)], axis=1)
    cos = jnp.concatenate([jnp.ones((TM, HEAD_DIM), F32), cos], axis=0)
    sin = jnp.concatenate([jnp.zeros((TM, HEAD_DIM), F32), sin], axis=0)
    return cos, sin


def _qkv_layer(x_all, mod, n_ctx_tiles, tiles_per_sample, n_samples, n_ctx, n_lat,
               nm, w_qkv, q_gain, k_gain):
    t, d = x_all.shape
    nqkv = w_qkv.shape[1]
    nkv = N_KV_HEADS * HEAD_DIM
    nq = nqkv - 2 * nkv
    cos, sin = _rope_tables(n_lat)
    ctx_blocks = n_ctx // TM
    tps = tiles_per_sample

    def pos_block(i):
        return jnp.where(i < n_ctx_tiles, 0, 1 + (i - n_ctx_tiles) % tps)

    def kv_map(i):
        lat = i >= n_ctx_tiles
        b = jnp.where(lat, (i - n_ctx_tiles) // tps, i // ctx_blocks)
        j = jnp.where(lat, ctx_blocks + (i - n_ctx_tiles) % tps, i % ctx_blocks)
        return (b, j, 0)

    return pl.pallas_call(
        functools.partial(_qkv_kernel, n_ctx_tiles, tiles_per_sample, n_samples),
        grid=(t // TM,),
        in_specs=[
            pl.BlockSpec((TM, d), lambda i: (i, 0)),
            _const_spec((MOD_ROWS, N_MOD * d)),
            _const_spec((1, d)),
            _const_spec((d, nqkv)),
            _const_spec((1, HEAD_DIM)),
            _const_spec((1, HEAD_DIM)),
            pl.BlockSpec((TM, HEAD_DIM), lambda i: (pos_block(i), 0)),
            pl.BlockSpec((TM, HEAD_DIM), lambda i: (pos_block(i), 0)),
        ],
        out_specs=(
            pl.BlockSpec((TM, nq), lambda i: (i, 0)),
            pl.BlockSpec((1, TM, nkv), kv_map),
            pl.BlockSpec((1, TM, nkv), kv_map),
        ),
        out_shape=(
            jax.ShapeDtypeStruct((t, nq), BF16),
            jax.ShapeDtypeStruct((n_samples, n_ctx + n_lat, nkv), BF16),
            jax.ShapeDtypeStruct((n_samples, n_ctx + n_lat, nkv), BF16),
        ),
        compiler_params=_cparams(("arbitrary",)),
        name="qkv_rope",
    )(x_all, mod, nm.reshape(1, d), w_qkv.astype(BF16), q_gain.reshape(1, HEAD_DIM),
      k_gain.reshape(1, HEAD_DIM), cos, sin)


def _attn_kernel(tiles_per_sample,
                 x_ref, q_ref, k_ref, v_ref, mod_ref, wo_ref, nf_ref, rwt_ref, rb_ref,
                 xo_ref, f_ref, idx_ref, gate_ref, rank_ref, cnt_ref, base_ref):
    b = pl.program_id(0)
    j = pl.program_id(1)
    d = x_ref.shape[1]
    n_heads = q_ref.shape[1] // HEAD_DIM
    group = n_heads // N_KV_HEADS
    _, _, g1 = _mod_slices(mod_ref, b, d, 0)
    sh2, sc2, _ = _mod_slices(mod_ref, b, d, 3)
    tq = q_ref.shape[0]
    exp2_scale = (HEAD_DIM ** -0.5) * 1.4426950408889634
    outs = []
    hpd = ATTN_HEADS_PER_DOT
    for h0 in range(0, n_heads, hpd):
        g = h0 // group
        qg = jnp.concatenate([q_ref[:, h * HEAD_DIM:(h + 1) * HEAD_DIM]
                              for h in range(h0, h0 + hpd)], axis=0)
        kg = k_ref[0, :, g * HEAD_DIM:(g + 1) * HEAD_DIM]
        vg = v_ref[0, :, g * HEAD_DIM:(g + 1) * HEAD_DIM]
        s = lax.dot_general(qg, kg, (((1,), (1,)), ((), ())), preferred_element_type=F32)
        p = jnp.exp2((s - jnp.max(s, axis=-1, keepdims=True)) * exp2_scale)
        den = jnp.sum(p, axis=-1, keepdims=True)
        pb = p.astype(BF16)
        for h in range(hpd):
            rows = slice(h * tq, (h + 1) * tq)
            o = jnp.dot(pb[rows, :], vg, preferred_element_type=F32) / den[rows, :]
            outs.append(o.astype(BF16))
    o_all = jnp.concatenate(outs, axis=1)
    y = jnp.dot(o_all, wo_ref[...], preferred_element_type=F32)
    x_new = x_ref[...] + g1 * y
    xo_ref[...] = x_new
    _router_epilogue(b * tiles_per_sample + j, x_new, sh2, sc2, nf_ref, rwt_ref, rb_ref,
                     f_ref, idx_ref, gate_ref, rank_ref, cnt_ref, base_ref)


def _attn_layer(x_all, q, k_all, v_all, mod, n_ctx_tiles, tiles_per_sample, n_samples,
                w_o, nf, rw, rb):
    d = x_all.shape[1]
    t_lat = n_samples * tiles_per_sample * TA
    n_exp = rw.shape[1]
    nq = q.shape[1]
    lk, nkv = k_all.shape[1], k_all.shape[2]
    tps = tiles_per_sample
    lat_tile = lambda b, j: n_ctx_tiles + b * tps + j
    tile = lambda b, j: b * tps + j
    return pl.pallas_call(
        functools.partial(_attn_kernel, tiles_per_sample),
        grid=(n_samples, tps),
        in_specs=[
            pl.BlockSpec((TA, d), lambda b, j: (lat_tile(b, j), 0)),
            pl.BlockSpec((TA, nq), lambda b, j: (lat_tile(b, j), 0)),
            pl.BlockSpec((1, lk, nkv), lambda b, j: (b, 0, 0)),
            pl.BlockSpec((1, lk, nkv), lambda b, j: (b, 0, 0)),
            _const_spec((MOD_ROWS, N_MOD * d)),
            _const_spec((nq, d)),
            _const_spec((1, d)),
            _const_spec((2 * n_exp, d)),
            _const_spec((n_exp, 1)),
        ],
        out_specs=(pl.BlockSpec((TA, d), lambda b, j: (tile(b, j), 0)),)
        + _router_out_specs(tile, TA, d, n_exp),
        out_shape=(jax.ShapeDtypeStruct((t_lat, d), F32),) + _router_out(t_lat, d, n_exp),
        scratch_shapes=[pltpu.VMEM((n_exp, TA), F32)],
        compiler_params=_cparams(("arbitrary", "arbitrary")),
        name="attn_mixer",
    )(x_all, q, k_all, v_all, mod, w_o.astype(BF16), nf.reshape(1, d), _router_weights(rw), rb.reshape(n_exp, 1))


def _dispatch_kernel(off_ref, idx_ref, rank_ref, f_ref, xs_hbm, dest_ref, sem):
    tm = idx_ref.shape[1]

    def start(t, c):
        src = f_ref.at[pl.ds(pl.multiple_of(t * SUBLANES, SUBLANES), SUBLANES)]
        for k in range(TOP_K):
            dst = off_ref[idx_ref[k, t]] + rank_ref[k, t]
            dest_ref[k, t] = dst
            row = xs_hbm.at[pl.ds(pl.multiple_of(dst * SUBLANES, SUBLANES), SUBLANES)]
            pltpu.make_async_copy(src, row, sem).start(priority=k % 2)
        return c

    lax.fori_loop(0, tm, start, 0)
    for k in range(TOP_K):
        pltpu.make_async_copy(f_ref, xs_hbm.at[pl.ds(0, tm * SUBLANES)], sem).wait()


def _dispatch(f, idx, rank, off_start):
    t = idx.shape[1]
    smem_tile = pl.BlockSpec((TOP_K, TM), lambda i, off: (0, i), memory_space=pltpu.SMEM)
    grid_spec = pltpu.PrefetchScalarGridSpec(
        num_scalar_prefetch=1,
        grid=(t // TM,),
        in_specs=[smem_tile, smem_tile, pl.BlockSpec((TM * SUBLANES, LANES), lambda i, off: (i, 0))],
        out_specs=(pl.BlockSpec(memory_space=pl.ANY), smem_tile),
        scratch_shapes=[pltpu.SemaphoreType.DMA],
    )
    return pl.pallas_call(
        _dispatch_kernel,
        grid_spec=grid_spec,
        out_shape=(jax.ShapeDtypeStruct((t * TOP_K * SUBLANES, LANES), F32),
                   jax.ShapeDtypeStruct((TOP_K, t), jnp.int32)),
        compiler_params=_cparams(("arbitrary",)),
        name="moe_dispatch",
    )(off_start, idx, rank, f)


def _expert_kernel(layer, blk_ref, exp_ref, lo_ref, hi_ref, nxt_ref,
                   xs_ref, wgu_hbm, bgu_ref, wd_hbm, bd_ref, o_ref,
                   gu_stage, d_stage, wgu_b, wd_b, sem):
    w = pl.program_id(0)
    tm = xs_ref.shape[0] // SUBLANES
    d = SUBLANES * LANES
    de = wd_b.shape[0]
    prev = jnp.maximum(w - 1, 0)
    lo = lo_ref[w]
    hi = hi_ref[w]
    blk = blk_ref[w]
    new_expert = (w == 0) | (exp_ref[w] != exp_ref[prev])
    first_visit = (w == 0) | (blk != blk_ref[prev])

    def weight_copies(e):
        return (pltpu.make_async_copy(wgu_hbm.at[layer, e], gu_stage, sem.at[0]),
                pltpu.make_async_copy(wd_hbm.at[layer, e], d_stage, sem.at[1]))

    @pl.when(w == 0)
    def _():
        for c in weight_copies(exp_ref[0]):
            c.start()

    @pl.when(new_expert)
    def _():
        for c in weight_copies(exp_ref[w]):
            c.wait()
        wgu_b[...] = gu_stage[...].astype(BF16)
        wd_b[...] = d_stage[...].astype(BF16)

        @pl.when(nxt_ref[w] >= 0)
        def _():
            for c in weight_copies(nxt_ref[w]):
                c.start()

    @pl.when(first_visit & (hi <= lo))
    def _():
        o_ref[...] = jnp.zeros_like(o_ref)

    @pl.when(hi > lo)
    def _():
        x = _load_tile_rows(xs_ref, tm).astype(BF16)
        gu = jnp.dot(x, wgu_b[...], preferred_element_type=F32) + bgu_ref[0]
        g = jnp.minimum(gu[:, :de], SWIGLU_LIMIT)
        u = jnp.clip(gu[:, de:], -SWIGLU_LIMIT, SWIGLU_LIMIT)
        act = g * _sigmoid(SWIGLU_ALPHA * g) * (u + 1.0)
        y = jnp.dot(act.astype(BF16), wd_b[...], preferred_element_type=F32) + bd_ref[0]
        rows = blk * tm + lax.broadcasted_iota(jnp.int32, (tm, d), 0)
        mine = (rows >= lo) & (rows < hi)

        @pl.when(first_visit)
        def _():
            _store_tile_rows(o_ref, jnp.where(mine, y, 0.0))

        @pl.when(jnp.logical_not(first_visit))
        def _():
            _store_tile_rows(o_ref, jnp.where(mine, y, _load_tile_rows(o_ref, tm)))


def _experts(xs, work, layer, w_gu, b_gu, w_down, b_down):
    _, n_exp, d, de2 = w_gu.shape
    de = w_down.shape[2]
    n_work = work[0].shape[0]
    item = lambda w, blk, ex, lo, hi, nxt: (blk[w], 0)
    bias = lambda w, blk, ex, lo, hi, nxt: (ex[w], 0, 0)
    grid_spec = pltpu.PrefetchScalarGridSpec(
        num_scalar_prefetch=5,
        grid=(n_work,),
        in_specs=[
            pl.BlockSpec((TE * SUBLANES, LANES), item),
            pl.BlockSpec(memory_space=pl.ANY),
            pl.BlockSpec((1, 1, de2), bias),
            pl.BlockSpec(memory_space=pl.ANY),
            pl.BlockSpec((1, 1, d), bias),
        ],
        out_specs=pl.BlockSpec((TE * SUBLANES, LANES), item),
        scratch_shapes=[pltpu.VMEM((d, de2), F32), pltpu.VMEM((de, d), F32),
                        pltpu.VMEM((d, de2), BF16), pltpu.VMEM((de, d), BF16),
                        pltpu.SemaphoreType.DMA((2,))],
    )
    return pl.pallas_call(
        functools.partial(_expert_kernel, layer),
        grid_spec=grid_spec,
        out_shape=jax.ShapeDtypeStruct(xs.shape, F32),
        compiler_params=_cparams(("arbitrary",)),
        name="moe_experts",
    )(*work, xs, w_gu, b_gu[layer].reshape(n_exp, 1, de2), w_down, b_down[layer].reshape(n_exp, 1, d))


def _combine_kernel(row_of_tile, dest_ref, next_ref, gate_ref, x_ref, mod_ref, yb_hbm, o_ref, buf, sem):
    i = pl.program_id(0)
    n_tiles = pl.num_programs(0)
    tm, d = x_ref.shape
    half = i % 2

    def fetch(idx_ref, h):
        def start(t, c):
            slot = pl.ds(pl.multiple_of(t * SUBLANES, SUBLANES), SUBLANES)
            for k in range(TOP_K):
                row = pl.ds(pl.multiple_of(idx_ref[k, t] * SUBLANES, SUBLANES), SUBLANES)
                pltpu.make_async_copy(yb_hbm.at[row], buf.at[h, k, slot], sem.at[h]).start(priority=k % 2)
            return c

        lax.fori_loop(0, tm, start, 0)

    @pl.when(i == 0)
    def _():
        fetch(dest_ref, 0)

    @pl.when(i + 1 < n_tiles)
    def _():
        fetch(next_ref, 1 - half)

    g2 = mod_ref[pl.ds(row_of_tile(i), 1), pl.ds(5 * d, d)]
    gates = jnp.concatenate([gate_ref[...], jnp.zeros((LANES - 8, tm), F32)], axis=0).T
    for k in range(TOP_K):
        pltpu.make_async_copy(yb_hbm.at[pl.ds(0, tm * SUBLANES)], buf.at[half, k], sem.at[half]).wait()
    y = _load_tile_rows(buf.at[half, 0], tm) * gates[:, 0:1]
    for k in range(1, TOP_K):
        y = y + _load_tile_rows(buf.at[half, k], tm) * gates[:, k:k + 1]
    o_ref[...] = x_ref[...] + g2 * y


def _combine(x_new, yb, dest, gates, mod, row_of_tile):
    t, d = x_new.shape
    last = t // TM - 1
    return pl.pallas_call(
        functools.partial(_combine_kernel, row_of_tile),
        grid=(t // TM,),
        in_specs=[
            pl.BlockSpec((TOP_K, TM), lambda i: (0, i), memory_space=pltpu.SMEM),
            pl.BlockSpec((TOP_K, TM), lambda i: (0, jnp.minimum(i + 1, last)), memory_space=pltpu.SMEM),
            pl.BlockSpec((8, TM), lambda i: (0, i)),
            pl.BlockSpec((TM, d), lambda i: (i, 0)),
            _const_spec((MOD_ROWS, N_MOD * d)),
            pl.BlockSpec(memory_space=pl.ANY),
        ],
        out_specs=pl.BlockSpec((TM, d), lambda i: (i, 0)),
        out_shape=jax.ShapeDtypeStruct((t, d), F32),
        scratch_shapes=[pltpu.VMEM((2, TOP_K, TM * SUBLANES, LANES), F32),
                        pltpu.SemaphoreType.DMA((2,))],
        compiler_params=_cparams(("arbitrary",)),
        name="moe_combine",
    )(dest, dest, gates, x_new, mod, yb)


def _work_items(counts, n_assign):
    n_exp = counts.shape[0]
    n_blocks = n_assign // TE
    n_work = n_blocks + n_exp - 1
    off_end = jnp.cumsum(counts)
    off_start = off_end - counts
    first_blk = off_start // TE
    n_items = jnp.where(counts > 0, (off_end - 1) // TE - first_blk + 1, 0)
    item_end = jnp.cumsum(n_items)
    item_start = item_end - n_items
    w = jnp.arange(n_work, dtype=jnp.int32)
    used = w < item_end[-1]
    ex = jnp.minimum(jnp.sum(w[:, None] >= item_end[None, :], axis=1), n_exp - 1).astype(jnp.int32)
    ex = jnp.where(used, ex, jnp.max(jnp.where(used, ex, 0)))
    blk = jnp.where(used, first_blk[ex] + w - item_start[ex], n_blocks - 1).astype(jnp.int32)
    lo = jnp.where(used, jnp.maximum(off_start[ex], blk * TE), 0).astype(jnp.int32)
    hi = jnp.where(used, jnp.minimum(off_end[ex], (blk + 1) * TE), 0).astype(jnp.int32)
    ids = jnp.arange(n_exp, dtype=jnp.int32)
    later = jnp.where((ids[None, :] > ids[:, None]) & (counts[None, :] > 0), ids[None, :], n_exp)
    nxt_of = jnp.min(later, axis=1)
    nxt = jnp.where(nxt_of[ex] < n_exp, nxt_of[ex], -1).astype(jnp.int32)
    return (blk, ex, lo, hi, nxt), off_start.astype(jnp.int32)


def _moe(x_new, f, idx, gates, rank, cnt, mod, row_of_tile, layer, w_gu, b_gu, w_down, b_down):
    t = idx.shape[1]
    work, off_start = _work_items(cnt[:, 0], t * TOP_K)
    xs, dest = _dispatch(f, idx, rank, off_start)
    yb = _experts(xs, work, layer, w_gu, b_gu, w_down, b_down)
    return _combine(x_new, yb, dest, gates, mod, row_of_tile)


def kernel(x, c, ctx, c_ctx, ada_w, ada_b, norm_mix, norm_ffn, gm_w_in, gm_b_in, gm_v_gain, gm_w_s,
           gm_b_s, gm_w_out, at_w_qkv, at_q_gain, at_k_gain, at_w_o, moe_router_w, moe_router_b,
           moe_w_gu, moe_b_gu, moe_w_down, moe_b_down):
    n_samples, n_lat, d = x.shape
    n_ctx = ctx.shape[1]
    assert d == SUBLANES * LANES, "MoE row movement assumes one f32 tile per token row"
    assert n_lat % TM == 0 and n_ctx % TM == 0 and n_samples < MOD_ROWS
    assert n_lat % TG == 0 and (n_samples * n_ctx) % TG == 0 and TG % CHUNK == 0
    assert n_lat % TA == 0 and (n_samples * n_ctx) % TA == 0
    assert (n_samples * n_lat * TOP_K) % TE == 0 and (n_samples * n_ctx * TOP_K) % TE == 0
    n_ctx_tiles = n_samples * n_ctx // TM
    tps = n_lat // TM

    mods = _ada_table(c, c_ctx, ada_w, ada_b)
    x_all = jnp.concatenate([ctx.reshape(-1, d), x.reshape(-1, d)], axis=0)

    def row_all(i):
        return jnp.where(i < n_ctx_tiles, n_samples, (i - n_ctx_tiles) // tps)

    x_new, f, idx, gates, rank, cnt = _gmlp_layer(
        x_all, mods[0], n_samples * n_ctx // TG, n_lat // TG, n_samples, norm_mix[0], gm_w_in[0], gm_b_in[0],
        gm_v_gain[0], gm_w_s[0], gm_b_s[0], gm_w_out[0], norm_ffn[0], moe_router_w[0],
        moe_router_b[0])
    x_all = _moe(x_new, f, idx, gates, rank, cnt, mods[0], row_all,
                 0, moe_w_gu, moe_b_gu, moe_w_down, moe_b_down)

    q, k_all, v_all = _qkv_layer(x_all, mods[1], n_ctx_tiles, tps, n_samples, n_ctx, n_lat,
                                 norm_mix[1], at_w_qkv[0], at_q_gain[0], at_k_gain[0])
    x_new, f, idx, gates, rank, cnt = _attn_layer(
        x_all, q, k_all, v_all, mods[1], n_samples * n_ctx // TA, n_lat // TA, n_samples, at_w_o[0], norm_ffn[1],
        moe_router_w[1], moe_router_b[1])
    out = _moe(x_new, f, idx, gates, rank, cnt, mods[1], lambda i: i // tps,
               1, moe_w_gu, moe_b_gu, moe_w_down, moe_b_down)
    return out.reshape(n_samples, n_lat, d)
```

```python
import functools

import jax
import jax.numpy as jnp
from jax import lax
from jax.experimental import pallas as pl
from jax.experimental.pallas import tpu as pltpu

F32 = jnp.float32
BF16 = jnp.bfloat16
HIGHEST = lax.Precision.HIGHEST

GRID_W = 64
N_MOD = 6
NORM_EPS = 1e-6
CHUNK = 128
GM_GROUPS = 8
HEAD_DIM = 128
N_KV_HEADS = 2
AXIS_DIM = HEAD_DIM // 2
ROPE_THETA = 10000.0
TOP_K = 4
SWIGLU_LIMIT = 7.0
SWIGLU_ALPHA = 1.702

TM = 256
TG = 512
TA = 512
TE = 512
ATTN_HEADS_PER_DOT = 2
MOD_ROWS = 16
V7X_VMEM_LIMIT = 56 * 1024 * 1024


def _cparams(sem, vmem=V7X_VMEM_LIMIT):
    return pltpu.CompilerParams(dimension_semantics=sem, vmem_limit_bytes=vmem)


def _const_spec(shape):
    nd = len(shape)
    return pl.BlockSpec(shape, lambda *_: (0,) * nd, pipeline_mode=pl.Buffered(1))


LANES = 128
SUBLANES = 8


def _load_tile_rows(ref, n):
    return jnp.concatenate([ref[pl.ds(s, n, stride=SUBLANES), :] for s in range(SUBLANES)], axis=1)


def _store_tile_rows(ref, val):
    n = val.shape[0]
    for s in range(SUBLANES):
        ref[pl.ds(s, n, stride=SUBLANES), :] = val[:, s * LANES:(s + 1) * LANES]


def _sigmoid(x):
    return 1.0 / (1.0 + jnp.exp(-x))


def _rms(x):
    return x * lax.rsqrt(jnp.mean(x * x, axis=-1, keepdims=True) + NORM_EPS)


def _ada_kernel(s_ref, w_ref, b_ref, o_ref):
    s = s_ref[...]
    s = s * _sigmoid(s)
    o_ref[0] = jnp.dot(s, w_ref[0], precision=HIGHEST, preferred_element_type=F32) + b_ref[0]


def _ada_table(c, c_ctx, ada_w, ada_b):
    depth, d, n = ada_w.shape
    b = c.shape[0]
    s = jnp.concatenate([c, c_ctx[None, :], jnp.zeros((MOD_ROWS - b - 1, d), F32)], axis=0)
    tn = 1536
    return pl.pallas_call(
        _ada_kernel,
        grid=(depth, n // tn),
        in_specs=[
            pl.BlockSpec((MOD_ROWS, d), lambda i, j: (0, 0)),
            pl.BlockSpec((1, d, tn), lambda i, j: (i, 0, j)),
            pl.BlockSpec((1, 1, tn), lambda i, j: (i, 0, j)),
        ],
        out_specs=pl.BlockSpec((1, MOD_ROWS, tn), lambda i, j: (i, 0, j)),
        out_shape=jax.ShapeDtypeStruct((depth, MOD_ROWS, n), F32),
        compiler_params=_cparams(("arbitrary", "arbitrary")),
        name="ada_table",
    )(s, ada_w, ada_b.reshape(depth, 1, n))


def _mod_slices(mod_ref, row, d, first):
    return [mod_ref[pl.ds(row, 1), pl.ds((first + k) * d, d)] for k in range(3)]


def _router_epilogue(step, x_new, sh2, sc2, nf_ref, rwt_ref, rb_ref,
                     f_ref, idx_ref, gate_ref, rank_ref, cnt_ref, base_ref):
    tm = x_new.shape[0]
    n_exp = rwt_ref.shape[0] // 2
    f = _rms(x_new) * nf_ref[...] * (1.0 + sc2) + sh2
    _store_tile_rows(f_ref, f)

    nt = (((1,), (1,)), ((), ()))
    f_hi = f.astype(BF16)
    f_lo = (f - f_hi.astype(F32)).astype(BF16)
    l_hi = lax.dot_general(rwt_ref[...], f_hi, nt, preferred_element_type=F32)
    l_lo = lax.dot_general(rwt_ref[:n_exp, :], f_lo, nt, preferred_element_type=F32)
    logits = l_hi[:n_exp, :] + l_hi[n_exp:, :] + l_lo + rb_ref[...]
    eid = lax.broadcasted_iota(jnp.int32, (n_exp, tm), 0).astype(F32)

    @pl.when(step == 0)
    def _():
        base_ref[...] = jnp.zeros_like(base_ref)

    r_io = lax.broadcasted_iota(jnp.int32, (tm, tm), 0)
    c_io = lax.broadcasted_iota(jnp.int32, (tm, tm), 1)
    before = jnp.where(r_io < c_io, 1.0, 0.0).astype(BF16)
    ones = jnp.ones((tm, tm), BF16)

    vals, idxs, hits = [], [], []
    l = logits
    for _ in range(TOP_K):
        m = jnp.max(l, axis=0, keepdims=True)
        sel = jnp.min(jnp.where(l == m, eid, float(n_exp)), axis=0, keepdims=True)
        hit = eid == sel
        l = jnp.where(hit, -jnp.inf, l)
        vals.append(m)
        idxs.append(sel)
        hits.append(hit)
    onehot = jnp.concatenate([jnp.where(h, 1.0, 0.0) for h in hits], axis=0).astype(BF16)
    prefix = jnp.dot(onehot, before, preferred_element_type=F32)
    count = jnp.dot(onehot, ones, preferred_element_type=F32)
    base = base_ref[...]
    ranks = []
    for k, hit in enumerate(hits):
        pk = prefix[k * n_exp:(k + 1) * n_exp, :]
        ranks.append(jnp.sum(jnp.where(hit, base + pk, 0.0), axis=0, keepdims=True))
        base = base + count[k * n_exp:(k + 1) * n_exp, :]
    base_ref[...] = base
    es = [jnp.exp(v - vals[0]) for v in vals]
    tot = es[0] + es[1] + es[2] + es[3]
    zero = jnp.zeros_like(tot)
    gate_ref[...] = jnp.concatenate([e / tot for e in es] + [zero] * (8 - TOP_K), axis=0)
    idx_ref[...] = jnp.concatenate(idxs, axis=0).astype(jnp.int32)
    rank_ref[...] = jnp.concatenate(ranks, axis=0).astype(jnp.int32)
    cnt_ref[...] = base_ref[:, :128].astype(jnp.int32)


def _router_weights(rw):
    hi = rw.T.astype(BF16)
    lo = (rw.T - hi.astype(F32)).astype(BF16)
    return jnp.concatenate([hi, lo], axis=0)


def _router_out(t, d, n_exp):
    shapes = (
        jax.ShapeDtypeStruct((t * SUBLANES, LANES), F32),
        jax.ShapeDtypeStruct((TOP_K, t), jnp.int32),
        jax.ShapeDtypeStruct((8, t), F32),
        jax.ShapeDtypeStruct((TOP_K, t), jnp.int32),
        jax.ShapeDtypeStruct((n_exp, 128), jnp.int32),
    )
    return shapes


def _router_out_specs(tile_of, tm, d, n_exp):
    return (
        pl.BlockSpec((tm * SUBLANES, LANES), lambda *g: (tile_of(*g), 0)),
        pl.BlockSpec((TOP_K, tm), lambda *g: (0, tile_of(*g))),
        pl.BlockSpec((8, tm), lambda *g: (0, tile_of(*g))),
        pl.BlockSpec((TOP_K, tm), lambda *g: (0, tile_of(*g))),
        pl.BlockSpec((n_exp, 128), lambda *g: (0, 0)),
    )


def _gmlp_kernel(n_ctx_tiles, tiles_per_sample, n_samples,
                 c_ref, x_ref, mod_ref, nm_ref, win_ref, bin_ref, vg_ref, ws_ref, bs_ref, wout_ref,
                 nf_ref, rwt_ref, rb_ref,
                 xo_ref, f_ref, idx_ref, gate_ref, rank_ref, cnt_ref, base_ref):
    i = pl.program_id(0)
    d = x_ref.shape[1]
    gw = wout_ref.shape[0]
    gc = gw // GM_GROUPS
    row = jnp.where(i < n_ctx_tiles, n_samples, (i - n_ctx_tiles) // tiles_per_sample)
    sh1, sc1, g1 = _mod_slices(mod_ref, row, d, 0)
    sh2, sc2, _ = _mod_slices(mod_ref, row, d, 3)

    x = jnp.where(i < n_ctx_tiles, c_ref[...], x_ref[...])
    h = _rms(x) * nm_ref[...] * (1.0 + sc1) + sh1
    hb = h.astype(BF16)

    def proj(col):
        a = jnp.dot(hb, win_ref[:, col:col + gc], preferred_element_type=F32) + bin_ref[:, col:col + gc]
        return 0.5 * a * (1.0 + lax.erf(a * (2.0 ** -0.5)))

    u = [proj(g * gc) for g in range(GM_GROUPS)]
    v = [proj(gw + g * gc) for g in range(GM_GROUPS)]
    ssq = v[0] * v[0]
    for g in range(1, GM_GROUPS):
        ssq = ssq + v[g] * v[g]
    inv = lax.rsqrt(jnp.sum(ssq, axis=-1, keepdims=True) * (1.0 / gw) + NORM_EPS)
    cols = []
    for g in range(GM_GROUPS):
        vn = (v[g] * inv * vg_ref[:, g * gc:(g + 1) * gc]).astype(BF16)
        s = jnp.concatenate(
            [jnp.dot(ws_ref[g], vn[c * CHUNK:(c + 1) * CHUNK, :], preferred_element_type=F32)
             + bs_ref[:, g * gc:(g + 1) * gc] for c in range(x.shape[0] // CHUNK)], axis=0)
        cols.append((u[g] * s).astype(BF16))
    z = jnp.concatenate(cols, axis=1)
    y = jnp.dot(z, wout_ref[...], preferred_element_type=F32)
    x_new = x + g1 * y
    xo_ref[...] = x_new
    _router_epilogue(i, x_new, sh2, sc2, nf_ref, rwt_ref, rb_ref,
                     f_ref, idx_ref, gate_ref, rank_ref, cnt_ref, base_ref)


def _gmlp_layer(ctx2d, x2d, mod, n_ctx_tiles, tiles_per_sample, n_samples,
                nm, w_in, b_in, v_gain, w_s, b_s, w_out, nf, rw, rb):
    d = x2d.shape[1]
    t = ctx2d.shape[0] + x2d.shape[0]
    gw = w_out.shape[0]
    n_exp = rw.shape[1]
    gc = gw // GM_GROUPS
    bs_full = jnp.repeat(b_s.T, gc, axis=1)
    tile = lambda i: i
    outs = pl.pallas_call(
        functools.partial(_gmlp_kernel, n_ctx_tiles, tiles_per_sample, n_samples),
        grid=(t // TG,),
        in_specs=[
            pl.BlockSpec((TG, d), lambda i: (jnp.minimum(i, n_ctx_tiles - 1), 0)),
            pl.BlockSpec((TG, d), lambda i: (jnp.maximum(i - n_ctx_tiles, 0), 0)),
            _const_spec((MOD_ROWS, N_MOD * d)),
            _const_spec((1, d)),
            _const_spec((d, 2 * gw)),
            _const_spec((1, 2 * gw)),
            _const_spec((1, gw)),
            _const_spec((GM_GROUPS, CHUNK, CHUNK)),
            _const_spec((CHUNK, gw)),
            _const_spec((gw, d)),
            _const_spec((1, d)),
            _const_spec((2 * n_exp, d)),
            _const_spec((n_exp, 1)),
        ],
        out_specs=(pl.BlockSpec((TG, d), lambda i: (i, 0)),) + _router_out_specs(tile, TG, d, n_exp),
        out_shape=(jax.ShapeDtypeStruct((t, d), F32),) + _router_out(t, d, n_exp),
        scratch_shapes=[pltpu.VMEM((n_exp, TG), F32)],
        compiler_params=_cparams(("arbitrary",)),
        name="gmlp_mixer",
    )(ctx2d, x2d, mod, nm.reshape(1, d), w_in.astype(BF16), b_in.reshape(1, -1), v_gain.reshape(1, gw),
      w_s.astype(BF16), bs_full, w_out.astype(BF16), nf.reshape(1, d), _router_weights(rw), rb.reshape(n_exp, 1))
    return outs


def _qkv_kernel(n_ctx_tiles, tiles_per_sample, n_samples,
                x_ref, mod_ref, nm_ref, w_ref, qg_ref, kg_ref, cos_ref, sin_ref,
                q_ref, k_ref, v_ref):
    i = pl.program_id(0)
    d = x_ref.shape[1]
    nq = q_ref.shape[1]
    nkv = k_ref.shape[2]
    row = jnp.where(i < n_ctx_tiles, n_samples, (i - n_ctx_tiles) // tiles_per_sample)
    sh1, sc1, _ = _mod_slices(mod_ref, row, d, 0)
    h = _rms(x_ref[...]) * nm_ref[...] * (1.0 + sc1) + sh1
    qkv = jnp.dot(h.astype(BF16), w_ref[...], preferred_element_type=F32)
    cos = cos_ref[...]
    sin = sin_ref[...]
    half = AXIS_DIM // 2
    lane = lax.broadcasted_iota(jnp.int32, (x_ref.shape[0], HEAD_DIM), 1)
    first_half = (lane % AXIS_DIM) < half

    def head(xh, gain):
        xh = _rms(xh) * gain
        partner = jnp.where(first_half, pltpu.roll(xh, HEAD_DIM - half, 1), pltpu.roll(xh, half, 1))
        return xh * cos + partner * sin

    q = [head(qkv[:, j * HEAD_DIM:(j + 1) * HEAD_DIM], qg_ref[...]) for j in range(nq // HEAD_DIM)]
    k = [head(qkv[:, nq + j * HEAD_DIM:nq + (j + 1) * HEAD_DIM], kg_ref[...])
         for j in range(nkv // HEAD_DIM)]
    q_ref[...] = jnp.concatenate(q, axis=1).astype(BF16)
    k_ref[0] = jnp.concatenate(k, axis=1).astype(BF16)
    v_ref[0] = qkv[:, nq + nkv:].astype(BF16)


def _rope_tables(n_lat):
    rows = n_lat // GRID_W
    row = jnp.repeat(jnp.arange(rows, dtype=jnp.int32), GRID_W).astype(F32)
    col = jnp.tile(jnp.arange(GRID_W, dtype=jnp.int32), rows).astype(F32)
    inv_freq = 1.0 / (ROPE_THETA ** (jnp.arange(0, AXIS_DIM, 2, dtype=F32) / AXIS_DIM))
    ang_r = row[:, None] * inv_freq
    ang_c = col[:, None] * inv_freq
    cos = jnp.concatenate([jnp.cos(ang_r)] * 2 + [jnp.cos(ang_c)] * 2, axis=1)
    sin = jnp.concatenate([-jnp.sin(ang_r), jnp.sin(ang_r), -jnp.sin(ang_c), jnp.sin(ang_c)], axis=1)
    cos = jnp.concatenate([jnp.ones((TM, HEAD_DIM), F32), cos], axis=0)
    sin = jnp.concatenate([jnp.zeros((TM, HEAD_DIM), F32), sin], axis=0)
    return cos, sin


def _qkv_layer(x_all, mod, n_ctx_tiles, tiles_per_sample, n_samples, n_ctx, n_lat,
               nm, w_qkv, q_gain, k_gain):
    t, d = x_all.shape
    nqkv = w_qkv.shape[1]
    nkv = N_KV_HEADS * HEAD_DIM
    nq = nqkv - 2 * nkv
    cos, sin = _rope_tables(n_lat)
    ctx_blocks = n_ctx // TM
    tps = tiles_per_sample

    def pos_block(i):
        return jnp.where(i < n_ctx_tiles, 0, 1 + (i - n_ctx_tiles) % tps)

    def kv_map(i):
        lat = i >= n_ctx_tiles
        b = jnp.where(lat, (i - n_ctx_tiles) // tps, i // ctx_blocks)
        j = jnp.where(lat, ctx_blocks + (i - n_ctx_tiles) % tps, i % ctx_blocks)
        return (b, j, 0)

    return pl.pallas_call(
        functools.partial(_qkv_kernel, n_ctx_tiles, tiles_per_sample, n_samples),
        grid=(t // TM,),
        in_specs=[
            pl.BlockSpec((TM, d), lambda i: (i, 0)),
            _const_spec((MOD_ROWS, N_MOD * d)),
            _const_spec((1, d)),
            _const_spec((d, nqkv)),
            _const_spec((1, HEAD_DIM)),
            _const_spec((1, HEAD_DIM)),
            pl.BlockSpec((TM, HEAD_DIM), lambda i: (pos_block(i), 0)),
            pl.BlockSpec((TM, HEAD_DIM), lambda i: (pos_block(i), 0)),
        ],
        out_specs=(
            pl.BlockSpec((TM, nq), lambda i: (i, 0)),
            pl.BlockSpec((1, TM, nkv), kv_map),
            pl.BlockSpec((1, TM, nkv), kv_map),
        ),
        out_shape=(
            jax.ShapeDtypeStruct((t, nq), BF16),
            jax.ShapeDtypeStruct((n_samples, n_ctx + n_lat, nkv), BF16),
            jax.ShapeDtypeStruct((n_samples, n_ctx + n_lat, nkv), BF16),
        ),
        compiler_params=_cparams(("arbitrary",)),
        name="qkv_rope",
    )(x_all, mod, nm.reshape(1, d), w_qkv.astype(BF16), q_gain.reshape(1, HEAD_DIM),
      k_gain.reshape(1, HEAD_DIM), cos, sin)


def _attn_kernel(tiles_per_sample,
                 x_ref, q_ref, k_ref, v_ref, mod_ref, wo_ref, nf_ref, rwt_ref, rb_ref,
                 xo_ref, f_ref, idx_ref, gate_ref, rank_ref, cnt_ref, base_ref):
    b = pl.program_id(0)
    j = pl.program_id(1)
    d = x_ref.shape[1]
    n_heads = q_ref.shape[1] // HEAD_DIM
    group = n_heads // N_KV_HEADS
    _, _, g1 = _mod_slices(mod_ref, b, d, 0)
    sh2, sc2, _ = _mod_slices(mod_ref, b, d, 3)
    tq = q_ref.shape[0]
    exp2_scale = (HEAD_DIM ** -0.5) * 1.4426950408889634
    outs = []
    hpd = ATTN_HEADS_PER_DOT
    for h0 in range(0, n_heads, hpd):
        g = h0 // group
        qg = jnp.concatenate([q_ref[:, h * HEAD_DIM:(h + 1) * HEAD_DIM]
                              for h in range(h0, h0 + hpd)], axis=0)
        kg = k_ref[0, :, g * HEAD_DIM:(g + 1) * HEAD_DIM]
        vg = v_ref[0, :, g * HEAD_DIM:(g + 1) * HEAD_DIM]
        s = lax.dot_general(qg, kg, (((1,), (1,)), ((), ())), preferred_element_type=F32)
        p = jnp.exp2((s - jnp.max(s, axis=-1, keepdims=True)) * exp2_scale)
        den = jnp.sum(p, axis=-1, keepdims=True)
        pb = p.astype(BF16)
        for h in range(hpd):
            rows = slice(h * tq, (h + 1) * tq)
            o = jnp.dot(pb[rows, :], vg, preferred_element_type=F32) / den[rows, :]
            outs.append(o.astype(BF16))
    o_all = jnp.concatenate(outs, axis=1)
    y = jnp.dot(o_all, wo_ref[...], preferred_element_type=F32)
    x_new = x_ref[...] + g1 * y
    xo_ref[...] = x_new
    _router_epilogue(b * tiles_per_sample + j, x_new, sh2, sc2, nf_ref, rwt_ref, rb_ref,
                     f_ref, idx_ref, gate_ref, rank_ref, cnt_ref, base_ref)


def _attn_layer(x_all, q, k_all, v_all, mod, n_ctx_tiles, tiles_per_sample, n_samples,
                w_o, nf, rw, rb):
    d = x_all.shape[1]
    t_lat = n_samples * tiles_per_sample * TA
    n_exp = rw.shape[1]
    nq = q.shape[1]
    lk, nkv = k_all.shape[1], k_all.shape[2]
    tps = tiles_per_sample
    lat_tile = lambda b, j: n_ctx_tiles + b * tps + j
    tile = lambda b, j: b * tps + j
    return pl.pallas_call(
        functools.partial(_attn_kernel, tiles_per_sample),
        grid=(n_samples, tps),
        in_specs=[
            pl.BlockSpec((TA, d), lambda b, j: (lat_tile(b, j), 0)),
            pl.BlockSpec((TA, nq), lambda b, j: (lat_tile(b, j), 0)),
            pl.BlockSpec((1, lk, nkv), lambda b, j: (b, 0, 0)),
            pl.BlockSpec((1, lk, nkv), lambda b, j: (b, 0, 0)),
            _const_spec((MOD_ROWS, N_MOD * d)),
            _const_spec((nq, d)),
            _const_spec((1, d)),
            _const_spec((2 * n_exp, d)),
            _const_spec((n_exp, 1)),
        ],
        out_specs=(pl.BlockSpec((TA, d), lambda b, j: (tile(b, j), 0)),)
        + _router_out_specs(tile, TA, d, n_exp),
        out_shape=(jax.ShapeDtypeStruct((t_lat, d), F32),) + _router_out(t_lat, d, n_exp),
        scratch_shapes=[pltpu.VMEM((n_exp, TA), F32)],
        compiler_params=_cparams(("arbitrary", "arbitrary")),
        name="attn_mixer",
    )(x_all, q, k_all, v_all, mod, w_o.astype(BF16), nf.reshape(1, d), _router_weights(rw), rb.reshape(n_exp, 1))


def _dispatch_kernel(off_ref, tail_ref, used_ref, idx_ref, rank_ref, f_ref, xs_hbm, dest_ref,
                     zeros, sem, zsem):
    tm = idx_ref.shape[1]
    n_exp = tail_ref.shape[0]
    block_rows = zeros.shape[0]
    n_blocks = xs_hbm.shape[0] // block_rows

    @pl.when(pl.program_id(0) == 0)
    def _():
        zeros[...] = jnp.zeros_like(zeros)

        def fill(blk):
            start = pl.multiple_of(blk * block_rows, block_rows)
            return pltpu.make_async_copy(zeros, xs_hbm.at[pl.ds(start, block_rows)], zsem)

        def each(fn):
            def tail(e, c):
                @pl.when(tail_ref[e] >= 0)
                def _():
                    fn(fill(tail_ref[e]))
                return c

            def unused(j, c):
                @pl.when(used_ref[0] + j < n_blocks)
                def _():
                    fn(fill(used_ref[0] + j))
                return c

            lax.fori_loop(0, n_exp, tail, 0)
            lax.fori_loop(0, n_exp, unused, 0)

        each(lambda c: c.start())
        each(lambda c: c.wait())

    def start(t, c):
        src = f_ref.at[pl.ds(pl.multiple_of(t * SUBLANES, SUBLANES), SUBLANES)]
        for k in range(TOP_K):
            dst = off_ref[idx_ref[k, t]] + rank_ref[k, t]
            dest_ref[k, t] = dst
            row = xs_hbm.at[pl.ds(pl.multiple_of(dst * SUBLANES, SUBLANES), SUBLANES)]
            pltpu.make_async_copy(src, row, sem).start(priority=k % 2)
        return c

    lax.fori_loop(0, tm, start, 0)
    for k in range(TOP_K):
        pltpu.make_async_copy(f_ref, xs_hbm.at[pl.ds(0, tm * SUBLANES)], sem).wait()


def _dispatch(f, idx, rank, pad_start, tail_blk, used_blocks, n_blocks):
    t = idx.shape[1]
    smem_tile = pl.BlockSpec((TOP_K, TM), lambda i, *_: (0, i), memory_space=pltpu.SMEM)
    grid_spec = pltpu.PrefetchScalarGridSpec(
        num_scalar_prefetch=3,
        grid=(t // TM,),
        in_specs=[smem_tile, smem_tile, pl.BlockSpec((TM * SUBLANES, LANES), lambda i, *_: (i, 0))],
        out_specs=(pl.BlockSpec(memory_space=pl.ANY), smem_tile),
        scratch_shapes=[pltpu.VMEM((TE * SUBLANES, LANES), F32),
                        pltpu.SemaphoreType.DMA, pltpu.SemaphoreType.DMA],
    )
    return pl.pallas_call(
        _dispatch_kernel,
        grid_spec=grid_spec,
        out_shape=(jax.ShapeDtypeStruct((n_blocks * TE * SUBLANES, LANES), F32),
                   jax.ShapeDtypeStruct((TOP_K, t), jnp.int32)),
        compiler_params=_cparams(("arbitrary",)),
        name="moe_dispatch",
    )(pad_start, tail_blk, used_blocks, idx, rank, f)


def _expert_kernel(layer, exp_ref, used_ref, nxt_ref,
                   xs_ref, wgu_hbm, bgu_ref, wd_hbm, bd_ref, o_ref,
                   gu_stage, d_stage, wgu_b, wd_b, sem):
    w = pl.program_id(0)
    tm = xs_ref.shape[0] // SUBLANES
    de = wd_b.shape[0]
    new_expert = (w == 0) | (exp_ref[w] != exp_ref[jnp.maximum(w - 1, 0)])

    def weight_copies(e):
        return (pltpu.make_async_copy(wgu_hbm.at[layer, e], gu_stage, sem.at[0]),
                pltpu.make_async_copy(wd_hbm.at[layer, e], d_stage, sem.at[1]))

    @pl.when(w == 0)
    def _():
        for c in weight_copies(exp_ref[0]):
            c.start()

    @pl.when(new_expert)
    def _():
        for c in weight_copies(exp_ref[w]):
            c.wait()
        wgu_b[...] = gu_stage[...].astype(BF16)
        wd_b[...] = d_stage[...].astype(BF16)

        @pl.when(nxt_ref[w] >= 0)
        def _():
            for c in weight_copies(nxt_ref[w]):
                c.start()

    @pl.when(used_ref[w] == 0)
    def _():
        o_ref[...] = jnp.zeros_like(o_ref)

    @pl.when(used_ref[w] != 0)
    def _():
        x = _load_tile_rows(xs_ref, tm).astype(BF16)
        gu = jnp.dot(x, wgu_b[...], preferred_element_type=F32) + bgu_ref[0]
        g = jnp.minimum(gu[:, :de], SWIGLU_LIMIT)
        u = jnp.clip(gu[:, de:], -SWIGLU_LIMIT, SWIGLU_LIMIT)
        act = g * _sigmoid(SWIGLU_ALPHA * g) * (u + 1.0)
        y = jnp.dot(act.astype(BF16), wd_b[...], preferred_element_type=F32) + bd_ref[0]
        _store_tile_rows(o_ref, y)


def _experts(xs, work, layer, w_gu, b_gu, w_down, b_down):
    _, n_exp, d, de2 = w_gu.shape
    de = w_down.shape[2]
    n_blocks = work[0].shape[0]
    item = lambda w, ex, used, nxt: (w, 0)
    bias = lambda w, ex, used, nxt: (ex[w], 0, 0)
    grid_spec = pltpu.PrefetchScalarGridSpec(
        num_scalar_prefetch=3,
        grid=(n_blocks,),
        in_specs=[
            pl.BlockSpec((TE * SUBLANES, LANES), item),
            pl.BlockSpec(memory_space=pl.ANY),
            pl.BlockSpec((1, 1, de2), bias),
            pl.BlockSpec(memory_space=pl.ANY),
            pl.BlockSpec((1, 1, d), bias),
        ],
        out_specs=pl.BlockSpec((TE * SUBLANES, LANES), item),
        scratch_shapes=[pltpu.VMEM((d, de2), F32), pltpu.VMEM((de, d), F32),
                        pltpu.VMEM((d, de2), BF16), pltpu.VMEM((de, d), BF16),
                        pltpu.SemaphoreType.DMA((2,))],
    )
    return pl.pallas_call(
        functools.partial(_expert_kernel, layer),
        grid_spec=grid_spec,
        out_shape=jax.ShapeDtypeStruct(xs.shape, F32),
        compiler_params=_cparams(("arbitrary",)),
        name="moe_experts",
    )(*work, xs, w_gu, b_gu[layer].reshape(n_exp, 1, de2), w_down, b_down[layer].reshape(n_exp, 1, d))


def _combine_kernel(row_of_tile, dest_ref, next_ref, gate_ref, x_ref, mod_ref, yb_hbm, o_ref, buf, sem):
    i = pl.program_id(0)
    n_tiles = pl.num_programs(0)
    tm, d = x_ref.shape
    half = i % 2

    def fetch(idx_ref, h):
        def start(t, c):
            slot = pl.ds(pl.multiple_of(t * SUBLANES, SUBLANES), SUBLANES)
            for k in range(TOP_K):
                row = pl.ds(pl.multiple_of(idx_ref[k, t] * SUBLANES, SUBLANES), SUBLANES)
                pltpu.make_async_copy(yb_hbm.at[row], buf.at[h, k, slot], sem.at[h]).start(priority=k % 2)
            return c

        lax.fori_loop(0, tm, start, 0)

    @pl.when(i == 0)
    def _():
        fetch(dest_ref, 0)

    @pl.when(i + 1 < n_tiles)
    def _():
        fetch(next_ref, 1 - half)

    g2 = mod_ref[pl.ds(row_of_tile(i), 1), pl.ds(5 * d, d)]
    gates = jnp.concatenate([gate_ref[...], jnp.zeros((LANES - 8, tm), F32)], axis=0).T
    for k in range(TOP_K):
        pltpu.make_async_copy(yb_hbm.at[pl.ds(0, tm * SUBLANES)], buf.at[half, k], sem.at[half]).wait()
    y = _load_tile_rows(buf.at[half, 0], tm) * gates[:, 0:1]
    for k in range(1, TOP_K):
        y = y + _load_tile_rows(buf.at[half, k], tm) * gates[:, k:k + 1]
    o_ref[...] = x_ref[...] + g2 * y


def _combine(x_new, yb, dest, gates, mod, row_of_tile):
    t, d = x_new.shape
    last = t // TM - 1
    return pl.pallas_call(
        functools.partial(_combine_kernel, row_of_tile),
        grid=(t // TM,),
        in_specs=[
            pl.BlockSpec((TOP_K, TM), lambda i: (0, i), memory_space=pltpu.SMEM),
            pl.BlockSpec((TOP_K, TM), lambda i: (0, jnp.minimum(i + 1, last)), memory_space=pltpu.SMEM),
            pl.BlockSpec((8, TM), lambda i: (0, i)),
            pl.BlockSpec((TM, d), lambda i: (i, 0)),
            _const_spec((MOD_ROWS, N_MOD * d)),
            pl.BlockSpec(memory_space=pl.ANY),
        ],
        out_specs=pl.BlockSpec((TM, d), lambda i: (i, 0)),
        out_shape=jax.ShapeDtypeStruct((t, d), F32),
        scratch_shapes=[pltpu.VMEM((2, TOP_K, TM * SUBLANES, LANES), F32),
                        pltpu.SemaphoreType.DMA((2,))],
        compiler_params=_cparams(("arbitrary",)),
        name="moe_combine",
    )(dest, dest, gates, x_new, mod, yb)


def _expert_blocks(counts, n_assign):
    n_exp = counts.shape[0]
    n_blocks = n_assign // TE + n_exp
    padded = (counts + TE - 1) // TE * TE
    pad_end = jnp.cumsum(padded)
    pad_start = (pad_end - padded).astype(jnp.int32)
    used_blocks = pad_end[-1] // TE
    w = jnp.arange(n_blocks, dtype=jnp.int32)
    used = w < used_blocks
    ex = jnp.minimum(jnp.sum(w[:, None] * TE >= pad_end[None, :], axis=1), n_exp - 1).astype(jnp.int32)
    ex = jnp.where(used, ex, jnp.max(jnp.where(used, ex, 0)))
    ids = jnp.arange(n_exp, dtype=jnp.int32)
    later = jnp.where((ids[None, :] > ids[:, None]) & (counts[None, :] > 0), ids[None, :], n_exp)
    nxt_of = jnp.min(later, axis=1)
    nxt = jnp.sum(jnp.where(ex[:, None] == ids[None, :], nxt_of[None, :], 0), axis=1)
    nxt = jnp.where(nxt < n_exp, nxt, -1).astype(jnp.int32)
    tail_blk = jnp.where(counts > 0, pad_end // TE - 1, -1).astype(jnp.int32)
    work = (ex, used.astype(jnp.int32), nxt)
    return work, pad_start, tail_blk, used_blocks.astype(jnp.int32).reshape(1), n_blocks


def _moe(x_new, f, idx, gates, rank, cnt, mod, row_of_tile, layer, w_gu, b_gu, w_down, b_down):
    t = idx.shape[1]
    work, pad_start, tail_blk, used_blocks, n_blocks = _expert_blocks(cnt[:, 0], t * TOP_K)
    xs, dest = _dispatch(f, idx, rank, pad_start, tail_blk, used_blocks, n_blocks)
    yb = _experts(xs, work, layer, w_gu, b_gu, w_down, b_down)
    return _combine(x_new, yb, dest, gates, mod, row_of_tile)


def kernel(x, c, ctx, c_ctx, ada_w, ada_b, norm_mix, norm_ffn, gm_w_in, gm_b_in, gm_v_gain, gm_w_s,
           gm_b_s, gm_w_out, at_w_qkv, at_q_gain, at_k_gain, at_w_o, moe_router_w, moe_router_b,
           moe_w_gu, moe_b_gu, moe_w_down, moe_b_down):
    n_samples, n_lat, d = x.shape
    n_ctx = ctx.shape[1]
    assert d == SUBLANES * LANES, "MoE row movement assumes one f32 tile per token row"
    assert n_lat % TM == 0 and n_ctx % TM == 0 and n_samples < MOD_ROWS
    assert n_lat % TG == 0 and (n_samples * n_ctx) % TG == 0 and TG % CHUNK == 0
    assert n_lat % TA == 0 and (n_samples * n_ctx) % TA == 0
    assert (n_samples * n_lat * TOP_K) % TE == 0 and (n_samples * n_ctx * TOP_K) % TE == 0
    n_ctx_tiles = n_samples * n_ctx // TM
    tps = n_lat // TM

    mods = _ada_table(c, c_ctx, ada_w, ada_b)

    def row_all(i):
        return jnp.where(i < n_ctx_tiles, n_samples, (i - n_ctx_tiles) // tps)

    x_new, f, idx, gates, rank, cnt = _gmlp_layer(
        ctx.reshape(-1, d), x.reshape(-1, d), mods[0], n_samples * n_ctx // TG, n_lat // TG, n_samples, norm_mix[0], gm_w_in[0], gm_b_in[0],
        gm_v_gain[0], gm_w_s[0], gm_b_s[0], gm_w_out[0], norm_ffn[0], moe_router_w[0],
        moe_router_b[0])
    x_all = _moe(x_new, f, idx, gates, rank, cnt, mods[0], row_all,
                 0, moe_w_gu, moe_b_gu, moe_w_down, moe_b_down)

    q, k_all, v_all = _qkv_layer(x_all, mods[1], n_ctx_tiles, tps, n_samples, n_ctx, n_lat,
                                 norm_mix[1], at_w_qkv[0], at_q_gain[0], at_k_gain[0])
    x_new, f, idx, gates, rank, cnt = _attn_layer(
        x_all, q, k_all, v_all, mods[1], n_samples * n_ctx // TA, n_lat // TA, n_samples, at_w_o[0], norm_ffn[1],
        moe_router_w[1], moe_router_b[1])
    out = _moe(x_new, f, idx, gates, rank, cnt, mods[1], lambda i: i // tps,
               1, moe_w_gu, moe_b_gu, moe_w_down, moe_b_down)
    return out.reshape(n_samples, n_lat, d)
```

```python
import functools

import jax
import jax.numpy as jnp
from jax import lax
from jax.experimental import pallas as pl
from jax.experimental.pallas import tpu as pltpu

F32 = jnp.float32
BF16 = jnp.bfloat16
HIGHEST = lax.Precision.HIGHEST

GRID_W = 64
N_MOD = 6
NORM_EPS = 1e-6
CHUNK = 128
GM_GROUPS = 8
HEAD_DIM = 128
N_KV_HEADS = 2
AXIS_DIM = HEAD_DIM // 2
ROPE_THETA = 10000.0
TOP_K = 4
SWIGLU_LIMIT = 7.0
SWIGLU_ALPHA = 1.702

TM = 256
TG = 512
TA = 512
TE = 512
ATTN_HEADS_PER_DOT = 2
MOD_ROWS = 16
V7X_VMEM_LIMIT = 56 * 1024 * 1024


def _cparams(sem, vmem=V7X_VMEM_LIMIT):
    return pltpu.CompilerParams(dimension_semantics=sem, vmem_limit_bytes=vmem)


def _const_spec(shape):
    nd = len(shape)
    return pl.BlockSpec(shape, lambda *_: (0,) * nd, pipeline_mode=pl.Buffered(1))


LANES = 128
SUBLANES = 8


def _load_tile_rows(ref, n):
    return jnp.concatenate([ref[pl.ds(s, n, stride=SUBLANES), :] for s in range(SUBLANES)], axis=1)


def _store_tile_rows(ref, val):
    n = val.shape[0]
    for s in range(SUBLANES):
        ref[pl.ds(s, n, stride=SUBLANES), :] = val[:, s * LANES:(s + 1) * LANES]


def _sigmoid(x):
    return 1.0 / (1.0 + jnp.exp(-x))


def _rms(x):
    return x * lax.rsqrt(jnp.mean(x * x, axis=-1, keepdims=True) + NORM_EPS)


def _ada_kernel(s_ref, w_ref, b_ref, o_ref):
    s = s_ref[...]
    s = s * _sigmoid(s)
    o_ref[0] = jnp.dot(s, w_ref[0], precision=HIGHEST, preferred_element_type=F32) + b_ref[0]


def _ada_table(c, c_ctx, ada_w, ada_b):
    depth, d, n = ada_w.shape
    b = c.shape[0]
    s = jnp.concatenate([c, c_ctx[None, :], jnp.zeros((MOD_ROWS - b - 1, d), F32)], axis=0)
    tn = 1536
    return pl.pallas_call(
        _ada_kernel,
        grid=(depth, n // tn),
        in_specs=[
            pl.BlockSpec((MOD_ROWS, d), lambda i, j: (0, 0)),
            pl.BlockSpec((1, d, tn), lambda i, j: (i, 0, j)),
            pl.BlockSpec((1, 1, tn), lambda i, j: (i, 0, j)),
        ],
        out_specs=pl.BlockSpec((1, MOD_ROWS, tn), lambda i, j: (i, 0, j)),
        out_shape=jax.ShapeDtypeStruct((depth, MOD_ROWS, n), F32),
        compiler_params=_cparams(("arbitrary", "arbitrary")),
        name="ada_table",
    )(s, ada_w, ada_b.reshape(depth, 1, n))


def _mod_slices(mod_ref, row, d, first):
    return [mod_ref[pl.ds(row, 1), pl.ds((first + k) * d, d)] for k in range(3)]


def _router_epilogue(step, x_new, sh2, sc2, nf_ref, rwt_ref, rb_ref,
                     f_ref, idx_ref, gate_ref, rank_ref, cnt_ref, base_ref):
    tm = x_new.shape[0]
    n_exp = rwt_ref.shape[0] // 2
    f = _rms(x_new) * nf_ref[...] * (1.0 + sc2) + sh2
    _store_tile_rows(f_ref, f)

    nt = (((1,), (1,)), ((), ()))
    f_hi = f.astype(BF16)
    f_lo = (f - f_hi.astype(F32)).astype(BF16)
    l_hi = lax.dot_general(rwt_ref[...], f_hi, nt, preferred_element_type=F32)
    l_lo = lax.dot_general(rwt_ref[:n_exp, :], f_lo, nt, preferred_element_type=F32)
    logits = l_hi[:n_exp, :] + l_hi[n_exp:, :] + l_lo + rb_ref[...]
    eid = lax.broadcasted_iota(jnp.int32, (n_exp, tm), 0).astype(F32)

    @pl.when(step == 0)
    def _():
        base_ref[...] = jnp.zeros_like(base_ref)

    r_io = lax.broadcasted_iota(jnp.int32, (tm, tm), 0)
    c_io = lax.broadcasted_iota(jnp.int32, (tm, tm), 1)
    before = jnp.where(r_io < c_io, 1.0, 0.0).astype(BF16)
    ones = jnp.ones((tm, tm), BF16)

    vals, idxs, hits = [], [], []
    l = logits
    for _ in range(TOP_K):
        m = jnp.max(l, axis=0, keepdims=True)
        sel = jnp.min(jnp.where(l == m, eid, float(n_exp)), axis=0, keepdims=True)
        hit = eid == sel
        l = jnp.where(hit, -jnp.inf, l)
        vals.append(m)
        idxs.append(sel)
        hits.append(hit)
    onehot = jnp.concatenate([jnp.where(h, 1.0, 0.0) for h in hits], axis=0).astype(BF16)
    prefix = jnp.dot(onehot, before, preferred_element_type=F32)
    count = jnp.dot(onehot, ones, preferred_element_type=F32)
    base = base_ref[...]
    ranks = []
    for k, hit in enumerate(hits):
        pk = prefix[k * n_exp:(k + 1) * n_exp, :]
        ranks.append(jnp.sum(jnp.where(hit, base + pk, 0.0), axis=0, keepdims=True))
        base = base + count[k * n_exp:(k + 1) * n_exp, :]
    base_ref[...] = base
    es = [jnp.exp(v - vals[0]) for v in vals]
    tot = es[0] + es[1] + es[2] + es[3]
    zero = jnp.zeros_like(tot)
    gate_ref[...] = jnp.concatenate([e / tot for e in es] + [zero] * (8 - TOP_K), axis=0)
    idx_ref[...] = jnp.concatenate(idxs, axis=0).astype(jnp.int32)
    rank_ref[...] = jnp.concatenate(ranks, axis=0).astype(jnp.int32)
    cnt_ref[...] = base_ref[:, :128].astype(jnp.int32)


def _router_weights(rw):
    hi = rw.T.astype(BF16)
    lo = (rw.T - hi.astype(F32)).astype(BF16)
    return jnp.concatenate([hi, lo], axis=0)


def _router_out(t, d, n_exp):
    shapes = (
        jax.ShapeDtypeStruct((t * SUBLANES, LANES), F32),
        jax.ShapeDtypeStruct((TOP_K, t), jnp.int32),
        jax.ShapeDtypeStruct((8, t), F32),
        jax.ShapeDtypeStruct((TOP_K, t), jnp.int32),
        jax.ShapeDtypeStruct((n_exp, 128), jnp.int32),
    )
    return shapes


def _router_out_specs(tile_of, tm, d, n_exp):
    return (
        pl.BlockSpec((tm * SUBLANES, LANES), lambda *g: (tile_of(*g), 0)),
        pl.BlockSpec((TOP_K, tm), lambda *g: (0, tile_of(*g))),
        pl.BlockSpec((8, tm), lambda *g: (0, tile_of(*g))),
        pl.BlockSpec((TOP_K, tm), lambda *g: (0, tile_of(*g))),
        pl.BlockSpec((n_exp, 128), lambda *g: (0, 0)),
    )


def _gmlp_kernel(n_ctx_tiles, tiles_per_sample, n_samples,
                 c_ref, x_ref, mod_ref, nm_ref, win_ref, bin_ref, vg_ref, ws_ref, bs_ref, wout_ref,
                 nf_ref, rwt_ref, rb_ref,
                 xo_ref, f_ref, idx_ref, gate_ref, rank_ref, cnt_ref, base_ref):
    i = pl.program_id(0)
    d = x_ref.shape[1]
    gw = wout_ref.shape[0]
    gc = gw // GM_GROUPS
    row = jnp.where(i < n_ctx_tiles, n_samples, (i - n_ctx_tiles) // tiles_per_sample)
    sh1, sc1, g1 = _mod_slices(mod_ref, row, d, 0)
    sh2, sc2, _ = _mod_slices(mod_ref, row, d, 3)

    x = jnp.where(i < n_ctx_tiles, c_ref[...], x_ref[...])
    h = _rms(x) * nm_ref[...] * (1.0 + sc1) + sh1
    hb = h.astype(BF16)

    def proj(col):
        a = jnp.dot(hb, win_ref[:, col:col + gc], preferred_element_type=F32) + bin_ref[:, col:col + gc]
        return 0.5 * a * (1.0 + lax.erf(a * (2.0 ** -0.5)))

    u = [proj(g * gc) for g in range(GM_GROUPS)]
    v = [proj(gw + g * gc) for g in range(GM_GROUPS)]
    ssq = v[0] * v[0]
    for g in range(1, GM_GROUPS):
        ssq = ssq + v[g] * v[g]
    inv = lax.rsqrt(jnp.sum(ssq, axis=-1, keepdims=True) * (1.0 / gw) + NORM_EPS)
    cols = []
    for g in range(GM_GROUPS):
        vn = (v[g] * inv * vg_ref[:, g * gc:(g + 1) * gc]).astype(BF16)
        s = jnp.concatenate(
            [jnp.dot(ws_ref[g], vn[c * CHUNK:(c + 1) * CHUNK, :], preferred_element_type=F32)
             + bs_ref[:, g * gc:(g + 1) * gc] for c in range(x.shape[0] // CHUNK)], axis=0)
        cols.append((u[g] * s).astype(BF16))
    z = jnp.concatenate(cols, axis=1)
    y = jnp.dot(z, wout_ref[...], preferred_element_type=F32)
    x_new = x + g1 * y
    xo_ref[...] = x_new
    _router_epilogue(i, x_new, sh2, sc2, nf_ref, rwt_ref, rb_ref,
                     f_ref, idx_ref, gate_ref, rank_ref, cnt_ref, base_ref)


def _gmlp_layer(ctx2d, x2d, mod, n_ctx_tiles, tiles_per_sample, n_samples,
                nm, w_in, b_in, v_gain, w_s, b_s, w_out, nf, rw, rb):
    d = x2d.shape[1]
    t = ctx2d.shape[0] + x2d.shape[0]
    gw = w_out.shape[0]
    n_exp = rw.shape[1]
    gc = gw // GM_GROUPS
    bs_full = jnp.repeat(b_s.T, gc, axis=1)
    tile = lambda i: i
    outs = pl.pallas_call(
        functools.partial(_gmlp_kernel, n_ctx_tiles, tiles_per_sample, n_samples),
        grid=(t // TG,),
        in_specs=[
            pl.BlockSpec((TG, d), lambda i: (jnp.minimum(i, n_ctx_tiles - 1), 0)),
            pl.BlockSpec((TG, d), lambda i: (jnp.maximum(i - n_ctx_tiles, 0), 0)),
            _const_spec((MOD_ROWS, N_MOD * d)),
            _const_spec((1, d)),
            _const_spec((d, 2 * gw)),
            _const_spec((1, 2 * gw)),
            _const_spec((1, gw)),
            _const_spec((GM_GROUPS, CHUNK, CHUNK)),
            _const_spec((CHUNK, gw)),
            _const_spec((gw, d)),
            _const_spec((1, d)),
            _const_spec((2 * n_exp, d)),
            _const_spec((n_exp, 1)),
        ],
        out_specs=(pl.BlockSpec((TG, d), lambda i: (i, 0)),) + _router_out_specs(tile, TG, d, n_exp),
        out_shape=(jax.ShapeDtypeStruct((t, d), F32),) + _router_out(t, d, n_exp),
        scratch_shapes=[pltpu.VMEM((n_exp, TG), F32)],
        compiler_params=_cparams(("arbitrary",)),
        name="gmlp_mixer",
    )(ctx2d, x2d, mod, nm.reshape(1, d), w_in.astype(BF16), b_in.reshape(1, -1), v_gain.reshape(1, gw),
      w_s.astype(BF16), bs_full, w_out.astype(BF16), nf.reshape(1, d), _router_weights(rw), rb.reshape(n_exp, 1))
    return outs


def _qkv_kernel(n_ctx_tiles, tiles_per_sample, n_samples,
                x_ref, mod_ref, nm_ref, w_ref, qg_ref, kg_ref, cos_ref, sin_ref,
                q_ref, k_ref, v_ref):
    i = pl.program_id(0)
    d = x_ref.shape[1]
    nq = q_ref.shape[1]
    nkv = k_ref.shape[2]
    row = jnp.where(i < n_ctx_tiles, n_samples, (i - n_ctx_tiles) // tiles_per_sample)
    sh1, sc1, _ = _mod_slices(mod_ref, row, d, 0)
    h = _rms(x_ref[...]) * nm_ref[...] * (1.0 + sc1) + sh1
    qkv = jnp.dot(h.astype(BF16), w_ref[...], preferred_element_type=F32)
    cos = cos_ref[...]
    sin = sin_ref[...]
    half = AXIS_DIM // 2
    lane = lax.broadcasted_iota(jnp.int32, (x_ref.shape[0], HEAD_DIM), 1)
    first_half = (lane % AXIS_DIM) < half

    def head(xh, gain):
        xh = _rms(xh) * gain
        partner = jnp.where(first_half, pltpu.roll(xh, HEAD_DIM - half, 1), pltpu.roll(xh, half, 1))
        return xh * cos + partner * sin

    q = [head(qkv[:, j * HEAD_DIM:(j + 1) * HEAD_DIM], qg_ref[...]) for j in range(nq // HEAD_DIM)]
    k = [head(qkv[:, nq + j * HEAD_DIM:nq + (j + 1) * HEAD_DIM], kg_ref[...])
         for j in range(nkv // HEAD_DIM)]
    q_ref[...] = jnp.concatenate(q, axis=1).astype(BF16)
    k_ref[0] = jnp.concatenate(k, axis=1).astype(BF16)
    v_ref[0] = qkv[:, nq + nkv:].astype(BF16)


def _rope_tables(n_lat):
    rows = n_lat // GRID_W
    row = jnp.repeat(jnp.arange(rows, dtype=jnp.int32), GRID_W).astype(F32)
    col = jnp.tile(jnp.arange(GRID_W, dtype=jnp.int32), rows).astype(F32)
    inv_freq = 1.0 / (ROPE_THETA ** (jnp.arange(0, AXIS_DIM, 2, dtype=F32) / AXIS_DIM))
    ang_r = row[:, None] * inv_freq
    ang_c = col[:, None] * inv_freq
    cos = jnp.concatenate([jnp.cos(ang_r)] * 2 + [jnp.cos(ang_c)] * 2, axis=1)
    sin = jnp.concatenate([-jnp.sin(ang_r), jnp.sin(ang_r), -jnp.sin(ang_c), jnp.sin(ang_c)], axis=1)
    cos = jnp.concatenate([jnp.ones((TM, HEAD_DIM), F32), cos], axis=0)
    sin = jnp.concatenate([jnp.zeros((TM, HEAD_DIM), F32), sin], axis=0)
    return cos, sin


def _qkv_layer(x_all, mod, n_ctx_tiles, tiles_per_sample, n_samples, n_ctx, n_lat,
               nm, w_qkv, q_gain, k_gain):
    t, d = x_all.shape
    nqkv = w_qkv.shape[1]
    nkv = N_KV_HEADS * HEAD_DIM
    nq = nqkv - 2 * nkv
    cos, sin = _rope_tables(n_lat)
    ctx_blocks = n_ctx // TM
    tps = tiles_per_sample

    def pos_block(i):
        return jnp.where(i < n_ctx_tiles, 0, 1 + (i - n_ctx_tiles) % tps)

    def kv_map(i):
        lat = i >= n_ctx_tiles
        b = jnp.where(lat, (i - n_ctx_tiles) // tps, i // ctx_blocks)
        j = jnp.where(lat, ctx_blocks + (i - n_ctx_tiles) % tps, i % ctx_blocks)
        return (b, j, 0)

    return pl.pallas_call(
        functools.partial(_qkv_kernel, n_ctx_tiles, tiles_per_sample, n_samples),
        grid=(t // TM,),
        in_specs=[
            pl.BlockSpec((TM, d), lambda i: (i, 0)),
            _const_spec((MOD_ROWS, N_MOD * d)),
            _const_spec((1, d)),
            _const_spec((d, nqkv)),
            _const_spec((1, HEAD_DIM)),
            _const_spec((1, HEAD_DIM)),
            pl.BlockSpec((TM, HEAD_DIM), lambda i: (pos_block(i), 0)),
            pl.BlockSpec((TM, HEAD_DIM), lambda i: (pos_block(i), 0)),
        ],
        out_specs=(
            pl.BlockSpec((TM, nq), lambda i: (i, 0)),
            pl.BlockSpec((1, TM, nkv), kv_map),
            pl.BlockSpec((1, TM, nkv), kv_map),
        ),
        out_shape=(
            jax.ShapeDtypeStruct((t, nq), BF16),
            jax.ShapeDtypeStruct((n_samples, n_ctx + n_lat, nkv), BF16),
            jax.ShapeDtypeStruct((n_samples, n_ctx + n_lat, nkv), BF16),
        ),
        compiler_params=_cparams(("arbitrary",)),
        name="qkv_rope",
    )(x_all, mod, nm.reshape(1, d), w_qkv.astype(BF16), q_gain.reshape(1, HEAD_DIM),
      k_gain.reshape(1, HEAD_DIM), cos, sin)


def _attn_kernel(tiles_per_sample,
                 x_ref, q_ref, k_ref, v_ref, mod_ref, wo_ref, nf_ref, rwt_ref, rb_ref,
                 xo_ref, f_ref, idx_ref, gate_ref, rank_ref, cnt_ref, base_ref):
    b = pl.program_id(0)
    j = pl.program_id(1)
    d = x_ref.shape[1]
    n_heads = q_ref.shape[1] // HEAD_DIM
    group = n_heads // N_KV_HEADS
    _, _, g1 = _mod_slices(mod_ref, b, d, 0)
    sh2, sc2, _ = _mod_slices(mod_ref, b, d, 3)
    tq = q_ref.shape[0]
    exp2_scale = (HEAD_DIM ** -0.5) * 1.4426950408889634
    outs = []
    hpd = ATTN_HEADS_PER_DOT
    for h0 in range(0, n_heads, hpd):
        g = h0 // group
        qg = jnp.concatenate([q_ref[:, h * HEAD_DIM:(h + 1) * HEAD_DIM]
                              for h in range(h0, h0 + hpd)], axis=0)
        kg = k_ref[0, :, g * HEAD_DIM:(g + 1) * HEAD_DIM]
        vg = v_ref[0, :, g * HEAD_DIM:(g + 1) * HEAD_DIM]
        s = lax.dot_general(qg, kg, (((1,), (1,)), ((), ())), preferred_element_type=F32)
        p = jnp.exp2((s - jnp.max(s, axis=-1, keepdims=True)) * exp2_scale)
        den = jnp.sum(p, axis=-1, keepdims=True)
        pb = p.astype(BF16)
        for h in range(hpd):
            rows = slice(h * tq, (h + 1) * tq)
            o = jnp.dot(pb[rows, :], vg, preferred_element_type=F32) / den[rows, :]
            outs.append(o.astype(BF16))
    o_all = jnp.concatenate(outs, axis=1)
    y = jnp.dot(o_all, wo_ref[...], preferred_element_type=F32)
    x_new = x_ref[...] + g1 * y
    xo_ref[...] = x_new
    _router_epilogue(b * tiles_per_sample + j, x_new, sh2, sc2, nf_ref, rwt_ref, rb_ref,
                     f_ref, idx_ref, gate_ref, rank_ref, cnt_ref, base_ref)


def _attn_layer(x_all, q, k_all, v_all, mod, n_ctx_tiles, tiles_per_sample, n_samples,
                w_o, nf, rw, rb):
    d = x_all.shape[1]
    t_lat = n_samples * tiles_per_sample * TA
    n_exp = rw.shape[1]
    nq = q.shape[1]
    lk, nkv = k_all.shape[1], k_all.shape[2]
    tps = tiles_per_sample
    lat_tile = lambda b, j: n_ctx_tiles + b * tps + j
    tile = lambda b, j: b * tps + j
    return pl.pallas_call(
        functools.partial(_attn_kernel, tiles_per_sample),
        grid=(n_samples, tps),
        in_specs=[
            pl.BlockSpec((TA, d), lambda b, j: (lat_tile(b, j), 0)),
            pl.BlockSpec((TA, nq), lambda b, j: (lat_tile(b, j), 0)),
            pl.BlockSpec((1, lk, nkv), lambda b, j: (b, 0, 0)),
            pl.BlockSpec((1, lk, nkv), lambda b, j: (b, 0, 0)),
            _const_spec((MOD_ROWS, N_MOD * d)),
            _const_spec((nq, d)),
            _const_spec((1, d)),
            _const_spec((2 * n_exp, d)),
            _const_spec((n_exp, 1)),
        ],
        out_specs=(pl.BlockSpec((TA, d), lambda b, j: (tile(b, j), 0)),)
        + _router_out_specs(tile, TA, d, n_exp),
        out_shape=(jax.ShapeDtypeStruct((t_lat, d), F32),) + _router_out(t_lat, d, n_exp),
        scratch_shapes=[pltpu.VMEM((n_exp, TA), F32)],
        compiler_params=_cparams(("arbitrary", "arbitrary")),
        name="attn_mixer",
    )(x_all, q, k_all, v_all, mod, w_o.astype(BF16), nf.reshape(1, d), _router_weights(rw), rb.reshape(n_exp, 1))


def _dispatch_kernel(off_ref, tail_ref, used_ref, idx_ref, rank_ref, f_ref, xs_hbm, dest_ref,
                     zeros, sem, zsem):
    tm = idx_ref.shape[1]
    n_exp = tail_ref.shape[0]
    block_rows = zeros.shape[0]
    n_blocks = xs_hbm.shape[0] // block_rows

    @pl.when(pl.program_id(0) == 0)
    def _():
        zeros[...] = jnp.zeros_like(zeros)

        def fill(blk):
            start = pl.multiple_of(blk * block_rows, block_rows)
            return pltpu.make_async_copy(zeros, xs_hbm.at[pl.ds(start, block_rows)], zsem)

        def each(fn):
            def tail(e, c):
                @pl.when(tail_ref[e] >= 0)
                def _():
                    fn(fill(tail_ref[e]))
                return c

            def unused(j, c):
                @pl.when(used_ref[0] + j < n_blocks)
                def _():
                    fn(fill(used_ref[0] + j))
                return c

            lax.fori_loop(0, n_exp, tail, 0)
            lax.fori_loop(0, n_exp, unused, 0)

        each(lambda c: c.start())
        each(lambda c: c.wait())

    def start(t, c):
        src = f_ref.at[pl.ds(pl.multiple_of(t * SUBLANES, SUBLANES), SUBLANES)]
        for k in range(TOP_K):
            dst = off_ref[idx_ref[k, t]] + rank_ref[k, t]
            dest_ref[k, t] = dst
            row = xs_hbm.at[pl.ds(pl.multiple_of(dst * SUBLANES, SUBLANES), SUBLANES)]
            pltpu.make_async_copy(src, row, sem).start(priority=k % 2)
        return c

    lax.fori_loop(0, tm, start, 0)
    for k in range(TOP_K):
        pltpu.make_async_copy(f_ref, xs_hbm.at[pl.ds(0, tm * SUBLANES)], sem).wait()


def _dispatch(f, idx, rank, pad_start, tail_blk, used_blocks, n_blocks):
    t = idx.shape[1]
    smem_tile = pl.BlockSpec((TOP_K, TM), lambda i, *_: (0, i), memory_space=pltpu.SMEM)
    grid_spec = pltpu.PrefetchScalarGridSpec(
        num_scalar_prefetch=3,
        grid=(t // TM,),
        in_specs=[smem_tile, smem_tile, pl.BlockSpec((TM * SUBLANES, LANES), lambda i, *_: (i, 0))],
        out_specs=(pl.BlockSpec(memory_space=pl.ANY), smem_tile),
        scratch_shapes=[pltpu.VMEM((TE * SUBLANES, LANES), F32),
                        pltpu.SemaphoreType.DMA, pltpu.SemaphoreType.DMA],
    )
    return pl.pallas_call(
        _dispatch_kernel,
        grid_spec=grid_spec,
        out_shape=(jax.ShapeDtypeStruct((n_blocks * TE * SUBLANES, LANES), F32),
                   jax.ShapeDtypeStruct((TOP_K, t), jnp.int32)),
        compiler_params=_cparams(("arbitrary",)),
        name="moe_dispatch",
    )(pad_start, tail_blk, used_blocks, idx, rank, f)


def _expert_kernel(layer, exp_ref, used_ref, nxt_ref,
                   xs_hbm, wgu_hbm, bgu_ref, wd_hbm, bd_ref, yb_hbm,
                   xbuf, ybuf, gu_stage, d_stage, wgu_b, wd_b, wsem, xsem, ysem):
    w = pl.program_id(0)
    n_blocks = pl.num_programs(0)
    te = xbuf.shape[1]
    de = wd_b.shape[0]
    slot = w % 2
    new_expert = (w == 0) | (exp_ref[w] != exp_ref[jnp.maximum(w - 1, 0)])

    def x_copies(blk, sl):
        rows = pl.ds(pl.multiple_of(blk * te, te), te)
        return [pltpu.make_async_copy(xs_hbm.at[rows, s], xbuf.at[sl, :, pl.ds(s * LANES, LANES)],
                                      xsem.at[sl]) for s in range(SUBLANES)]

    def y_copies(blk, sl):
        rows = pl.ds(pl.multiple_of(blk * te, te), te)
        return [pltpu.make_async_copy(ybuf.at[sl, :, pl.ds(s * LANES, LANES)], yb_hbm.at[rows, s],
                                      ysem.at[sl]) for s in range(SUBLANES)]

    def weight_copies(e):
        return (pltpu.make_async_copy(wgu_hbm.at[layer, e], gu_stage, wsem.at[0]),
                pltpu.make_async_copy(wd_hbm.at[layer, e], d_stage, wsem.at[1]))

    @pl.when(w == 0)
    def _():
        for c in x_copies(0, 0):
            c.start()
        for c in weight_copies(exp_ref[0]):
            c.start()

    @pl.when(w + 1 < n_blocks)
    def _():
        for c in x_copies(w + 1, 1 - slot):
            c.start()

    @pl.when(new_expert)
    def _():
        for c in weight_copies(exp_ref[w]):
            c.wait()
        wgu_b[...] = gu_stage[...].astype(BF16)
        wd_b[...] = d_stage[...].astype(BF16)

        @pl.when(nxt_ref[w] >= 0)
        def _():
            for c in weight_copies(nxt_ref[w]):
                c.start()

    for c in x_copies(w, slot):
        c.wait()

    @pl.when(w >= 2)
    def _():
        for c in y_copies(w - 2, slot):
            c.wait()

    @pl.when(used_ref[w] == 0)
    def _():
        ybuf[slot] = jnp.zeros(ybuf.shape[1:], F32)

    @pl.when(used_ref[w] != 0)
    def _():
        x = xbuf[slot].astype(BF16)
        gu = jnp.dot(x, wgu_b[...], preferred_element_type=F32) + bgu_ref[0]
        g = jnp.minimum(gu[:, :de], SWIGLU_LIMIT)
        u = jnp.clip(gu[:, de:], -SWIGLU_LIMIT, SWIGLU_LIMIT)
        act = g * _sigmoid(SWIGLU_ALPHA * g) * (u + 1.0)
        ybuf[slot] = jnp.dot(act.astype(BF16), wd_b[...], preferred_element_type=F32) + bd_ref[0]

    for c in y_copies(w, slot):
        c.start()

    @pl.when(w == n_blocks - 1)
    def _():
        for c in y_copies(w, slot):
            c.wait()

        @pl.when(w >= 1)
        def _():
            for c in y_copies(w - 1, 1 - slot):
                c.wait()


def _experts(xs, work, layer, w_gu, b_gu, w_down, b_down):
    _, n_exp, d, de2 = w_gu.shape
    de = w_down.shape[2]
    n_blocks = work[0].shape[0]
    bias = lambda w, ex, used, nxt: (ex[w], 0, 0)
    grid_spec = pltpu.PrefetchScalarGridSpec(
        num_scalar_prefetch=3,
        grid=(n_blocks,),
        in_specs=[
            pl.BlockSpec(memory_space=pl.ANY),
            pl.BlockSpec(memory_space=pl.ANY),
            pl.BlockSpec((1, 1, de2), bias),
            pl.BlockSpec(memory_space=pl.ANY),
            pl.BlockSpec((1, 1, d), bias),
        ],
        out_specs=pl.BlockSpec(memory_space=pl.ANY),
        scratch_shapes=[pltpu.VMEM((2, TE, d), F32), pltpu.VMEM((2, TE, d), F32),
                        pltpu.VMEM((d, de2), F32), pltpu.VMEM((de, d), F32),
                        pltpu.VMEM((d, de2), BF16), pltpu.VMEM((de, d), BF16),
                        pltpu.SemaphoreType.DMA((2,)), pltpu.SemaphoreType.DMA((2,)),
                        pltpu.SemaphoreType.DMA((2,))],
    )
    return pl.pallas_call(
        functools.partial(_expert_kernel, layer),
        grid_spec=grid_spec,
        out_shape=jax.ShapeDtypeStruct(xs.shape, F32),
        compiler_params=_cparams(("arbitrary",)),
        name="moe_experts",
    )(*work, xs, w_gu, b_gu[layer].reshape(n_exp, 1, de2), w_down, b_down[layer].reshape(n_exp, 1, d))


def _combine_kernel(row_of_tile, dest_ref, next_ref, gate_ref, x_ref, mod_ref, yb_hbm, o_ref, buf, sem):
    i = pl.program_id(0)
    n_tiles = pl.num_programs(0)
    tm, d = x_ref.shape
    half = i % 2

    def fetch(idx_ref, h):
        def start(t, c):
            slot = pl.ds(pl.multiple_of(t * SUBLANES, SUBLANES), SUBLANES)
            for k in range(TOP_K):
                row = pl.ds(pl.multiple_of(idx_ref[k, t] * SUBLANES, SUBLANES), SUBLANES)
                pltpu.make_async_copy(yb_hbm.at[row], buf.at[h, k, slot], sem.at[h]).start(priority=k % 2)
            return c

        lax.fori_loop(0, tm, start, 0)

    @pl.when(i == 0)
    def _():
        fetch(dest_ref, 0)

    @pl.when(i + 1 < n_tiles)
    def _():
        fetch(next_ref, 1 - half)

    g2 = mod_ref[pl.ds(row_of_tile(i), 1), pl.ds(5 * d, d)]
    gates = jnp.concatenate([gate_ref[...], jnp.zeros((LANES - 8, tm), F32)], axis=0).T
    for k in range(TOP_K):
        pltpu.make_async_copy(yb_hbm.at[pl.ds(0, tm * SUBLANES)], buf.at[half, k], sem.at[half]).wait()
    y = _load_tile_rows(buf.at[half, 0], tm) * gates[:, 0:1]
    for k in range(1, TOP_K):
        y = y + _load_tile_rows(buf.at[half, k], tm) * gates[:, k:k + 1]
    o_ref[...] = x_ref[...] + g2 * y


def _combine(x_new, yb, dest, gates, mod, row_of_tile):
    t, d = x_new.shape
    last = t // TM - 1
    return pl.pallas_call(
        functools.partial(_combine_kernel, row_of_tile),
        grid=(t // TM,),
        in_specs=[
            pl.BlockSpec((TOP_K, TM), lambda i: (0, i), memory_space=pltpu.SMEM),
            pl.BlockSpec((TOP_K, TM), lambda i: (0, jnp.minimum(i + 1, last)), memory_space=pltpu.SMEM),
            pl.BlockSpec((8, TM), lambda i: (0, i)),
            pl.BlockSpec((TM, d), lambda i: (i, 0)),
            _const_spec((MOD_ROWS, N_MOD * d)),
            pl.BlockSpec(memory_space=pl.ANY),
        ],
        out_specs=pl.BlockSpec((TM, d), lambda i: (i, 0)),
        out_shape=jax.ShapeDtypeStruct((t, d), F32),
        scratch_shapes=[pltpu.VMEM((2, TOP_K, TM * SUBLANES, LANES), F32),
                        pltpu.SemaphoreType.DMA((2,))],
        compiler_params=_cparams(("arbitrary",)),
        name="moe_combine",
    )(dest, dest, gates, x_new, mod, yb)


def _expert_blocks(counts, n_assign):
    n_exp = counts.shape[0]
    n_blocks = n_assign // TE + n_exp
    padded = (counts + TE - 1) // TE * TE
    pad_end = jnp.cumsum(padded)
    pad_start = (pad_end - padded).astype(jnp.int32)
    used_blocks = pad_end[-1] // TE
    w = jnp.arange(n_blocks, dtype=jnp.int32)
    used = w < used_blocks
    ex = jnp.minimum(jnp.sum(w[:, None] * TE >= pad_end[None, :], axis=1), n_exp - 1).astype(jnp.int32)
    ex = jnp.where(used, ex, jnp.max(jnp.where(used, ex, 0)))
    ids = jnp.arange(n_exp, dtype=jnp.int32)
    later = jnp.where((ids[None, :] > ids[:, None]) & (counts[None, :] > 0), ids[None, :], n_exp)
    nxt_of = jnp.min(later, axis=1)
    nxt = jnp.sum(jnp.where(ex[:, None] == ids[None, :], nxt_of[None, :], 0), axis=1)
    nxt = jnp.where(nxt < n_exp, nxt, -1).astype(jnp.int32)
    tail_blk = jnp.where(counts > 0, pad_end // TE - 1, -1).astype(jnp.int32)
    work = (ex, used.astype(jnp.int32), nxt)
    return work, pad_start, tail_blk, used_blocks.astype(jnp.int32).reshape(1), n_blocks


def _moe(x_new, f, idx, gates, rank, cnt, mod, row_of_tile, layer, w_gu, b_gu, w_down, b_down):
    t = idx.shape[1]
    work, pad_start, tail_blk, used_blocks, n_blocks = _expert_blocks(cnt[:, 0], t * TOP_K)
    xs, dest = _dispatch(f, idx, rank, pad_start, tail_blk, used_blocks, n_blocks)
    yb = _experts(xs.reshape(-1, SUBLANES, LANES), work, layer, w_gu, b_gu, w_down, b_down)
    return _combine(x_new, yb.reshape(-1, LANES), dest, gates, mod, row_of_tile)


def kernel(x, c, ctx, c_ctx, ada_w, ada_b, norm_mix, norm_ffn, gm_w_in, gm_b_in, gm_v_gain, gm_w_s,
           gm_b_s, gm_w_out, at_w_qkv, at_q_gain, at_k_gain, at_w_o, moe_router_w, moe_router_b,
           moe_w_gu, moe_b_gu, moe_w_down, moe_b_down):
    n_samples, n_lat, d = x.shape
    n_ctx = ctx.shape[1]
    assert d == SUBLANES * LANES, "MoE row movement assumes one f32 tile per token row"
    assert n_lat % TM == 0 and n_ctx % TM == 0 and n_samples < MOD_ROWS
    assert n_lat % TG == 0 and (n_samples * n_ctx) % TG == 0 and TG % CHUNK == 0
    assert n_lat % TA == 0 and (n_samples * n_ctx) % TA == 0
    assert (n_samples * n_lat * TOP_K) % TE == 0 and (n_samples * n_ctx * TOP_K) % TE == 0
    n_ctx_tiles = n_samples * n_ctx // TM
    tps = n_lat // TM

    mods = _ada_table(c, c_ctx, ada_w, ada_b)

    def row_all(i):
        return jnp.where(i < n_ctx_tiles, n_samples, (i - n_ctx_tiles) // tps)

    x_new, f, idx, gates, rank, cnt = _gmlp_layer(
        ctx.reshape(-1, d), x.reshape(-1, d), mods[0], n_samples * n_ctx // TG, n_lat // TG, n_samples, norm_mix[0], gm_w_in[0], gm_b_in[0],
        gm_v_gain[0], gm_w_s[0], gm_b_s[0], gm_w_out[0], norm_ffn[0], moe_router_w[0],
        moe_router_b[0])
    x_all = _moe(x_new, f, idx, gates, rank, cnt, mods[0], row_all,
                 0, moe_w_gu, moe_b_gu, moe_w_down, moe_b_down)

    q, k_all, v_all = _qkv_layer(x_all, mods[1], n_ctx_tiles, tps, n_samples, n_ctx, n_lat,
                                 norm_mix[1], at_w_qkv[0], at_q_gain[0], at_k_gain[0])
    x_new, f, idx, gates, rank, cnt = _attn_layer(
        x_all, q, k_all, v_all, mods[1], n_samples * n_ctx // TA, n_lat // TA, n_samples, at_w_o[0], norm_ffn[1],
        moe_router_w[1], moe_router_b[1])
    out = _moe(x_new, f, idx, gates, rank, cnt, mods[1], lambda i: i // tps,
               1, moe_w_gu, moe_b_gu, moe_w_down, moe_b_down)
    return out.reshape(n_samples, n_lat, d)
```

```python
import functools

import jax
import jax.numpy as jnp
from jax import lax
from jax.experimental import pallas as pl
from jax.experimental.pallas import tpu as pltpu

F32 = jnp.float32
BF16 = jnp.bfloat16
HIGHEST = lax.Precision.HIGHEST

GRID_W = 64
N_MOD = 6
NORM_EPS = 1e-6
CHUNK = 128
GM_GROUPS = 8
HEAD_DIM = 128
N_KV_HEADS = 2
AXIS_DIM = HEAD_DIM // 2
ROPE_THETA = 10000.0
TOP_K = 4
SWIGLU_LIMIT = 7.0
SWIGLU_ALPHA = 1.702

TM = 256
TG = 512
TA = 512
TE = 512
ATTN_HEADS_PER_DOT = 2
MOD_ROWS = 16
V7X_VMEM_LIMIT = 56 * 1024 * 1024


def _cparams(sem, vmem=V7X_VMEM_LIMIT):
    return pltpu.CompilerParams(dimension_semantics=sem, vmem_limit_bytes=vmem)


def _const_spec(shape):
    nd = len(shape)
    return pl.BlockSpec(shape, lambda *_: (0,) * nd, pipeline_mode=pl.Buffered(1))


LANES = 128
SUBLANES = 8


def _load_tile_rows(ref, n):
    return jnp.concatenate([ref[pl.ds(s, n, stride=SUBLANES), :] for s in range(SUBLANES)], axis=1)


def _store_tile_rows(ref, val):
    n = val.shape[0]
    for s in range(SUBLANES):
        ref[pl.ds(s, n, stride=SUBLANES), :] = val[:, s * LANES:(s + 1) * LANES]


def _sigmoid(x):
    return 1.0 / (1.0 + jnp.exp(-x))


def _rms(x):
    return x * lax.rsqrt(jnp.mean(x * x, axis=-1, keepdims=True) + NORM_EPS)


def _ada_kernel(s_ref, w_ref, b_ref, o_ref):
    s = s_ref[...]
    s = s * _sigmoid(s)
    o_ref[0] = jnp.dot(s, w_ref[0], precision=HIGHEST, preferred_element_type=F32) + b_ref[0]


def _ada_table(c, c_ctx, ada_w, ada_b):
    depth, d, n = ada_w.shape
    b = c.shape[0]
    s = jnp.concatenate([c, c_ctx[None, :], jnp.zeros((MOD_ROWS - b - 1, d), F32)], axis=0)
    tn = 1536
    return pl.pallas_call(
        _ada_kernel,
        grid=(depth, n // tn),
        in_specs=[
            pl.BlockSpec((MOD_ROWS, d), lambda i, j: (0, 0)),
            pl.BlockSpec((1, d, tn), lambda i, j: (i, 0, j)),
            pl.BlockSpec((1, 1, tn), lambda i, j: (i, 0, j)),
        ],
        out_specs=pl.BlockSpec((1, MOD_ROWS, tn), lambda i, j: (i, 0, j)),
        out_shape=jax.ShapeDtypeStruct((depth, MOD_ROWS, n), F32),
        compiler_params=_cparams(("arbitrary", "arbitrary")),
        name="ada_table",
    )(s, ada_w, ada_b.reshape(depth, 1, n))


def _mod_slices(mod_ref, row, d, first):
    return [mod_ref[pl.ds(row, 1), pl.ds((first + k) * d, d)] for k in range(3)]


def _router_epilogue(step, x_new, sh2, sc2, nf_ref, rwt_ref, rb_ref,
                     f_ref, idx_ref, gate_ref, rank_ref, cnt_ref, base_ref):
    tm = x_new.shape[0]
    n_exp = rwt_ref.shape[0] // 2
    f = _rms(x_new) * nf_ref[...] * (1.0 + sc2) + sh2
    _store_tile_rows(f_ref, f)

    nt = (((1,), (1,)), ((), ()))
    f_hi = f.astype(BF16)
    f_lo = (f - f_hi.astype(F32)).astype(BF16)
    l_hi = lax.dot_general(rwt_ref[...], f_hi, nt, preferred_element_type=F32)
    l_lo = lax.dot_general(rwt_ref[:n_exp, :], f_lo, nt, preferred_element_type=F32)
    logits = l_hi[:n_exp, :] + l_hi[n_exp:, :] + l_lo + rb_ref[...]
    eid = lax.broadcasted_iota(jnp.int32, (n_exp, tm), 0).astype(F32)

    @pl.when(step == 0)
    def _():
        base_ref[...] = jnp.zeros_like(base_ref)

    r_io = lax.broadcasted_iota(jnp.int32, (tm, tm), 0)
    c_io = lax.broadcasted_iota(jnp.int32, (tm, tm), 1)
    before = jnp.where(r_io < c_io, 1.0, 0.0).astype(BF16)
    ones = jnp.ones((tm, tm), BF16)

    vals, idxs, hits = [], [], []
    l = logits
    for _ in range(TOP_K):
        m = jnp.max(l, axis=0, keepdims=True)
        sel = jnp.min(jnp.where(l == m, eid, float(n_exp)), axis=0, keepdims=True)
        hit = eid == sel
        l = jnp.where(hit, -jnp.inf, l)
        vals.append(m)
        idxs.append(sel)
        hits.append(hit)
    onehot = jnp.concatenate([jnp.where(h, 1.0, 0.0) for h in hits], axis=0).astype(BF16)
    prefix = jnp.dot(onehot, before, preferred_element_type=F32)
    count = jnp.dot(onehot, ones, preferred_element_type=F32)
    base = base_ref[...]
    ranks = []
    for k, hit in enumerate(hits):
        pk = prefix[k * n_exp:(k + 1) * n_exp, :]
        ranks.append(jnp.sum(jnp.where(hit, base + pk, 0.0), axis=0, keepdims=True))
        base = base + count[k * n_exp:(k + 1) * n_exp, :]
    base_ref[...] = base
    es = [jnp.exp(v - vals[0]) for v in vals]
    tot = es[0] + es[1] + es[2] + es[3]
    zero = jnp.zeros_like(tot)
    gate_ref[...] = jnp.concatenate([e / tot for e in es] + [zero] * (8 - TOP_K), axis=0)
    idx_ref[...] = jnp.concatenate(idxs, axis=0).astype(jnp.int32)
    rank_ref[...] = jnp.concatenate(ranks, axis=0).astype(jnp.int32)
    cnt_ref[...] = base_ref[:, :128].astype(jnp.int32)


def _router_weights(rw):
    hi = rw.T.astype(BF16)
    lo = (rw.T - hi.astype(F32)).astype(BF16)
    return jnp.concatenate([hi, lo], axis=0)


def _router_out(t, d, n_exp):
    shapes = (
        jax.ShapeDtypeStruct((t * SUBLANES, LANES), F32),
        jax.ShapeDtypeStruct((TOP_K, t), jnp.int32),
        jax.ShapeDtypeStruct((8, t), F32),
        jax.ShapeDtypeStruct((TOP_K, t), jnp.int32),
        jax.ShapeDtypeStruct((n_exp, 128), jnp.int32),
    )
    return shapes


def _router_out_specs(tile_of, tm, d, n_exp):
    return (
        pl.BlockSpec((tm * SUBLANES, LANES), lambda *g: (tile_of(*g), 0)),
        pl.BlockSpec((TOP_K, tm), lambda *g: (0, tile_of(*g))),
        pl.BlockSpec((8, tm), lambda *g: (0, tile_of(*g))),
        pl.BlockSpec((TOP_K, tm), lambda *g: (0, tile_of(*g))),
        pl.BlockSpec((n_exp, 128), lambda *g: (0, 0)),
    )


def _gmlp_kernel(n_ctx_tiles, tiles_per_sample, n_samples,
                 c_ref, x_ref, mod_ref, nm_ref, win_ref, bin_ref, vg_ref, ws_ref, bs_ref, wout_ref,
                 nf_ref, rwt_ref, rb_ref,
                 xo_ref, f_ref, idx_ref, gate_ref, rank_ref, cnt_ref, base_ref):
    i = pl.program_id(0)
    d = x_ref.shape[1]
    gw = wout_ref.shape[0]
    gc = gw // GM_GROUPS
    row = jnp.where(i < n_ctx_tiles, n_samples, (i - n_ctx_tiles) // tiles_per_sample)
    sh1, sc1, g1 = _mod_slices(mod_ref, row, d, 0)
    sh2, sc2, _ = _mod_slices(mod_ref, row, d, 3)

    x = jnp.where(i < n_ctx_tiles, c_ref[...], x_ref[...])
    h = _rms(x) * nm_ref[...] * (1.0 + sc1) + sh1
    hb = h.astype(BF16)

    def proj(col):
        a = jnp.dot(hb, win_ref[:, col:col + gc], preferred_element_type=F32) + bin_ref[:, col:col + gc]
        return 0.5 * a * (1.0 + lax.erf(a * (2.0 ** -0.5)))

    u = [proj(g * gc) for g in range(GM_GROUPS)]
    v = [proj(gw + g * gc) for g in range(GM_GROUPS)]
    ssq = v[0] * v[0]
    for g in range(1, GM_GROUPS):
        ssq = ssq + v[g] * v[g]
    inv = lax.rsqrt(jnp.sum(ssq, axis=-1, keepdims=True) * (1.0 / gw) + NORM_EPS)
    cols = []
    for g in range(GM_GROUPS):
        vn = (v[g] * inv * vg_ref[:, g * gc:(g + 1) * gc]).astype(BF16)
        s = jnp.concatenate(
            [jnp.dot(ws_ref[g], vn[c * CHUNK:(c + 1) * CHUNK, :], preferred_element_type=F32)
             + bs_ref[:, g * gc:(g + 1) * gc] for c in range(x.shape[0] // CHUNK)], axis=0)
        cols.append((u[g] * s).astype(BF16))
    z = jnp.concatenate(cols, axis=1)
    y = jnp.dot(z, wout_ref[...], preferred_element_type=F32)
    x_new = x + g1 * y
    xo_ref[...] = x_new
    _router_epilogue(i, x_new, sh2, sc2, nf_ref, rwt_ref, rb_ref,
                     f_ref, idx_ref, gate_ref, rank_ref, cnt_ref, base_ref)


def _gmlp_layer(ctx2d, x2d, mod, n_ctx_tiles, tiles_per_sample, n_samples,
                nm, w_in, b_in, v_gain, w_s, b_s, w_out, nf, rw, rb):
    d = x2d.shape[1]
    t = ctx2d.shape[0] + x2d.shape[0]
    gw = w_out.shape[0]
    n_exp = rw.shape[1]
    gc = gw // GM_GROUPS
    bs_full = jnp.repeat(b_s.T, gc, axis=1)
    tile = lambda i: i
    outs = pl.pallas_call(
        functools.partial(_gmlp_kernel, n_ctx_tiles, tiles_per_sample, n_samples),
        grid=(t // TG,),
        in_specs=[
            pl.BlockSpec((TG, d), lambda i: (jnp.minimum(i, n_ctx_tiles - 1), 0)),
            pl.BlockSpec((TG, d), lambda i: (jnp.maximum(i - n_ctx_tiles, 0), 0)),
            _const_spec((MOD_ROWS, N_MOD * d)),
            _const_spec((1, d)),
            _const_spec((d, 2 * gw)),
            _const_spec((1, 2 * gw)),
            _const_spec((1, gw)),
            _const_spec((GM_GROUPS, CHUNK, CHUNK)),
            _const_spec((CHUNK, gw)),
            _const_spec((gw, d)),
            _const_spec((1, d)),
            _const_spec((2 * n_exp, d)),
            _const_spec((n_exp, 1)),
        ],
        out_specs=(pl.BlockSpec((TG, d), lambda i: (i, 0)),) + _router_out_specs(tile, TG, d, n_exp),
        out_shape=(jax.ShapeDtypeStruct((t, d), F32),) + _router_out(t, d, n_exp),
        scratch_shapes=[pltpu.VMEM((n_exp, TG), F32)],
        compiler_params=_cparams(("arbitrary",)),
        name="gmlp_mixer",
    )(ctx2d, x2d, mod, nm.reshape(1, d), w_in.astype(BF16), b_in.reshape(1, -1), v_gain.reshape(1, gw),
      w_s.astype(BF16), bs_full, w_out.astype(BF16), nf.reshape(1, d), _router_weights(rw), rb.reshape(n_exp, 1))
    return outs


def _qkv_kernel(n_ctx_tiles, tiles_per_sample, n_samples,
                x_ref, mod_ref, nm_ref, w_ref, qg_ref, kg_ref, cos_ref, sin_ref,
                q_ref, k_ref, v_ref):
    i = pl.program_id(0)
    d = x_ref.shape[1]
    nq = q_ref.shape[1]
    nkv = k_ref.shape[2]
    row = jnp.where(i < n_ctx_tiles, n_samples, (i - n_ctx_tiles) // tiles_per_sample)
    sh1, sc1, _ = _mod_slices(mod_ref, row, d, 0)
    h = _rms(x_ref[...]) * nm_ref[...] * (1.0 + sc1) + sh1
    qkv = jnp.dot(h.astype(BF16), w_ref[...], preferred_element_type=F32)
    cos = cos_ref[...]
    sin = sin_ref[...]
    half = AXIS_DIM // 2
    lane = lax.broadcasted_iota(jnp.int32, (x_ref.shape[0], HEAD_DIM), 1)
    first_half = (lane % AXIS_DIM) < half

    def head(xh, gain):
        xh = _rms(xh) * gain
        partner = jnp.where(first_half, pltpu.roll(xh, HEAD_DIM - half, 1), pltpu.roll(xh, half, 1))
        return xh * cos + partner * sin

    q = [head(qkv[:, j * HEAD_DIM:(j + 1) * HEAD_DIM], qg_ref[...]) for j in range(nq // HEAD_DIM)]
    k = [head(qkv[:, nq + j * HEAD_DIM:nq + (j + 1) * HEAD_DIM], kg_ref[...])
         for j in range(nkv // HEAD_DIM)]
    q_ref[...] = jnp.concatenate(q, axis=1).astype(BF16)
    k_ref[0] = jnp.concatenate(k, axis=1).astype(BF16)
    v_ref[0] = qkv[:, nq + nkv:].astype(BF16)


def _rope_tables(n_lat):
    rows = n_lat // GRID_W
    row = jnp.repeat(jnp.arange(rows, dtype=jnp.int32), GRID_W).astype(F32)
    col = jnp.tile(jnp.arange(GRID_W, dtype=jnp.int32), rows).astype(F32)
    inv_freq = 1.0 / (ROPE_THETA ** (jnp.arange(0, AXIS_DIM, 2, dtype=F32) / AXIS_DIM))
    ang_r = row[:, None] * inv_freq
    ang_c = col[:, None] * inv_freq
    cos = jnp.concatenate([jnp.cos(ang_r)] * 2 + [jnp.cos(ang_c)] * 2, axis=1)
    sin = jnp.concatenate([-jnp.sin(ang_r), jnp.sin(ang_r), -jnp.sin(ang_c), jnp.sin(ang_c)], axis=1)
    cos = jnp.concatenate([jnp.ones((TM, HEAD_DIM), F32), cos], axis=0)
    sin = jnp.concatenate([jnp.zeros((TM, HEAD_DIM), F32), sin], axis=0)
    return cos, sin


def _qkv_layer(x_all, mod, n_ctx_tiles, tiles_per_sample, n_samples, n_ctx, n_lat,
               nm, w_qkv, q_gain, k_gain):
    t, d = x_all.shape
    nqkv = w_qkv.shape[1]
    nkv = N_KV_HEADS * HEAD_DIM
    nq = nqkv - 2 * nkv
    cos, sin = _rope_tables(n_lat)
    ctx_blocks = n_ctx // TM
    tps = tiles_per_sample

    def pos_block(i):
        return jnp.where(i < n_ctx_tiles, 0, 1 + (i - n_ctx_tiles) % tps)

    def kv_map(i):
        lat = i >= n_ctx_tiles
        b = jnp.where(lat, (i - n_ctx_tiles) // tps, i // ctx_blocks)
        j = jnp.where(lat, ctx_blocks + (i - n_ctx_tiles) % tps, i % ctx_blocks)
        return (b, j, 0)

    return pl.pallas_call(
        functools.partial(_qkv_kernel, n_ctx_tiles, tiles_per_sample, n_samples),
        grid=(t // TM,),
        in_specs=[
            pl.BlockSpec((TM, d), lambda i: (i, 0)),
            _const_spec((MOD_ROWS, N_MOD * d)),
            _const_spec((1, d)),
            _const_spec((d, nqkv)),
            _const_spec((1, HEAD_DIM)),
            _const_spec((1, HEAD_DIM)),
            pl.BlockSpec((TM, HEAD_DIM), lambda i: (pos_block(i), 0)),
            pl.BlockSpec((TM, HEAD_DIM), lambda i: (pos_block(i), 0)),
        ],
        out_specs=(
            pl.BlockSpec((TM, nq), lambda i: (i, 0)),
            pl.BlockSpec((1, TM, nkv), kv_map),
            pl.BlockSpec((1, TM, nkv), kv_map),
        ),
        out_shape=(
            jax.ShapeDtypeStruct((t, nq), BF16),
            jax.ShapeDtypeStruct((n_samples, n_ctx + n_lat, nkv), BF16),
            jax.ShapeDtypeStruct((n_samples, n_ctx + n_lat, nkv), BF16),
        ),
        compiler_params=_cparams(("arbitrary",)),
        name="qkv_rope",
    )(x_all, mod, nm.reshape(1, d), w_qkv.astype(BF16), q_gain.reshape(1, HEAD_DIM),
      k_gain.reshape(1, HEAD_DIM), cos, sin)


def _attn_kernel(tiles_per_sample,
                 x_ref, q_ref, k_ref, v_ref, mod_ref, wo_ref, nf_ref, rwt_ref, rb_ref,
                 xo_ref, f_ref, idx_ref, gate_ref, rank_ref, cnt_ref, base_ref):
    b = pl.program_id(0)
    j = pl.program_id(1)
    d = x_ref.shape[1]
    n_heads = q_ref.shape[1] // HEAD_DIM
    group = n_heads // N_KV_HEADS
    _, _, g1 = _mod_slices(mod_ref, b, d, 0)
    sh2, sc2, _ = _mod_slices(mod_ref, b, d, 3)
    tq = q_ref.shape[0]
    exp2_scale = (HEAD_DIM ** -0.5) * 1.4426950408889634
    outs = []
    hpd = ATTN_HEADS_PER_DOT
    for h0 in range(0, n_heads, hpd):
        g = h0 // group
        qg = jnp.concatenate([q_ref[:, h * HEAD_DIM:(h + 1) * HEAD_DIM]
                              for h in range(h0, h0 + hpd)], axis=0)
        kg = k_ref[0, :, g * HEAD_DIM:(g + 1) * HEAD_DIM]
        vg = v_ref[0, :, g * HEAD_DIM:(g + 1) * HEAD_DIM]
        s = lax.dot_general(qg, kg, (((1,), (1,)), ((), ())), preferred_element_type=F32)
        p = jnp.exp2((s - jnp.max(s, axis=-1, keepdims=True)) * exp2_scale)
        den = jnp.sum(p, axis=-1, keepdims=True)
        pb = p.astype(BF16)
        for h in range(hpd):
            rows = slice(h * tq, (h + 1) * tq)
            o = jnp.dot(pb[rows, :], vg, preferred_element_type=F32) / den[rows, :]
            outs.append(o.astype(BF16))
    o_all = jnp.concatenate(outs, axis=1)
    y = jnp.dot(o_all, wo_ref[...], preferred_element_type=F32)
    x_new = x_ref[...] + g1 * y
    xo_ref[...] = x_new
    _router_epilogue(b * tiles_per_sample + j, x_new, sh2, sc2, nf_ref, rwt_ref, rb_ref,
                     f_ref, idx_ref, gate_ref, rank_ref, cnt_ref, base_ref)


def _attn_layer(x_all, q, k_all, v_all, mod, n_ctx_tiles, tiles_per_sample, n_samples,
                w_o, nf, rw, rb):
    d = x_all.shape[1]
    t_lat = n_samples * tiles_per_sample * TA
    n_exp = rw.shape[1]
    nq = q.shape[1]
    lk, nkv = k_all.shape[1], k_all.shape[2]
    tps = tiles_per_sample
    lat_tile = lambda b, j: n_ctx_tiles + b * tps + j
    tile = lambda b, j: b * tps + j
    return pl.pallas_call(
        functools.partial(_attn_kernel, tiles_per_sample),
        grid=(n_samples, tps),
        in_specs=[
            pl.BlockSpec((TA, d), lambda b, j: (lat_tile(b, j), 0)),
            pl.BlockSpec((TA, nq), lambda b, j: (lat_tile(b, j), 0)),
            pl.BlockSpec((1, lk, nkv), lambda b, j: (b, 0, 0)),
            pl.BlockSpec((1, lk, nkv), lambda b, j: (b, 0, 0)),
            _const_spec((MOD_ROWS, N_MOD * d)),
            _const_spec((nq, d)),
            _const_spec((1, d)),
            _const_spec((2 * n_exp, d)),
            _const_spec((n_exp, 1)),
        ],
        out_specs=(pl.BlockSpec((TA, d), lambda b, j: (tile(b, j), 0)),)
        + _router_out_specs(tile, TA, d, n_exp),
        out_shape=(jax.ShapeDtypeStruct((t_lat, d), F32),) + _router_out(t_lat, d, n_exp),
        scratch_shapes=[pltpu.VMEM((n_exp, TA), F32)],
        compiler_params=_cparams(("arbitrary", "arbitrary")),
        name="attn_mixer",
    )(x_all, q, k_all, v_all, mod, w_o.astype(BF16), nf.reshape(1, d), _router_weights(rw), rb.reshape(n_exp, 1))


def _dispatch_kernel(tail_ref, used_ref, dest_ref, f_hbm, xs_hbm, zeros, fbuf, sem, zsem, fsem):
    i = pl.program_id(0)
    n_tiles = pl.num_programs(0)
    tm = dest_ref.shape[1]
    tile_rows = tm * SUBLANES
    n_exp = tail_ref.shape[0]
    block_rows = zeros.shape[0]
    n_blocks = xs_hbm.shape[0] // block_rows

    def load(tile):
        src = f_hbm.at[pl.ds(pl.multiple_of(tile * tile_rows, tile_rows), tile_rows)]
        return pltpu.make_async_copy(src, fbuf.at[tile % 3], fsem.at[tile % 3])

    def drain(tile):
        for k in range(TOP_K):
            pltpu.make_async_copy(fbuf.at[0], xs_hbm.at[pl.ds(0, tile_rows)], sem.at[tile % 2]).wait()

    @pl.when(i == 0)
    def _():
        load(0).start()
        zeros[...] = jnp.zeros_like(zeros)

        def fill(blk):
            start = pl.multiple_of(blk * block_rows, block_rows)
            return pltpu.make_async_copy(zeros, xs_hbm.at[pl.ds(start, block_rows)], zsem)

        def each(fn):
            def tail(e, c):
                @pl.when(tail_ref[e] >= 0)
                def _():
                    fn(fill(tail_ref[e]))
                return c

            def unused(j, c):
                @pl.when(used_ref[0] + j < n_blocks)
                def _():
                    fn(fill(used_ref[0] + j))
                return c

            lax.fori_loop(0, n_exp, tail, 0)
            lax.fori_loop(0, n_exp, unused, 0)

        each(lambda c: c.start())
        each(lambda c: c.wait())

    @pl.when(i + 1 < n_tiles)
    def _():
        load(i + 1).start()

    load(i).wait()
    tile = fbuf.at[i % 3]

    def start(t, c):
        src = tile.at[pl.ds(pl.multiple_of(t * SUBLANES, SUBLANES), SUBLANES)]
        for k in range(TOP_K):
            row = xs_hbm.at[pl.ds(pl.multiple_of(dest_ref[k, t] * SUBLANES, SUBLANES), SUBLANES)]
            pltpu.make_async_copy(src, row, sem.at[i % 2]).start(priority=k % 2)
        return c

    lax.fori_loop(0, tm, start, 0)

    @pl.when(i >= 1)
    def _():
        drain(i - 1)

    @pl.when(i == n_tiles - 1)
    def _():
        drain(i)


def _dispatch(f, dest, tail_blk, used_blocks, n_blocks):
    t = dest.shape[1]
    grid_spec = pltpu.PrefetchScalarGridSpec(
        num_scalar_prefetch=2,
        grid=(t // TM,),
        in_specs=[pl.BlockSpec((TOP_K, TM), lambda i, *_: (0, i), memory_space=pltpu.SMEM),
                  pl.BlockSpec(memory_space=pl.ANY)],
        out_specs=pl.BlockSpec(memory_space=pl.ANY),
        scratch_shapes=[pltpu.VMEM((TE * SUBLANES, LANES), F32),
                        pltpu.VMEM((3, TM * SUBLANES, LANES), F32),
                        pltpu.SemaphoreType.DMA((2,)), pltpu.SemaphoreType.DMA,
                        pltpu.SemaphoreType.DMA((3,))],
    )
    return pl.pallas_call(
        _dispatch_kernel,
        grid_spec=grid_spec,
        out_shape=jax.ShapeDtypeStruct((n_blocks * TE * SUBLANES, LANES), F32),
        compiler_params=_cparams(("arbitrary",)),
        name="moe_dispatch",
    )(tail_blk, used_blocks, dest, f)


def _expert_kernel(layer, exp_ref, used_ref, nxt_ref,
                   xs_hbm, wgu_hbm, bgu_ref, wd_hbm, bd_ref, yb_hbm,
                   xbuf, ybuf, gu_stage, d_stage, wgu_b, wd_b, wsem, xsem, ysem):
    w = pl.program_id(0)
    n_blocks = pl.num_programs(0)
    te = xbuf.shape[1]
    de = wd_b.shape[0]
    slot = w % 2
    new_expert = (w == 0) | (exp_ref[w] != exp_ref[jnp.maximum(w - 1, 0)])

    def x_copies(blk, sl):
        rows = pl.ds(pl.multiple_of(blk * te, te), te)
        return [pltpu.make_async_copy(xs_hbm.at[rows, s], xbuf.at[sl, :, pl.ds(s * LANES, LANES)],
                                      xsem.at[sl]) for s in range(SUBLANES)]

    def y_copies(blk, sl):
        rows = pl.ds(pl.multiple_of(blk * te, te), te)
        return [pltpu.make_async_copy(ybuf.at[sl, :, pl.ds(s * LANES, LANES)], yb_hbm.at[rows, s],
                                      ysem.at[sl]) for s in range(SUBLANES)]

    def weight_copies(e):
        return (pltpu.make_async_copy(wgu_hbm.at[layer, e], gu_stage, wsem.at[0]),
                pltpu.make_async_copy(wd_hbm.at[layer, e], d_stage, wsem.at[1]))

    @pl.when(w == 0)
    def _():
        for c in x_copies(0, 0):
            c.start()
        for c in weight_copies(exp_ref[0]):
            c.start()

    @pl.when(w + 1 < n_blocks)
    def _():
        for c in x_copies(w + 1, 1 - slot):
            c.start()

    @pl.when(new_expert)
    def _():
        for c in weight_copies(exp_ref[w]):
            c.wait()
        wgu_b[...] = gu_stage[...].astype(BF16)
        wd_b[...] = d_stage[...].astype(BF16)

        @pl.when(nxt_ref[w] >= 0)
        def _():
            for c in weight_copies(nxt_ref[w]):
                c.start()

    for c in x_copies(w, slot):
        c.wait()

    @pl.when(w >= 2)
    def _():
        for c in y_copies(w - 2, slot):
            c.wait()

    @pl.when(used_ref[w] == 0)
    def _():
        ybuf[slot] = jnp.zeros(ybuf.shape[1:], F32)

    @pl.when(used_ref[w] != 0)
    def _():
        x = xbuf[slot].astype(BF16)
        gu = jnp.dot(x, wgu_b[...], preferred_element_type=F32) + bgu_ref[0]
        g = jnp.minimum(gu[:, :de], SWIGLU_LIMIT)
        u = jnp.clip(gu[:, de:], -SWIGLU_LIMIT, SWIGLU_LIMIT)
        act = g * _sigmoid(SWIGLU_ALPHA * g) * (u + 1.0)
        ybuf[slot] = jnp.dot(act.astype(BF16), wd_b[...], preferred_element_type=F32) + bd_ref[0]

    for c in y_copies(w, slot):
        c.start()

    @pl.when(w == n_blocks - 1)
    def _():
        for c in y_copies(w, slot):
            c.wait()

        @pl.when(w >= 1)
        def _():
            for c in y_copies(w - 1, 1 - slot):
                c.wait()


def _experts(xs, work, layer, w_gu, b_gu, w_down, b_down):
    _, n_exp, d, de2 = w_gu.shape
    de = w_down.shape[2]
    n_blocks = work[0].shape[0]
    bias = lambda w, ex, used, nxt: (ex[w], 0, 0)
    grid_spec = pltpu.PrefetchScalarGridSpec(
        num_scalar_prefetch=3,
        grid=(n_blocks,),
        in_specs=[
            pl.BlockSpec(memory_space=pl.ANY),
            pl.BlockSpec(memory_space=pl.ANY),
            pl.BlockSpec((1, 1, de2), bias),
            pl.BlockSpec(memory_space=pl.ANY),
            pl.BlockSpec((1, 1, d), bias),
        ],
        out_specs=pl.BlockSpec(memory_space=pl.ANY),
        scratch_shapes=[pltpu.VMEM((2, TE, d), F32), pltpu.VMEM((2, TE, d), F32),
                        pltpu.VMEM((d, de2), F32), pltpu.VMEM((de, d), F32),
                        pltpu.VMEM((d, de2), BF16), pltpu.VMEM((de, d), BF16),
                        pltpu.SemaphoreType.DMA((2,)), pltpu.SemaphoreType.DMA((2,)),
                        pltpu.SemaphoreType.DMA((2,))],
    )
    return pl.pallas_call(
        functools.partial(_expert_kernel, layer),
        grid_spec=grid_spec,
        out_shape=jax.ShapeDtypeStruct(xs.shape, F32),
        compiler_params=_cparams(("arbitrary",)),
        name="moe_experts",
    )(*work, xs, w_gu, b_gu[layer].reshape(n_exp, 1, de2), w_down, b_down[layer].reshape(n_exp, 1, d))


def _combine_kernel(row_of_tile, dest_ref, next_ref, gate_ref, x_ref, mod_ref, yb_hbm, o_ref, buf, sem):
    i = pl.program_id(0)
    n_tiles = pl.num_programs(0)
    tm, d = x_ref.shape
    half = i % 2

    def fetch(idx_ref, h):
        def start(t, c):
            slot = pl.ds(pl.multiple_of(t * SUBLANES, SUBLANES), SUBLANES)
            for k in range(TOP_K):
                row = pl.ds(pl.multiple_of(idx_ref[k, t] * SUBLANES, SUBLANES), SUBLANES)
                pltpu.make_async_copy(yb_hbm.at[row], buf.at[h, k, slot], sem.at[h]).start(priority=k % 2)
            return c

        lax.fori_loop(0, tm, start, 0)

    @pl.when(i == 0)
    def _():
        fetch(dest_ref, 0)

    @pl.when(i + 1 < n_tiles)
    def _():
        fetch(next_ref, 1 - half)

    g2 = mod_ref[pl.ds(row_of_tile(i), 1), pl.ds(5 * d, d)]
    gates = jnp.concatenate([gate_ref[...], jnp.zeros((LANES - 8, tm), F32)], axis=0).T
    for k in range(TOP_K):
        pltpu.make_async_copy(yb_hbm.at[pl.ds(0, tm * SUBLANES)], buf.at[half, k], sem.at[half]).wait()
    y = _load_tile_rows(buf.at[half, 0], tm) * gates[:, 0:1]
    for k in range(1, TOP_K):
        y = y + _load_tile_rows(buf.at[half, k], tm) * gates[:, k:k + 1]
    o_ref[...] = x_ref[...] + g2 * y


def _combine(x_new, yb, dest, gates, mod, row_of_tile):
    t, d = x_new.shape
    last = t // TM - 1
    return pl.pallas_call(
        functools.partial(_combine_kernel, row_of_tile),
        grid=(t // TM,),
        in_specs=[
            pl.BlockSpec((TOP_K, TM), lambda i: (0, i), memory_space=pltpu.SMEM),
            pl.BlockSpec((TOP_K, TM), lambda i: (0, jnp.minimum(i + 1, last)), memory_space=pltpu.SMEM),
            pl.BlockSpec((8, TM), lambda i: (0, i)),
            pl.BlockSpec((TM, d), lambda i: (i, 0)),
            _const_spec((MOD_ROWS, N_MOD * d)),
            pl.BlockSpec(memory_space=pl.ANY),
        ],
        out_specs=pl.BlockSpec((TM, d), lambda i: (i, 0)),
        out_shape=jax.ShapeDtypeStruct((t, d), F32),
        scratch_shapes=[pltpu.VMEM((2, TOP_K, TM * SUBLANES, LANES), F32),
                        pltpu.SemaphoreType.DMA((2,))],
        compiler_params=_cparams(("arbitrary",)),
        name="moe_combine",
    )(dest, dest, gates, x_new, mod, yb)


def _expert_blocks(counts, n_assign):
    n_exp = counts.shape[0]
    n_blocks = n_assign // TE + n_exp
    padded = (counts + TE - 1) // TE * TE
    pad_end = jnp.cumsum(padded)
    pad_start = (pad_end - padded).astype(jnp.int32)
    used_blocks = pad_end[-1] // TE
    w = jnp.arange(n_blocks, dtype=jnp.int32)
    used = w < used_blocks
    ex = jnp.minimum(jnp.sum(w[:, None] * TE >= pad_end[None, :], axis=1), n_exp - 1).astype(jnp.int32)
    ex = jnp.where(used, ex, jnp.max(jnp.where(used, ex, 0)))
    ids = jnp.arange(n_exp, dtype=jnp.int32)
    later = jnp.where((ids[None, :] > ids[:, None]) & (counts[None, :] > 0), ids[None, :], n_exp)
    nxt_of = jnp.min(later, axis=1)
    nxt = jnp.sum(jnp.where(ex[:, None] == ids[None, :], nxt_of[None, :], 0), axis=1)
    nxt = jnp.where(nxt < n_exp, nxt, -1).astype(jnp.int32)
    tail_blk = jnp.where(counts > 0, pad_end // TE - 1, -1).astype(jnp.int32)
    work = (ex, used.astype(jnp.int32), nxt)
    return work, pad_start, tail_blk, used_blocks.astype(jnp.int32).reshape(1), n_blocks


def _moe(x_new, f, idx, gates, rank, cnt, mod, row_of_tile, layer, w_gu, b_gu, w_down, b_down):
    t = idx.shape[1]
    work, pad_start, tail_blk, used_blocks, n_blocks = _expert_blocks(cnt[:, 0], t * TOP_K)
    ids = jnp.arange(pad_start.shape[0], dtype=jnp.int32)
    dest = rank + jnp.sum(jnp.where(idx[..., None] == ids, pad_start, 0), axis=-1)
    xs = _dispatch(f, dest, tail_blk, used_blocks, n_blocks)
    yb = _experts(xs.reshape(-1, SUBLANES, LANES), work, layer, w_gu, b_gu, w_down, b_down)
    return _combine(x_new, yb.reshape(-1, LANES), dest, gates, mod, row_of_tile)


def kernel(x, c, ctx, c_ctx, ada_w, ada_b, norm_mix, norm_ffn, gm_w_in, gm_b_in, gm_v_gain, gm_w_s,
           gm_b_s, gm_w_out, at_w_qkv, at_q_gain, at_k_gain, at_w_o, moe_router_w, moe_router_b,
           moe_w_gu, moe_b_gu, moe_w_down, moe_b_down):
    n_samples, n_lat, d = x.shape
    n_ctx = ctx.shape[1]
    assert d == SUBLANES * LANES, "MoE row movement assumes one f32 tile per token row"
    assert n_lat % TM == 0 and n_ctx % TM == 0 and n_samples < MOD_ROWS
    assert n_lat % TG == 0 and (n_samples * n_ctx) % TG == 0 and TG % CHUNK == 0
    assert n_lat % TA == 0 and (n_samples * n_ctx) % TA == 0
    assert (n_samples * n_lat * TOP_K) % TE == 0 and (n_samples * n_ctx * TOP_K) % TE == 0
    n_ctx_tiles = n_samples * n_ctx // TM
    tps = n_lat // TM

    mods = _ada_table(c, c_ctx, ada_w, ada_b)

    def row_all(i):
        return jnp.where(i < n_ctx_tiles, n_samples, (i - n_ctx_tiles) // tps)

    x_new, f, idx, gates, rank, cnt = _gmlp_layer(
        ctx.reshape(-1, d), x.reshape(-1, d), mods[0], n_samples * n_ctx // TG, n_lat // TG, n_samples, norm_mix[0], gm_w_in[0], gm_b_in[0],
        gm_v_gain[0], gm_w_s[0], gm_b_s[0], gm_w_out[0], norm_ffn[0], moe_router_w[0],
        moe_router_b[0])
    x_all = _moe(x_new, f, idx, gates, rank, cnt, mods[0], row_all,
                 0, moe_w_gu, moe_b_gu, moe_w_down, moe_b_down)

    q, k_all, v_all = _qkv_layer(x_all, mods[1], n_ctx_tiles, tps, n_samples, n_ctx, n_lat,
                                 norm_mix[1], at_w_qkv[0], at_q_gain[0], at_k_gain[0])
    x_new, f, idx, gates, rank, cnt = _attn_layer(
        x_all, q, k_all, v_all, mods[1], n_samples * n_ctx // TA, n_lat // TA, n_samples, at_w_o[0], norm_ffn[1],
        moe_router_w[1], moe_router_b[1])
    out = _moe(x_new, f, idx, gates, rank, cnt, mods[1], lambda i: i // tps,
               1, moe_w_gu, moe_b_gu, moe_w_down, moe_b_down)
    return out.reshape(n_samples, n_lat, d)
```

```python
import functools

import jax
import jax.numpy as jnp
from jax import lax
from jax.experimental import pallas as pl
from jax.experimental.pallas import tpu as pltpu

F32 = jnp.float32
BF16 = jnp.bfloat16
HIGHEST = lax.Precision.HIGHEST

GRID_W = 64
N_MOD = 6
NORM_EPS = 1e-6
CHUNK = 128
GM_GROUPS = 8
HEAD_DIM = 128
N_KV_HEADS = 2
AXIS_DIM = HEAD_DIM // 2
ROPE_THETA = 10000.0
TOP_K = 4
SWIGLU_LIMIT = 7.0
SWIGLU_ALPHA = 1.702

TM = 256
TG = 512
TA = 512
TE = 512
ROW_DMA_UNROLL = 4
ATTN_HEADS_PER_DOT = 2
MOD_ROWS = 16
V7X_VMEM_LIMIT = 56 * 1024 * 1024


def _cparams(sem, vmem=V7X_VMEM_LIMIT):
    return pltpu.CompilerParams(dimension_semantics=sem, vmem_limit_bytes=vmem)


def _const_spec(shape):
    nd = len(shape)
    return pl.BlockSpec(shape, lambda *_: (0,) * nd, pipeline_mode=pl.Buffered(1))


LANES = 128
SUBLANES = 8


def _load_tile_rows(ref, n):
    return jnp.concatenate([ref[pl.ds(s, n, stride=SUBLANES), :] for s in range(SUBLANES)], axis=1)


def _store_tile_rows(ref, val):
    n = val.shape[0]
    for s in range(SUBLANES):
        ref[pl.ds(s, n, stride=SUBLANES), :] = val[:, s * LANES:(s + 1) * LANES]


def _sigmoid(x):
    return 1.0 / (1.0 + jnp.exp(-x))


def _rms(x):
    return x * lax.rsqrt(jnp.mean(x * x, axis=-1, keepdims=True) + NORM_EPS)


def _ada_kernel(s_ref, w_ref, b_ref, o_ref):
    s = s_ref[...]
    s = s * _sigmoid(s)
    o_ref[0] = jnp.dot(s, w_ref[0], precision=HIGHEST, preferred_element_type=F32) + b_ref[0]


def _ada_table(c, c_ctx, ada_w, ada_b):
    depth, d, n = ada_w.shape
    b = c.shape[0]
    s = jnp.concatenate([c, c_ctx[None, :], jnp.zeros((MOD_ROWS - b - 1, d), F32)], axis=0)
    tn = 1536
    return pl.pallas_call(
        _ada_kernel,
        grid=(depth, n // tn),
        in_specs=[
            pl.BlockSpec((MOD_ROWS, d), lambda i, j: (0, 0)),
            pl.BlockSpec((1, d, tn), lambda i, j: (i, 0, j)),
            pl.BlockSpec((1, 1, tn), lambda i, j: (i, 0, j)),
        ],
        out_specs=pl.BlockSpec((1, MOD_ROWS, tn), lambda i, j: (i, 0, j)),
        out_shape=jax.ShapeDtypeStruct((depth, MOD_ROWS, n), F32),
        compiler_params=_cparams(("arbitrary", "arbitrary")),
        name="ada_table",
    )(s, ada_w, ada_b.reshape(depth, 1, n))


def _mod_slices(mod_ref, row, d, first):
    return [mod_ref[pl.ds(row, 1), pl.ds((first + k) * d, d)] for k in range(3)]


def _router_epilogue(step, x_new, sh2, sc2, nf_ref, rwt_ref, rb_ref,
                     f_ref, idx_ref, gate_ref, rank_ref, cnt_ref, base_ref):
    tm = x_new.shape[0]
    n_exp = rwt_ref.shape[0] // 2
    f = _rms(x_new) * nf_ref[...] * (1.0 + sc2) + sh2
    _store_tile_rows(f_ref, f)

    nt = (((1,), (1,)), ((), ()))
    f_hi = f.astype(BF16)
    f_lo = (f - f_hi.astype(F32)).astype(BF16)
    l_hi = lax.dot_general(rwt_ref[...], f_hi, nt, preferred_element_type=F32)
    l_lo = lax.dot_general(rwt_ref[:n_exp, :], f_lo, nt, preferred_element_type=F32)
    logits = l_hi[:n_exp, :] + l_hi[n_exp:, :] + l_lo + rb_ref[...]
    eid = lax.broadcasted_iota(jnp.int32, (n_exp, tm), 0).astype(F32)

    @pl.when(step == 0)
    def _():
        base_ref[...] = jnp.zeros_like(base_ref)

    r_io = lax.broadcasted_iota(jnp.int32, (tm, tm), 0)
    c_io = lax.broadcasted_iota(jnp.int32, (tm, tm), 1)
    before = jnp.where(r_io < c_io, 1.0, 0.0).astype(BF16)
    ones = jnp.ones((tm, tm), BF16)

    vals, idxs, hits = [], [], []
    l = logits
    for _ in range(TOP_K):
        m = jnp.max(l, axis=0, keepdims=True)
        sel = jnp.min(jnp.where(l == m, eid, float(n_exp)), axis=0, keepdims=True)
        hit = eid == sel
        l = jnp.where(hit, -jnp.inf, l)
        vals.append(m)
        idxs.append(sel)
        hits.append(hit)
    onehot = jnp.concatenate([jnp.where(h, 1.0, 0.0) for h in hits], axis=0).astype(BF16)
    prefix = jnp.dot(onehot, before, preferred_element_type=F32)
    count = jnp.dot(onehot, ones, preferred_element_type=F32)
    base = base_ref[...]
    ranks = []
    for k, hit in enumerate(hits):
        pk = prefix[k * n_exp:(k + 1) * n_exp, :]
        ranks.append(jnp.sum(jnp.where(hit, base + pk, 0.0), axis=0, keepdims=True))
        base = base + count[k * n_exp:(k + 1) * n_exp, :]
    base_ref[...] = base
    es = [jnp.exp(v - vals[0]) for v in vals]
    tot = es[0] + es[1] + es[2] + es[3]
    zero = jnp.zeros_like(tot)
    gate_ref[...] = jnp.concatenate([e / tot for e in es] + [zero] * (8 - TOP_K), axis=0)
    idx_ref[...] = jnp.concatenate(idxs, axis=0).astype(jnp.int32)
    rank_ref[...] = jnp.concatenate(ranks, axis=0).astype(jnp.int32)
    cnt_ref[...] = base_ref[:, :128].astype(jnp.int32)


def _router_weights(rw):
    hi = rw.T.astype(BF16)
    lo = (rw.T - hi.astype(F32)).astype(BF16)
    return jnp.concatenate([hi, lo], axis=0)


def _router_out(t, d, n_exp):
    shapes = (
        jax.ShapeDtypeStruct((t * SUBLANES, LANES), F32),
        jax.ShapeDtypeStruct((TOP_K, t), jnp.int32),
        jax.ShapeDtypeStruct((8, t), F32),
        jax.ShapeDtypeStruct((TOP_K, t), jnp.int32),
        jax.ShapeDtypeStruct((n_exp, 128), jnp.int32),
    )
    return shapes


def _router_out_specs(tile_of, tm, d, n_exp):
    return (
        pl.BlockSpec((tm * SUBLANES, LANES), lambda *g: (tile_of(*g), 0)),
        pl.BlockSpec((TOP_K, tm), lambda *g: (0, tile_of(*g))),
        pl.BlockSpec((8, tm), lambda *g: (0, tile_of(*g))),
        pl.BlockSpec((TOP_K, tm), lambda *g: (0, tile_of(*g))),
        pl.BlockSpec((n_exp, 128), lambda *g: (0, 0)),
    )


def _gmlp_kernel(n_ctx_tiles, tiles_per_sample, n_samples,
                 c_ref, x_ref, mod_ref, nm_ref, win_ref, bin_ref, vg_ref, ws_ref, bs_ref, wout_ref,
                 nf_ref, rwt_ref, rb_ref,
                 xo_ref, f_ref, idx_ref, gate_ref, rank_ref, cnt_ref, base_ref):
    i = pl.program_id(0)
    d = x_ref.shape[1]
    gw = wout_ref.shape[0]
    gc = gw // GM_GROUPS
    row = jnp.where(i < n_ctx_tiles, n_samples, (i - n_ctx_tiles) // tiles_per_sample)
    sh1, sc1, g1 = _mod_slices(mod_ref, row, d, 0)
    sh2, sc2, _ = _mod_slices(mod_ref, row, d, 3)

    x = jnp.where(i < n_ctx_tiles, c_ref[...], x_ref[...])
    h = _rms(x) * nm_ref[...] * (1.0 + sc1) + sh1
    hb = h.astype(BF16)

    def proj(col):
        a = jnp.dot(hb, win_ref[:, col:col + gc], preferred_element_type=F32) + bin_ref[:, col:col + gc]
        return 0.5 * a * (1.0 + lax.erf(a * (2.0 ** -0.5)))

    u = [proj(g * gc) for g in range(GM_GROUPS)]
    v = [proj(gw + g * gc) for g in range(GM_GROUPS)]
    ssq = v[0] * v[0]
    for g in range(1, GM_GROUPS):
        ssq = ssq + v[g] * v[g]
    inv = lax.rsqrt(jnp.sum(ssq, axis=-1, keepdims=True) * (1.0 / gw) + NORM_EPS)
    cols = []
    for g in range(GM_GROUPS):
        vn = (v[g] * inv * vg_ref[:, g * gc:(g + 1) * gc]).astype(BF16)
        s = jnp.concatenate(
            [jnp.dot(ws_ref[g], vn[c * CHUNK:(c + 1) * CHUNK, :], preferred_element_type=F32)
             + bs_ref[:, g * gc:(g + 1) * gc] for c in range(x.shape[0] // CHUNK)], axis=0)
        cols.append((u[g] * s).astype(BF16))
    z = jnp.concatenate(cols, axis=1)
    y = jnp.dot(z, wout_ref[...], preferred_element_type=F32)
    x_new = x + g1 * y
    xo_ref[...] = x_new
    _router_epilogue(i, x_new, sh2, sc2, nf_ref, rwt_ref, rb_ref,
                     f_ref, idx_ref, gate_ref, rank_ref, cnt_ref, base_ref)


def _gmlp_layer(ctx2d, x2d, mod, n_ctx_tiles, tiles_per_sample, n_samples,
                nm, w_in, b_in, v_gain, w_s, b_s, w_out, nf, rw, rb):
    d = x2d.shape[1]
    t = ctx2d.shape[0] + x2d.shape[0]
    gw = w_out.shape[0]
    n_exp = rw.shape[1]
    gc = gw // GM_GROUPS
    bs_full = jnp.repeat(b_s.T, gc, axis=1)
    tile = lambda i: i
    outs = pl.pallas_call(
        functools.partial(_gmlp_kernel, n_ctx_tiles, tiles_per_sample, n_samples),
        grid=(t // TG,),
        in_specs=[
            pl.BlockSpec((TG, d), lambda i: (jnp.minimum(i, n_ctx_tiles - 1), 0)),
            pl.BlockSpec((TG, d), lambda i: (jnp.maximum(i - n_ctx_tiles, 0), 0)),
            _const_spec((MOD_ROWS, N_MOD * d)),
            _const_spec((1, d)),
            _const_spec((d, 2 * gw)),
            _const_spec((1, 2 * gw)),
            _const_spec((1, gw)),
            _const_spec((GM_GROUPS, CHUNK, CHUNK)),
            _const_spec((CHUNK, gw)),
            _const_spec((gw, d)),
            _const_spec((1, d)),
            _const_spec((2 * n_exp, d)),
            _const_spec((n_exp, 1)),
        ],
        out_specs=(pl.BlockSpec((TG, d), lambda i: (i, 0)),) + _router_out_specs(tile, TG, d, n_exp),
        out_shape=(jax.ShapeDtypeStruct((t, d), F32),) + _router_out(t, d, n_exp),
        scratch_shapes=[pltpu.VMEM((n_exp, TG), F32)],
        compiler_params=_cparams(("arbitrary",)),
        name="gmlp_mixer",
    )(ctx2d, x2d, mod, nm.reshape(1, d), w_in.astype(BF16), b_in.reshape(1, -1), v_gain.reshape(1, gw),
      w_s.astype(BF16), bs_full, w_out.astype(BF16), nf.reshape(1, d), _router_weights(rw), rb.reshape(n_exp, 1))
    return outs


def _qkv_kernel(n_ctx_tiles, tiles_per_sample, n_samples,
                x_ref, mod_ref, nm_ref, w_ref, qg_ref, kg_ref, cos_ref, sin_ref,
                q_ref, k_ref, v_ref):
    i = pl.program_id(0)
    d = x_ref.shape[1]
    nq = q_ref.shape[1]
    nkv = k_ref.shape[2]
    row = jnp.where(i < n_ctx_tiles, n_samples, (i - n_ctx_tiles) // tiles_per_sample)
    sh1, sc1, _ = _mod_slices(mod_ref, row, d, 0)
    h = _rms(x_ref[...]) * nm_ref[...] * (1.0 + sc1) + sh1
    qkv = jnp.dot(h.astype(BF16), w_ref[...], preferred_element_type=F32)
    cos = cos_ref[...]
    sin = sin_ref[...]
    half = AXIS_DIM // 2
    lane = lax.broadcasted_iota(jnp.int32, (x_ref.shape[0], HEAD_DIM), 1)
    first_half = (lane % AXIS_DIM) < half

    def head(xh, gain):
        xh = _rms(xh) * gain
        partner = jnp.where(first_half, pltpu.roll(xh, HEAD_DIM - half, 1), pltpu.roll(xh, half, 1))
        return xh * cos + partner * sin

    q = [head(qkv[:, j * HEAD_DIM:(j + 1) * HEAD_DIM], qg_ref[...]) for j in range(nq // HEAD_DIM)]
    k = [head(qkv[:, nq + j * HEAD_DIM:nq + (j + 1) * HEAD_DIM], kg_ref[...])
         for j in range(nkv // HEAD_DIM)]
    q_ref[...] = jnp.concatenate(q, axis=1).astype(BF16)
    k_ref[0] = jnp.concatenate(k, axis=1).astype(BF16)
    v_ref[0] = qkv[:, nq + nkv:].astype(BF16)


def _rope_tables(n_lat):
    rows = n_lat // GRID_W
    row = jnp.repeat(jnp.arange(rows, dtype=jnp.int32), GRID_W).astype(F32)
    col = jnp.tile(jnp.arange(GRID_W, dtype=jnp.int32), rows).astype(F32)
    inv_freq = 1.0 / (ROPE_THETA ** (jnp.arange(0, AXIS_DIM, 2, dtype=F32) / AXIS_DIM))
    ang_r = row[:, None] * inv_freq
    ang_c = col[:, None] * inv_freq
    cos = jnp.concatenate([jnp.cos(ang_r)] * 2 + [jnp.cos(ang_c)] * 2, axis=1)
    sin = jnp.concatenate([-jnp.sin(ang_r), jnp.sin(ang_r), -jnp.sin(ang_c), jnp.sin(ang_c)], axis=1)
    cos = jnp.concatenate([jnp.ones((TM, HEAD_DIM), F32), cos], axis=0)
    sin = jnp.concatenate([jnp.zeros((TM, HEAD_DIM), F32), sin], axis=0)
    return cos, sin


def _qkv_layer(x_all, mod, n_ctx_tiles, tiles_per_sample, n_samples, n_ctx, n_lat,
               nm, w_qkv, q_gain, k_gain):
    t, d = x_all.shape
    nqkv = w_qkv.shape[1]
    nkv = N_KV_HEADS * HEAD_DIM
    nq = nqkv - 2 * nkv
    cos, sin = _rope_tables(n_lat)
    ctx_blocks = n_ctx // TM
    tps = tiles_per_sample

    def pos_block(i):
        return jnp.where(i < n_ctx_tiles, 0, 1 + (i - n_ctx_tiles) % tps)

    def kv_map(i):
        lat = i >= n_ctx_tiles
        b = jnp.where(lat, (i - n_ctx_tiles) // tps, i // ctx_blocks)
        j = jnp.where(lat, ctx_blocks + (i - n_ctx_tiles) % tps, i % ctx_blocks)
        return (b, j, 0)

    return pl.pallas_call(
        functools.partial(_qkv_kernel, n_ctx_tiles, tiles_per_sample, n_samples),
        grid=(t // TM,),
        in_specs=[
            pl.BlockSpec((TM, d), lambda i: (i, 0)),
            _const_spec((MOD_ROWS, N_MOD * d)),
            _const_spec((1, d)),
            _const_spec((d, nqkv)),
            _const_spec((1, HEAD_DIM)),
            _const_spec((1, HEAD_DIM)),
            pl.BlockSpec((TM, HEAD_DIM), lambda i: (pos_block(i), 0)),
            pl.BlockSpec((TM, HEAD_DIM), lambda i: (pos_block(i), 0)),
        ],
        out_specs=(
            pl.BlockSpec((TM, nq), lambda i: (i, 0)),
            pl.BlockSpec((1, TM, nkv), kv_map),
            pl.BlockSpec((1, TM, nkv), kv_map),
        ),
        out_shape=(
            jax.ShapeDtypeStruct((t, nq), BF16),
            jax.ShapeDtypeStruct((n_samples, n_ctx + n_lat, nkv), BF16),
            jax.ShapeDtypeStruct((n_samples, n_ctx + n_lat, nkv), BF16),
        ),
        compiler_params=_cparams(("arbitrary",)),
        name="qkv_rope",
    )(x_all, mod, nm.reshape(1, d), w_qkv.astype(BF16), q_gain.reshape(1, HEAD_DIM),
      k_gain.reshape(1, HEAD_DIM), cos, sin)


def _attn_kernel(tiles_per_sample,
                 x_ref, q_ref, k_ref, v_ref, mod_ref, wo_ref, nf_ref, rwt_ref, rb_ref,
                 xo_ref, f_ref, idx_ref, gate_ref, rank_ref, cnt_ref, base_ref):
    b = pl.program_id(0)
    j = pl.program_id(1)
    d = x_ref.shape[1]
    n_heads = q_ref.shape[1] // HEAD_DIM
    group = n_heads // N_KV_HEADS
    _, _, g1 = _mod_slices(mod_ref, b, d, 0)
    sh2, sc2, _ = _mod_slices(mod_ref, b, d, 3)
    tq = q_ref.shape[0]
    exp2_scale = (HEAD_DIM ** -0.5) * 1.4426950408889634
    outs = []
    hpd = ATTN_HEADS_PER_DOT
    for h0 in range(0, n_heads, hpd):
        g = h0 // group
        qg = jnp.concatenate([q_ref[:, h * HEAD_DIM:(h + 1) * HEAD_DIM]
                              for h in range(h0, h0 + hpd)], axis=0)
        kg = k_ref[0, :, g * HEAD_DIM:(g + 1) * HEAD_DIM]
        vg = v_ref[0, :, g * HEAD_DIM:(g + 1) * HEAD_DIM]
        s = lax.dot_general(qg, kg, (((1,), (1,)), ((), ())), preferred_element_type=F32)
        p = jnp.exp2((s - jnp.max(s, axis=-1, keepdims=True)) * exp2_scale)
        den = jnp.sum(p, axis=-1, keepdims=True)
        pb = p.astype(BF16)
        for h in range(hpd):
            rows = slice(h * tq, (h + 1) * tq)
            o = jnp.dot(pb[rows, :], vg, preferred_element_type=F32) / den[rows, :]
            outs.append(o.astype(BF16))
    o_all = jnp.concatenate(outs, axis=1)
    y = jnp.dot(o_all, wo_ref[...], preferred_element_type=F32)
    x_new = x_ref[...] + g1 * y
    xo_ref[...] = x_new
    _router_epilogue(b * tiles_per_sample + j, x_new, sh2, sc2, nf_ref, rwt_ref, rb_ref,
                     f_ref, idx_ref, gate_ref, rank_ref, cnt_ref, base_ref)


def _attn_layer(x_all, q, k_all, v_all, mod, n_ctx_tiles, tiles_per_sample, n_samples,
                w_o, nf, rw, rb):
    d = x_all.shape[1]
    t_lat = n_samples * tiles_per_sample * TA
    n_exp = rw.shape[1]
    nq = q.shape[1]
    lk, nkv = k_all.shape[1], k_all.shape[2]
    tps = tiles_per_sample
    lat_tile = lambda b, j: n_ctx_tiles + b * tps + j
    tile = lambda b, j: b * tps + j
    return pl.pallas_call(
        functools.partial(_attn_kernel, tiles_per_sample),
        grid=(n_samples, tps),
        in_specs=[
            pl.BlockSpec((TA, d), lambda b, j: (lat_tile(b, j), 0)),
            pl.BlockSpec((TA, nq), lambda b, j: (lat_tile(b, j), 0)),
            pl.BlockSpec((1, lk, nkv), lambda b, j: (b, 0, 0)),
            pl.BlockSpec((1, lk, nkv), lambda b, j: (b, 0, 0)),
            _const_spec((MOD_ROWS, N_MOD * d)),
            _const_spec((nq, d)),
            _const_spec((1, d)),
            _const_spec((2 * n_exp, d)),
            _const_spec((n_exp, 1)),
        ],
        out_specs=(pl.BlockSpec((TA, d), lambda b, j: (tile(b, j), 0)),)
        + _router_out_specs(tile, TA, d, n_exp),
        out_shape=(jax.ShapeDtypeStruct((t_lat, d), F32),) + _router_out(t_lat, d, n_exp),
        scratch_shapes=[pltpu.VMEM((n_exp, TA), F32)],
        compiler_params=_cparams(("arbitrary", "arbitrary")),
        name="attn_mixer",
    )(x_all, q, k_all, v_all, mod, w_o.astype(BF16), nf.reshape(1, d), _router_weights(rw), rb.reshape(n_exp, 1))


def _dispatch_kernel(tail_ref, used_ref, dest_ref, f_hbm, xs_hbm, zeros, fbuf, sem, zsem, fsem):
    i = pl.program_id(0)
    n_tiles = pl.num_programs(0)
    tm = dest_ref.shape[0] // TOP_K
    tile_rows = tm * SUBLANES
    n_exp = tail_ref.shape[0]
    block_rows = zeros.shape[0]
    n_blocks = xs_hbm.shape[0] // block_rows

    def load(tile):
        src = f_hbm.at[pl.ds(pl.multiple_of(tile * tile_rows, tile_rows), tile_rows)]
        return pltpu.make_async_copy(src, fbuf.at[tile % 3], fsem.at[tile % 3])

    def drain(tile):
        for k in range(TOP_K):
            pltpu.make_async_copy(fbuf.at[0], xs_hbm.at[pl.ds(0, tile_rows)], sem.at[tile % 2]).wait()

    @pl.when(i == 0)
    def _():
        load(0).start()
        zeros[...] = jnp.zeros_like(zeros)

        def fill(blk):
            start = pl.multiple_of(blk * block_rows, block_rows)
            return pltpu.make_async_copy(zeros, xs_hbm.at[pl.ds(start, block_rows)], zsem)

        def each(fn):
            def tail(e, c):
                @pl.when(tail_ref[e] >= 0)
                def _():
                    fn(fill(tail_ref[e]))
                return c

            def unused(j, c):
                @pl.when(used_ref[0] + j < n_blocks)
                def _():
                    fn(fill(used_ref[0] + j))
                return c

            lax.fori_loop(0, n_exp, tail, 0)
            lax.fori_loop(0, n_exp, unused, 0)

        each(lambda c: c.start())
        each(lambda c: c.wait())

    @pl.when(i + 1 < n_tiles)
    def _():
        load(i + 1).start()

    load(i).wait()
    tile = fbuf.at[i % 3]

    def start(t, c):
        src = tile.at[pl.ds(pl.multiple_of(t * SUBLANES, SUBLANES), SUBLANES)]
        for k in range(TOP_K):
            row = xs_hbm.at[pl.ds(pl.multiple_of(dest_ref[t * TOP_K + k] * SUBLANES, SUBLANES), SUBLANES)]
            pltpu.make_async_copy(src, row, sem.at[i % 2]).start(priority=k % 2)
        return c

    lax.fori_loop(0, tm, start, 0, unroll=ROW_DMA_UNROLL)

    @pl.when(i >= 1)
    def _():
        drain(i - 1)

    @pl.when(i == n_tiles - 1)
    def _():
        drain(i)


def _dispatch(f, dest, tail_blk, used_blocks, n_blocks):
    t = dest.shape[0] // TOP_K
    grid_spec = pltpu.PrefetchScalarGridSpec(
        num_scalar_prefetch=2,
        grid=(t // TM,),
        in_specs=[pl.BlockSpec((TM * TOP_K,), lambda i, *_: (i,), memory_space=pltpu.SMEM),
                  pl.BlockSpec(memory_space=pl.ANY)],
        out_specs=pl.BlockSpec(memory_space=pl.ANY),
        scratch_shapes=[pltpu.VMEM((TE * SUBLANES, LANES), F32),
                        pltpu.VMEM((3, TM * SUBLANES, LANES), F32),
                        pltpu.SemaphoreType.DMA((2,)), pltpu.SemaphoreType.DMA,
                        pltpu.SemaphoreType.DMA((3,))],
    )
    return pl.pallas_call(
        _dispatch_kernel,
        grid_spec=grid_spec,
        out_shape=jax.ShapeDtypeStruct((n_blocks * TE * SUBLANES, LANES), F32),
        compiler_params=_cparams(("arbitrary",)),
        name="moe_dispatch",
    )(tail_blk, used_blocks, dest, f)


def _expert_kernel(layer, exp_ref, used_ref, nxt_ref,
                   xs_hbm, wgu_hbm, bgu_ref, wd_hbm, bd_ref, yb_hbm,
                   xbuf, ybuf, gu_stage, d_stage, wgu_b, wd_b, wsem, xsem, ysem):
    w = pl.program_id(0)
    n_blocks = pl.num_programs(0)
    te = xbuf.shape[1]
    de = wd_b.shape[0]
    slot = w % 2
    new_expert = (w == 0) | (exp_ref[w] != exp_ref[jnp.maximum(w - 1, 0)])

    def x_copies(blk, sl):
        rows = pl.ds(pl.multiple_of(blk * te, te), te)
        return [pltpu.make_async_copy(xs_hbm.at[rows, s], xbuf.at[sl, :, pl.ds(s * LANES, LANES)],
                                      xsem.at[sl]) for s in range(SUBLANES)]

    def y_copies(blk, sl):
        rows = pl.ds(pl.multiple_of(blk * te, te), te)
        return [pltpu.make_async_copy(ybuf.at[sl, :, pl.ds(s * LANES, LANES)], yb_hbm.at[rows, s],
                                      ysem.at[sl]) for s in range(SUBLANES)]

    def weight_copies(e):
        return (pltpu.make_async_copy(wgu_hbm.at[layer, e], gu_stage, wsem.at[0]),
                pltpu.make_async_copy(wd_hbm.at[layer, e], d_stage, wsem.at[1]))

    @pl.when(w == 0)
    def _():
        for c in x_copies(0, 0):
            c.start()
        for c in weight_copies(exp_ref[0]):
            c.start()

    @pl.when(w + 1 < n_blocks)
    def _():
        for c in x_copies(w + 1, 1 - slot):
            c.start()

    @pl.when(new_expert)
    def _():
        for c in weight_copies(exp_ref[w]):
            c.wait()
        wgu_b[...] = gu_stage[...].astype(BF16)
        wd_b[...] = d_stage[...].astype(BF16)

        @pl.when(nxt_ref[w] >= 0)
        def _():
            for c in weight_copies(nxt_ref[w]):
                c.start()

    for c in x_copies(w, slot):
        c.wait()

    @pl.when(w >= 2)
    def _():
        for c in y_copies(w - 2, slot):
            c.wait()

    @pl.when(used_ref[w] == 0)
    def _():
        ybuf[slot] = jnp.zeros(ybuf.shape[1:], F32)

    @pl.when(used_ref[w] != 0)
    def _():
        x = xbuf[slot].astype(BF16)
        gu = jnp.dot(x, wgu_b[...], preferred_element_type=F32) + bgu_ref[0]
        g = jnp.minimum(gu[:, :de], SWIGLU_LIMIT)
        u = jnp.clip(gu[:, de:], -SWIGLU_LIMIT, SWIGLU_LIMIT)
        act = g * _sigmoid(SWIGLU_ALPHA * g) * (u + 1.0)
        ybuf[slot] = jnp.dot(act.astype(BF16), wd_b[...], preferred_element_type=F32) + bd_ref[0]

    for c in y_copies(w, slot):
        c.start()

    @pl.when(w == n_blocks - 1)
    def _():
        for c in y_copies(w, slot):
            c.wait()

        @pl.when(w >= 1)
        def _():
            for c in y_copies(w - 1, 1 - slot):
                c.wait()


def _experts(xs, work, layer, w_gu, b_gu, w_down, b_down):
    _, n_exp, d, de2 = w_gu.shape
    de = w_down.shape[2]
    n_blocks = work[0].shape[0]
    bias = lambda w, ex, used, nxt: (ex[w], 0, 0)
    grid_spec = pltpu.PrefetchScalarGridSpec(
        num_scalar_prefetch=3,
        grid=(n_blocks,),
        in_specs=[
            pl.BlockSpec(memory_space=pl.ANY),
            pl.BlockSpec(memory_space=pl.ANY),
            pl.BlockSpec((1, 1, de2), bias),
            pl.BlockSpec(memory_space=pl.ANY),
            pl.BlockSpec((1, 1, d), bias),
        ],
        out_specs=pl.BlockSpec(memory_space=pl.ANY),
        scratch_shapes=[pltpu.VMEM((2, TE, d), F32), pltpu.VMEM((2, TE, d), F32),
                        pltpu.VMEM((d, de2), F32), pltpu.VMEM((de, d), F32),
                        pltpu.VMEM((d, de2), BF16), pltpu.VMEM((de, d), BF16),
                        pltpu.SemaphoreType.DMA((2,)), pltpu.SemaphoreType.DMA((2,)),
                        pltpu.SemaphoreType.DMA((2,))],
    )
    return pl.pallas_call(
        functools.partial(_expert_kernel, layer),
        grid_spec=grid_spec,
        out_shape=jax.ShapeDtypeStruct(xs.shape, F32),
        compiler_params=_cparams(("arbitrary",)),
        name="moe_experts",
    )(*work, xs, w_gu, b_gu[layer].reshape(n_exp, 1, de2), w_down, b_down[layer].reshape(n_exp, 1, d))


def _combine_kernel(row_of_tile, dest_ref, next_ref, gate_ref, x_ref, mod_ref, yb_hbm, o_ref, buf, sem):
    i = pl.program_id(0)
    n_tiles = pl.num_programs(0)
    tm, d = x_ref.shape
    half = i % 2

    def fetch(idx_ref, h):
        def start(t, c):
            slot = pl.ds(pl.multiple_of(t * SUBLANES, SUBLANES), SUBLANES)
            for k in range(TOP_K):
                row = pl.ds(pl.multiple_of(idx_ref[t * TOP_K + k] * SUBLANES, SUBLANES), SUBLANES)
                pltpu.make_async_copy(yb_hbm.at[row], buf.at[h, k, slot], sem.at[h]).start(priority=k % 2)
            return c

        lax.fori_loop(0, tm, start, 0, unroll=ROW_DMA_UNROLL)

    @pl.when(i == 0)
    def _():
        fetch(dest_ref, 0)

    @pl.when(i + 1 < n_tiles)
    def _():
        fetch(next_ref, 1 - half)

    g2 = mod_ref[pl.ds(row_of_tile(i), 1), pl.ds(5 * d, d)]
    gates = jnp.concatenate([gate_ref[...], jnp.zeros((LANES - 8, tm), F32)], axis=0).T
    for k in range(TOP_K):
        pltpu.make_async_copy(yb_hbm.at[pl.ds(0, tm * SUBLANES)], buf.at[half, k], sem.at[half]).wait()
    y = _load_tile_rows(buf.at[half, 0], tm) * gates[:, 0:1]
    for k in range(1, TOP_K):
        y = y + _load_tile_rows(buf.at[half, k], tm) * gates[:, k:k + 1]
    o_ref[...] = x_ref[...] + g2 * y


def _combine(x_new, yb, dest, gates, mod, row_of_tile):
    t, d = x_new.shape
    last = t // TM - 1
    return pl.pallas_call(
        functools.partial(_combine_kernel, row_of_tile),
        grid=(t // TM,),
        in_specs=[
            pl.BlockSpec((TM * TOP_K,), lambda i: (i,), memory_space=pltpu.SMEM),
            pl.BlockSpec((TM * TOP_K,), lambda i: (jnp.minimum(i + 1, last),), memory_space=pltpu.SMEM),
            pl.BlockSpec((8, TM), lambda i: (0, i)),
            pl.BlockSpec((TM, d), lambda i: (i, 0)),
            _const_spec((MOD_ROWS, N_MOD * d)),
            pl.BlockSpec(memory_space=pl.ANY),
        ],
        out_specs=pl.BlockSpec((TM, d), lambda i: (i, 0)),
        out_shape=jax.ShapeDtypeStruct((t, d), F32),
        scratch_shapes=[pltpu.VMEM((2, TOP_K, TM * SUBLANES, LANES), F32),
                        pltpu.SemaphoreType.DMA((2,))],
        compiler_params=_cparams(("arbitrary",)),
        name="moe_combine",
    )(dest, dest, gates, x_new, mod, yb)


def _expert_blocks(counts, n_assign):
    n_exp = counts.shape[0]
    n_blocks = n_assign // TE + n_exp
    padded = (counts + TE - 1) // TE * TE
    pad_end = jnp.cumsum(padded)
    pad_start = (pad_end - padded).astype(jnp.int32)
    used_blocks = pad_end[-1] // TE
    w = jnp.arange(n_blocks, dtype=jnp.int32)
    used = w < used_blocks
    ex = jnp.minimum(jnp.sum(w[:, None] * TE >= pad_end[None, :], axis=1), n_exp - 1).astype(jnp.int32)
    ex = jnp.where(used, ex, jnp.max(jnp.where(used, ex, 0)))
    ids = jnp.arange(n_exp, dtype=jnp.int32)
    later = jnp.where((ids[None, :] > ids[:, None]) & (counts[None, :] > 0), ids[None, :], n_exp)
    nxt_of = jnp.min(later, axis=1)
    nxt = jnp.sum(jnp.where(ex[:, None] == ids[None, :], nxt_of[None, :], 0), axis=1)
    nxt = jnp.where(nxt < n_exp, nxt, -1).astype(jnp.int32)
    tail_blk = jnp.where(counts > 0, pad_end // TE - 1, -1).astype(jnp.int32)
    work = (ex, used.astype(jnp.int32), nxt)
    return work, pad_start, tail_blk, used_blocks.astype(jnp.int32).reshape(1), n_blocks


def _moe(x_new, f, idx, gates, rank, cnt, mod, row_of_tile, layer, w_gu, b_gu, w_down, b_down):
    t = idx.shape[1]
    work, pad_start, tail_blk, used_blocks, n_blocks = _expert_blocks(cnt[:, 0], t * TOP_K)
    ids = jnp.arange(pad_start.shape[0], dtype=jnp.int32)
    dest = rank + jnp.sum(jnp.where(idx[..., None] == ids, pad_start, 0), axis=-1)
    dest = dest.T.reshape(-1)
    xs = _dispatch(f, dest, tail_blk, used_blocks, n_blocks)
    yb = _experts(xs.reshape(-1, SUBLANES, LANES), work, layer, w_gu, b_gu, w_down, b_down)
    return _combine(x_new, yb.reshape(-1, LANES), dest, gates, mod, row_of_tile)


def kernel(x, c, ctx, c_ctx, ada_w, ada_b, norm_mix, norm_ffn, gm_w_in, gm_b_in, gm_v_gain, gm_w_s,
           gm_b_s, gm_w_out, at_w_qkv, at_q_gain, at_k_gain, at_w_o, moe_router_w, moe_router_b,
           moe_w_gu, moe_b_gu, moe_w_down, moe_b_down):
    n_samples, n_lat, d = x.shape
    n_ctx = ctx.shape[1]
    assert d == SUBLANES * LANES, "MoE row movement assumes one f32 tile per token row"
    assert n_lat % TM == 0 and n_ctx % TM == 0 and n_samples < MOD_ROWS
    assert n_lat % TG == 0 and (n_samples * n_ctx) % TG == 0 and TG % CHUNK == 0
    assert n_lat % TA == 0 and (n_samples * n_ctx) % TA == 0
    assert (n_samples * n_lat * TOP_K) % TE == 0 and (n_samples * n_ctx * TOP_K) % TE == 0
    n_ctx_tiles = n_samples * n_ctx // TM
    tps = n_lat // TM

    mods = _ada_table(c, c_ctx, ada_w, ada_b)

    def row_all(i):
        return jnp.where(i < n_ctx_tiles, n_samples, (i - n_ctx_tiles) // tps)

    x_new, f, idx, gates, rank, cnt = _gmlp_layer(
        ctx.reshape(-1, d), x.reshape(-1, d), mods[0], n_samples * n_ctx // TG, n_lat // TG, n_samples, norm_mix[0], gm_w_in[0], gm_b_in[0],
        gm_v_gain[0], gm_w_s[0], gm_b_s[0], gm_w_out[0], norm_ffn[0], moe_router_w[0],
        moe_router_b[0])
    x_all = _moe(x_new, f, idx, gates, rank, cnt, mods[0], row_all,
                 0, moe_w_gu, moe_b_gu, moe_w_down, moe_b_down)

    q, k_all, v_all = _qkv_layer(x_all, mods[1], n_ctx_tiles, tps, n_samples, n_ctx, n_lat,
                                 norm_mix[1], at_w_qkv[0], at_q_gain[0], at_k_gain[0])
    x_new, f, idx, gates, rank, cnt = _attn_layer(
        x_all, q, k_all, v_all, mods[1], n_samples * n_ctx // TA, n_lat // TA, n_samples, at_w_o[0], norm_ffn[1],
        moe_router_w[1], moe_router_b[1])
    out = _moe(x_new, f, idx, gates, rank, cnt, mods[1], lambda i: i // tps,
               1, moe_w_gu, moe_b_gu, moe_w_down, moe_b_down)
    return out.reshape(n_samples, n_lat, d)
```

```python
import functools

import jax
import jax.numpy as jnp
from jax import lax
from jax.experimental import pallas as pl
from jax.experimental.pallas import tpu as pltpu

F32 = jnp.float32
BF16 = jnp.bfloat16
HIGHEST = lax.Precision.HIGHEST

GRID_W = 64
N_MOD = 6
NORM_EPS = 1e-6
CHUNK = 128
GM_GROUPS = 8
HEAD_DIM = 128
N_KV_HEADS = 2
AXIS_DIM = HEAD_DIM // 2
ROPE_THETA = 10000.0
TOP_K = 4
SWIGLU_LIMIT = 7.0
SWIGLU_ALPHA = 1.702

TM = 256
TQ = 256
TG = 512
TA = 512
TE = 512
ROW_DMA_UNROLL = 8
ATTN_HEADS_PER_DOT = 2
MOD_ROWS = 16
V7X_VMEM_LIMIT = 56 * 1024 * 1024


def _cparams(sem, vmem=V7X_VMEM_LIMIT):
    return pltpu.CompilerParams(dimension_semantics=sem, vmem_limit_bytes=vmem)


def _const_spec(shape):
    nd = len(shape)
    return pl.BlockSpec(shape, lambda *_: (0,) * nd, pipeline_mode=pl.Buffered(1))


LANES = 128
SUBLANES = 8


def _load_tile_rows(ref, n):
    return jnp.concatenate([ref[pl.ds(s, n, stride=SUBLANES), :] for s in range(SUBLANES)], axis=1)


def _store_tile_rows(ref, val):
    n = val.shape[0]
    for s in range(SUBLANES):
        ref[pl.ds(s, n, stride=SUBLANES), :] = val[:, s * LANES:(s + 1) * LANES]


def _sigmoid(x):
    return 1.0 / (1.0 + jnp.exp(-x))


def _rms(x):
    return x * lax.rsqrt(jnp.mean(x * x, axis=-1, keepdims=True) + NORM_EPS)


def _ada_kernel(s_ref, w_ref, b_ref, o_ref):
    s = s_ref[...]
    s = s * _sigmoid(s)
    o_ref[0] = jnp.dot(s, w_ref[0], precision=HIGHEST, preferred_element_type=F32) + b_ref[0]


def _ada_table(c, c_ctx, ada_w, ada_b):
    depth, d, n = ada_w.shape
    b = c.shape[0]
    s = jnp.concatenate([c, c_ctx[None, :], jnp.zeros((MOD_ROWS - b - 1, d), F32)], axis=0)
    tn = 1536
    return pl.pallas_call(
        _ada_kernel,
        grid=(depth, n // tn),
        in_specs=[
            pl.BlockSpec((MOD_ROWS, d), lambda i, j: (0, 0)),
            pl.BlockSpec((1, d, tn), lambda i, j: (i, 0, j)),
            pl.BlockSpec((1, 1, tn), lambda i, j: (i, 0, j)),
        ],
        out_specs=pl.BlockSpec((1, MOD_ROWS, tn), lambda i, j: (i, 0, j)),
        out_shape=jax.ShapeDtypeStruct((depth, MOD_ROWS, n), F32),
        compiler_params=_cparams(("arbitrary", "arbitrary")),
        name="ada_table",
    )(s, ada_w, ada_b.reshape(depth, 1, n))


def _mod_slices(mod_ref, row, d, first):
    return [mod_ref[pl.ds(row, 1), pl.ds((first + k) * d, d)] for k in range(3)]


def _router_epilogue(step, x_new, sh2, sc2, nf_ref, rwt_ref, rb_ref,
                     f_ref, idx_ref, gate_ref, rank_ref, cnt_ref, base_ref):
    tm = x_new.shape[0]
    n_exp = rwt_ref.shape[0] // 2
    f = _rms(x_new) * nf_ref[...] * (1.0 + sc2) + sh2
    _store_tile_rows(f_ref, f)

    nt = (((1,), (1,)), ((), ()))
    f_hi = f.astype(BF16)
    f_lo = (f - f_hi.astype(F32)).astype(BF16)
    l_hi = lax.dot_general(rwt_ref[...], f_hi, nt, preferred_element_type=F32)
    l_lo = lax.dot_general(rwt_ref[:n_exp, :], f_lo, nt, preferred_element_type=F32)
    logits = l_hi[:n_exp, :] + l_hi[n_exp:, :] + l_lo + rb_ref[...]
    eid = lax.broadcasted_iota(jnp.int32, (n_exp, tm), 0).astype(F32)

    @pl.when(step == 0)
    def _():
        base_ref[...] = jnp.zeros_like(base_ref)

    r_io = lax.broadcasted_iota(jnp.int32, (tm, tm), 0)
    c_io = lax.broadcasted_iota(jnp.int32, (tm, tm), 1)
    before = jnp.where(r_io < c_io, 1.0, 0.0).astype(BF16)
    ones = jnp.ones((tm, tm), BF16)

    vals, idxs, hits = [], [], []
    l = logits
    for _ in range(TOP_K):
        m = jnp.max(l, axis=0, keepdims=True)
        sel = jnp.min(jnp.where(l == m, eid, float(n_exp)), axis=0, keepdims=True)
        hit = eid == sel
        l = jnp.where(hit, -jnp.inf, l)
        vals.append(m)
        idxs.append(sel)
        hits.append(hit)
    onehot = jnp.concatenate([jnp.where(h, 1.0, 0.0) for h in hits], axis=0).astype(BF16)
    prefix = jnp.dot(onehot, before, preferred_element_type=F32)
    count = jnp.dot(onehot, ones, preferred_element_type=F32)
    base = base_ref[...]
    ranks = []
    for k, hit in enumerate(hits):
        pk = prefix[k * n_exp:(k + 1) * n_exp, :]
        ranks.append(jnp.sum(jnp.where(hit, base + pk, 0.0), axis=0, keepdims=True))
        base = base + count[k * n_exp:(k + 1) * n_exp, :]
    base_ref[...] = base
    es = [jnp.exp(v - vals[0]) for v in vals]
    tot = es[0] + es[1] + es[2] + es[3]
    zero = jnp.zeros_like(tot)
    gate_ref[...] = jnp.concatenate([e / tot for e in es] + [zero] * (8 - TOP_K), axis=0)
    idx_ref[...] = jnp.concatenate(idxs, axis=0).astype(jnp.int32)
    rank_ref[...] = jnp.concatenate(ranks, axis=0).astype(jnp.int32)
    cnt_ref[...] = base_ref[:, :128].astype(jnp.int32)


def _router_weights(rw):
    hi = rw.T.astype(BF16)
    lo = (rw.T - hi.astype(F32)).astype(BF16)
    return jnp.concatenate([hi, lo], axis=0)


def _router_out(t, d, n_exp):
    shapes = (
        jax.ShapeDtypeStruct((t * SUBLANES, LANES), F32),
        jax.ShapeDtypeStruct((TOP_K, t), jnp.int32),
        jax.ShapeDtypeStruct((8, t), F32),
        jax.ShapeDtypeStruct((TOP_K, t), jnp.int32),
        jax.ShapeDtypeStruct((n_exp, 128), jnp.int32),
    )
    return shapes


def _router_out_specs(tile_of, tm, d, n_exp):
    return (
        pl.BlockSpec((tm * SUBLANES, LANES), lambda *g: (tile_of(*g), 0)),
        pl.BlockSpec((TOP_K, tm), lambda *g: (0, tile_of(*g))),
        pl.BlockSpec((8, tm), lambda *g: (0, tile_of(*g))),
        pl.BlockSpec((TOP_K, tm), lambda *g: (0, tile_of(*g))),
        pl.BlockSpec((n_exp, 128), lambda *g: (0, 0)),
    )


def _gmlp_kernel(n_ctx_tiles, tiles_per_sample, n_samples,
                 c_ref, x_ref, mod_ref, nm_ref, win_ref, bin_ref, vg_ref, ws_ref, bs_ref, wout_ref,
                 nf_ref, rwt_ref, rb_ref,
                 xo_ref, f_ref, idx_ref, gate_ref, rank_ref, cnt_ref, base_ref):
    i = pl.program_id(0)
    d = x_ref.shape[1]
    gw = wout_ref.shape[0]
    gc = gw // GM_GROUPS
    row = jnp.where(i < n_ctx_tiles, n_samples, (i - n_ctx_tiles) // tiles_per_sample)
    sh1, sc1, g1 = _mod_slices(mod_ref, row, d, 0)
    sh2, sc2, _ = _mod_slices(mod_ref, row, d, 3)

    x = jnp.where(i < n_ctx_tiles, c_ref[...], x_ref[...])
    h = _rms(x) * nm_ref[...] * (1.0 + sc1) + sh1
    hb = h.astype(BF16)

    def proj(col):
        a = jnp.dot(hb, win_ref[:, col:col + gc], preferred_element_type=F32) + bin_ref[:, col:col + gc]
        return 0.5 * a * (1.0 + lax.erf(a * (2.0 ** -0.5)))

    u = [proj(g * gc) for g in range(GM_GROUPS)]
    v = [proj(gw + g * gc) for g in range(GM_GROUPS)]
    ssq = v[0] * v[0]
    for g in range(1, GM_GROUPS):
        ssq = ssq + v[g] * v[g]
    inv = lax.rsqrt(jnp.sum(ssq, axis=-1, keepdims=True) * (1.0 / gw) + NORM_EPS)
    cols = []
    for g in range(GM_GROUPS):
        vn = (v[g] * inv * vg_ref[:, g * gc:(g + 1) * gc]).astype(BF16)
        s = jnp.concatenate(
            [jnp.dot(ws_ref[g], vn[c * CHUNK:(c + 1) * CHUNK, :], preferred_element_type=F32)
             + bs_ref[:, g * gc:(g + 1) * gc] for c in range(x.shape[0] // CHUNK)], axis=0)
        cols.append((u[g] * s).astype(BF16))
    z = jnp.concatenate(cols, axis=1)
    y = jnp.dot(z, wout_ref[...], preferred_element_type=F32)
    x_new = x + g1 * y
    xo_ref[...] = x_new
    _router_epilogue(i, x_new, sh2, sc2, nf_ref, rwt_ref, rb_ref,
                     f_ref, idx_ref, gate_ref, rank_ref, cnt_ref, base_ref)


def _gmlp_layer(ctx2d, x2d, mod, n_ctx_tiles, tiles_per_sample, n_samples,
                nm, w_in, b_in, v_gain, w_s, b_s, w_out, nf, rw, rb):
    d = x2d.shape[1]
    t = ctx2d.shape[0] + x2d.shape[0]
    gw = w_out.shape[0]
    n_exp = rw.shape[1]
    gc = gw // GM_GROUPS
    bs_full = jnp.repeat(b_s.T, gc, axis=1)
    tile = lambda i: i
    outs = pl.pallas_call(
        functools.partial(_gmlp_kernel, n_ctx_tiles, tiles_per_sample, n_samples),
        grid=(t // TG,),
        in_specs=[
            pl.BlockSpec((TG, d), lambda i: (jnp.minimum(i, n_ctx_tiles - 1), 0)),
            pl.BlockSpec((TG, d), lambda i: (jnp.maximum(i - n_ctx_tiles, 0), 0)),
            _const_spec((MOD_ROWS, N_MOD * d)),
            _const_spec((1, d)),
            _const_spec((d, 2 * gw)),
            _const_spec((1, 2 * gw)),
            _const_spec((1, gw)),
            _const_spec((GM_GROUPS, CHUNK, CHUNK)),
            _const_spec((CHUNK, gw)),
            _const_spec((gw, d)),
            _const_spec((1, d)),
            _const_spec((2 * n_exp, d)),
            _const_spec((n_exp, 1)),
        ],
        out_specs=(pl.BlockSpec((TG, d), lambda i: (i, 0)),) + _router_out_specs(tile, TG, d, n_exp),
        out_shape=(jax.ShapeDtypeStruct((t, d), F32),) + _router_out(t, d, n_exp),
        scratch_shapes=[pltpu.VMEM((n_exp, TG), F32)],
        compiler_params=_cparams(("arbitrary",)),
        name="gmlp_mixer",
    )(ctx2d, x2d, mod, nm.reshape(1, d), w_in.astype(BF16), b_in.reshape(1, -1), v_gain.reshape(1, gw),
      w_s.astype(BF16), bs_full, w_out.astype(BF16), nf.reshape(1, d), _router_weights(rw), rb.reshape(n_exp, 1))
    return outs


def _qkv_kernel(n_ctx_tiles, tiles_per_sample, n_samples,
                x_ref, mod_ref, nm_ref, w_ref, qg_ref, kg_ref, cos_ref, sin_ref,
                q_ref, k_ref, v_ref):
    i = pl.program_id(0)
    d = x_ref.shape[1]
    nq = q_ref.shape[1]
    nkv = k_ref.shape[1]
    row = jnp.where(i < n_ctx_tiles, n_samples, (i - n_ctx_tiles) // tiles_per_sample)
    sh1, sc1, _ = _mod_slices(mod_ref, row, d, 0)
    h = _rms(x_ref[...]) * nm_ref[...] * (1.0 + sc1) + sh1
    qkv = jnp.dot(h.astype(BF16), w_ref[...], preferred_element_type=F32)
    cos = cos_ref[...]
    sin = sin_ref[...]
    half = AXIS_DIM // 2
    lane = lax.broadcasted_iota(jnp.int32, (x_ref.shape[0], HEAD_DIM), 1)
    first_half = (lane % AXIS_DIM) < half

    def head(xh, gain):
        xh = _rms(xh) * gain
        partner = jnp.where(first_half, pltpu.roll(xh, HEAD_DIM - half, 1), pltpu.roll(xh, half, 1))
        return xh * cos + partner * sin

    q = [head(qkv[:, j * HEAD_DIM:(j + 1) * HEAD_DIM], qg_ref[...]) for j in range(nq // HEAD_DIM)]
    k = [head(qkv[:, nq + j * HEAD_DIM:nq + (j + 1) * HEAD_DIM], kg_ref[...])
         for j in range(nkv // HEAD_DIM)]
    q_ref[...] = jnp.concatenate(q, axis=1).astype(BF16)
    k_ref[...] = jnp.concatenate(k, axis=1).astype(BF16)
    v_ref[...] = qkv[:, nq + nkv:].astype(BF16)


def _rope_tables(n_lat):
    rows = n_lat // GRID_W
    row = jnp.repeat(jnp.arange(rows, dtype=jnp.int32), GRID_W).astype(F32)
    col = jnp.tile(jnp.arange(GRID_W, dtype=jnp.int32), rows).astype(F32)
    inv_freq = 1.0 / (ROPE_THETA ** (jnp.arange(0, AXIS_DIM, 2, dtype=F32) / AXIS_DIM))
    ang_r = row[:, None] * inv_freq
    ang_c = col[:, None] * inv_freq
    cos = jnp.concatenate([jnp.cos(ang_r)] * 2 + [jnp.cos(ang_c)] * 2, axis=1)
    sin = jnp.concatenate([-jnp.sin(ang_r), jnp.sin(ang_r), -jnp.sin(ang_c), jnp.sin(ang_c)], axis=1)
    cos = jnp.concatenate([jnp.ones((TQ, HEAD_DIM), F32), cos], axis=0)
    sin = jnp.concatenate([jnp.zeros((TQ, HEAD_DIM), F32), sin], axis=0)
    return cos, sin


def _qkv_layer(x_all, mod, n_ctx_tiles, tiles_per_sample, n_samples, n_lat,
               nm, w_qkv, q_gain, k_gain):
    t, d = x_all.shape
    nqkv = w_qkv.shape[1]
    nkv = N_KV_HEADS * HEAD_DIM
    nq = nqkv - 2 * nkv
    cos, sin = _rope_tables(n_lat)
    tps = tiles_per_sample

    def pos_block(i):
        return jnp.where(i < n_ctx_tiles, 0, 1 + (i - n_ctx_tiles) % tps)

    return pl.pallas_call(
        functools.partial(_qkv_kernel, n_ctx_tiles, tiles_per_sample, n_samples),
        grid=(t // TQ,),
        in_specs=[
            pl.BlockSpec((TQ, d), lambda i: (i, 0)),
            _const_spec((MOD_ROWS, N_MOD * d)),
            _const_spec((1, d)),
            _const_spec((d, nqkv)),
            _const_spec((1, HEAD_DIM)),
            _const_spec((1, HEAD_DIM)),
            pl.BlockSpec((TQ, HEAD_DIM), lambda i: (pos_block(i), 0)),
            pl.BlockSpec((TQ, HEAD_DIM), lambda i: (pos_block(i), 0)),
        ],
        out_specs=(
            pl.BlockSpec((TQ, nq), lambda i: (i, 0)),
            pl.BlockSpec((TQ, nkv), lambda i: (i, 0)),
            pl.BlockSpec((TQ, nkv), lambda i: (i, 0)),
        ),
        out_shape=(
            jax.ShapeDtypeStruct((t, nq), BF16),
            jax.ShapeDtypeStruct((t, nkv), BF16),
            jax.ShapeDtypeStruct((t, nkv), BF16),
        ),
        compiler_params=_cparams(("arbitrary",)),
        name="qkv_rope",
    )(x_all, mod, nm.reshape(1, d), w_qkv.astype(BF16), q_gain.reshape(1, HEAD_DIM),
      k_gain.reshape(1, HEAD_DIM), cos, sin)


def _attn_kernel(tiles_per_sample,
                 x_ref, q_ref, kc_ref, kl_ref, vc_ref, vl_ref, mod_ref, wo_ref, nf_ref, rwt_ref, rb_ref,
                 xo_ref, f_ref, idx_ref, gate_ref, rank_ref, cnt_ref, base_ref):
    b = pl.program_id(0)
    j = pl.program_id(1)
    d = x_ref.shape[1]
    n_heads = q_ref.shape[1] // HEAD_DIM
    group = n_heads // N_KV_HEADS
    _, _, g1 = _mod_slices(mod_ref, b, d, 0)
    sh2, sc2, _ = _mod_slices(mod_ref, b, d, 3)
    tq = q_ref.shape[0]
    exp2_scale = (HEAD_DIM ** -0.5) * 1.4426950408889634
    outs = []
    hpd = ATTN_HEADS_PER_DOT
    for h0 in range(0, n_heads, hpd):
        g = h0 // group
        qg = jnp.concatenate([q_ref[:, h * HEAD_DIM:(h + 1) * HEAD_DIM]
                              for h in range(h0, h0 + hpd)], axis=0)
        cols = slice(g * HEAD_DIM, (g + 1) * HEAD_DIM)
        kg = jnp.concatenate([kc_ref[:, cols], kl_ref[:, cols]], axis=0)
        vg = jnp.concatenate([vc_ref[:, cols], vl_ref[:, cols]], axis=0)
        s = lax.dot_general(qg, kg, (((1,), (1,)), ((), ())), preferred_element_type=F32)
        p = jnp.exp2((s - jnp.max(s, axis=-1, keepdims=True)) * exp2_scale)
        den = jnp.sum(p, axis=-1, keepdims=True)
        pb = p.astype(BF16)
        for h in range(hpd):
            rows = slice(h * tq, (h + 1) * tq)
            o = jnp.dot(pb[rows, :], vg, preferred_element_type=F32) / den[rows, :]
            outs.append(o.astype(BF16))
    o_all = jnp.concatenate(outs, axis=1)
    y = jnp.dot(o_all, wo_ref[...], preferred_element_type=F32)
    x_new = x_ref[...] + g1 * y
    xo_ref[...] = x_new
    _router_epilogue(b * tiles_per_sample + j, x_new, sh2, sc2, nf_ref, rwt_ref, rb_ref,
                     f_ref, idx_ref, gate_ref, rank_ref, cnt_ref, base_ref)


def _attn_layer(x_all, q, k_all, v_all, mod, n_ctx_tiles, tiles_per_sample, n_samples, n_ctx,
                w_o, nf, rw, rb):
    d = x_all.shape[1]
    t_lat = n_samples * tiles_per_sample * TA
    n_exp = rw.shape[1]
    nq = q.shape[1]
    nkv = k_all.shape[1]
    n_lat = tiles_per_sample * TA
    lat_block0 = n_samples * n_ctx // n_lat
    ctx_kv = pl.BlockSpec((n_ctx, nkv), lambda b, j: (b, 0))
    lat_kv = pl.BlockSpec((n_lat, nkv), lambda b, j: (lat_block0 + b, 0))
    tps = tiles_per_sample
    lat_tile = lambda b, j: n_ctx_tiles + b * tps + j
    tile = lambda b, j: b * tps + j
    return pl.pallas_call(
        functools.partial(_attn_kernel, tiles_per_sample),
        grid=(n_samples, tps),
        in_specs=[
            pl.BlockSpec((TA, d), lambda b, j: (lat_tile(b, j), 0)),
            pl.BlockSpec((TA, nq), lambda b, j: (lat_tile(b, j), 0)),
            ctx_kv, lat_kv, ctx_kv, lat_kv,
            _const_spec((MOD_ROWS, N_MOD * d)),
            _const_spec((nq, d)),
            _const_spec((1, d)),
            _const_spec((2 * n_exp, d)),
            _const_spec((n_exp, 1)),
        ],
        out_specs=(pl.BlockSpec((TA, d), lambda b, j: (tile(b, j), 0)),)
        + _router_out_specs(tile, TA, d, n_exp),
        out_shape=(jax.ShapeDtypeStruct((t_lat, d), F32),) + _router_out(t_lat, d, n_exp),
        scratch_shapes=[pltpu.VMEM((n_exp, TA), F32)],
        compiler_params=_cparams(("arbitrary", "arbitrary")),
        name="attn_mixer",
    )(x_all, q, k_all, k_all, v_all, v_all, mod, w_o.astype(BF16), nf.reshape(1, d), _router_weights(rw),
      rb.reshape(n_exp, 1))


def _dispatch_kernel(tail_ref, used_ref, dest_ref, f_hbm, xs_hbm, zeros, fbuf, sem, zsem, fsem):
    i = pl.program_id(0)
    n_tiles = pl.num_programs(0)
    tm = dest_ref.shape[0] // TOP_K
    tile_rows = tm * SUBLANES
    n_exp = tail_ref.shape[0]
    block_rows = zeros.shape[0]
    n_blocks = xs_hbm.shape[0] // block_rows

    def load(tile):
        src = f_hbm.at[pl.ds(pl.multiple_of(tile * tile_rows, tile_rows), tile_rows)]
        return pltpu.make_async_copy(src, fbuf.at[tile % 3], fsem.at[tile % 3])

    def drain(tile):
        for k in range(TOP_K):
            pltpu.make_async_copy(fbuf.at[0], xs_hbm.at[pl.ds(0, tile_rows)], sem.at[tile % 2]).wait()

    @pl.when(i == 0)
    def _():
        load(0).start()
        zeros[...] = jnp.zeros_like(zeros)

        def fill(blk):
            start = pl.multiple_of(blk * block_rows, block_rows)
            return pltpu.make_async_copy(zeros, xs_hbm.at[pl.ds(start, block_rows)], zsem)

        def each(fn):
            def tail(e, c):
                @pl.when(tail_ref[e] >= 0)
                def _():
                    fn(fill(tail_ref[e]))
                return c

            def unused(j, c):
                @pl.when(used_ref[0] + j < n_blocks)
                def _():
                    fn(fill(used_ref[0] + j))
                return c

            lax.fori_loop(0, n_exp, tail, 0)
            lax.fori_loop(0, n_exp, unused, 0)

        each(lambda c: c.start())
        each(lambda c: c.wait())

    @pl.when(i + 1 < n_tiles)
    def _():
        load(i + 1).start()

    load(i).wait()
    tile = fbuf.at[i % 3]

    def start(t, c):
        src = tile.at[pl.ds(pl.multiple_of(t * SUBLANES, SUBLANES), SUBLANES)]
        for k in range(TOP_K):
            row = xs_hbm.at[pl.ds(pl.multiple_of(dest_ref[t * TOP_K + k] * SUBLANES, SUBLANES), SUBLANES)]
            pltpu.make_async_copy(src, row, sem.at[i % 2]).start(priority=k % 2)
        return c

    lax.fori_loop(0, tm, start, 0, unroll=ROW_DMA_UNROLL)

    @pl.when(i >= 1)
    def _():
        drain(i - 1)

    @pl.when(i == n_tiles - 1)
    def _():
        drain(i)


def _dispatch(f, dest, tail_blk, used_blocks, n_blocks):
    t = dest.shape[0] // TOP_K
    grid_spec = pltpu.PrefetchScalarGridSpec(
        num_scalar_prefetch=2,
        grid=(t // TM,),
        in_specs=[pl.BlockSpec((TM * TOP_K,), lambda i, *_: (i,), memory_space=pltpu.SMEM),
                  pl.BlockSpec(memory_space=pl.ANY)],
        out_specs=pl.BlockSpec(memory_space=pl.ANY),
        scratch_shapes=[pltpu.VMEM((TE * SUBLANES, LANES), F32),
                        pltpu.VMEM((3, TM * SUBLANES, LANES), F32),
                        pltpu.SemaphoreType.DMA((2,)), pltpu.SemaphoreType.DMA,
                        pltpu.SemaphoreType.DMA((3,))],
    )
    return pl.pallas_call(
        _dispatch_kernel,
        grid_spec=grid_spec,
        out_shape=jax.ShapeDtypeStruct((n_blocks * TE * SUBLANES, LANES), F32),
        compiler_params=_cparams(("arbitrary",)),
        name="moe_dispatch",
    )(tail_blk, used_blocks, dest, f)


def _expert_kernel(layer, exp_ref, used_ref, nxt_ref,
                   xs_hbm, wgu_hbm, bgu_ref, wd_hbm, bd_ref, yb_hbm,
                   xbuf, ybuf, gu_stage, d_stage, wgu_b, wd_b, wsem, xsem, ysem):
    w = pl.program_id(0)
    n_blocks = pl.num_programs(0)
    te = xbuf.shape[1]
    de = wd_b.shape[0]
    slot = w % 2
    new_expert = (w == 0) | (exp_ref[w] != exp_ref[jnp.maximum(w - 1, 0)])

    def x_copies(blk, sl):
        rows = pl.ds(pl.multiple_of(blk * te, te), te)
        return [pltpu.make_async_copy(xs_hbm.at[rows, s], xbuf.at[sl, :, pl.ds(s * LANES, LANES)],
                                      xsem.at[sl]) for s in range(SUBLANES)]

    def y_copies(blk, sl):
        rows = pl.ds(pl.multiple_of(blk * te, te), te)
        return [pltpu.make_async_copy(ybuf.at[sl, :, pl.ds(s * LANES, LANES)], yb_hbm.at[rows, s],
                                      ysem.at[sl]) for s in range(SUBLANES)]

    def weight_copies(e):
        return (pltpu.make_async_copy(wgu_hbm.at[layer, e], gu_stage, wsem.at[0]),
                pltpu.make_async_copy(wd_hbm.at[layer, e], d_stage, wsem.at[1]))

    @pl.when(w == 0)
    def _():
        for c in x_copies(0, 0):
            c.start()
        for c in weight_copies(exp_ref[0]):
            c.start()

    @pl.when(w + 1 < n_blocks)
    def _():
        for c in x_copies(w + 1, 1 - slot):
            c.start()

    @pl.when(new_expert)
    def _():
        for c in weight_copies(exp_ref[w]):
            c.wait()
        wgu_b[...] = gu_stage[...].astype(BF16)
        wd_b[...] = d_stage[...].astype(BF16)

        @pl.when(nxt_ref[w] >= 0)
        def _():
            for c in weight_copies(nxt_ref[w]):
                c.start()

    for c in x_copies(w, slot):
        c.wait()

    @pl.when(w >= 2)
    def _():
        for c in y_copies(w - 2, slot):
            c.wait()

    @pl.when(used_ref[w] == 0)
    def _():
        ybuf[slot] = jnp.zeros(ybuf.shape[1:], F32)

    @pl.when(used_ref[w] != 0)
    def _():
        x = xbuf[slot].astype(BF16)
        gu = jnp.dot(x, wgu_b[...], preferred_element_type=F32) + bgu_ref[0]
        g = jnp.minimum(gu[:, :de], SWIGLU_LIMIT)
        u = jnp.clip(gu[:, de:], -SWIGLU_LIMIT, SWIGLU_LIMIT)
        act = g * _sigmoid(SWIGLU_ALPHA * g) * (u + 1.0)
        ybuf[slot] = jnp.dot(act.astype(BF16), wd_b[...], preferred_element_type=F32) + bd_ref[0]

    for c in y_copies(w, slot):
        c.start()

    @pl.when(w == n_blocks - 1)
    def _():
        for c in y_copies(w, slot):
            c.wait()

        @pl.when(w >= 1)
        def _():
            for c in y_copies(w - 1, 1 - slot):
                c.wait()


def _experts(xs, work, layer, w_gu, b_gu, w_down, b_down):
    _, n_exp, d, de2 = w_gu.shape
    de = w_down.shape[2]
    n_blocks = work[0].shape[0]
    bias = lambda w, ex, used, nxt: (ex[w], 0, 0)
    grid_spec = pltpu.PrefetchScalarGridSpec(
        num_scalar_prefetch=3,
        grid=(n_blocks,),
        in_specs=[
            pl.BlockSpec(memory_space=pl.ANY),
            pl.BlockSpec(memory_space=pl.ANY),
            pl.BlockSpec((1, 1, de2), bias),
            pl.BlockSpec(memory_space=pl.ANY),
            pl.BlockSpec((1, 1, d), bias),
        ],
        out_specs=pl.BlockSpec(memory_space=pl.ANY),
        scratch_shapes=[pltpu.VMEM((2, TE, d), F32), pltpu.VMEM((2, TE, d), F32),
                        pltpu.VMEM((d, de2), F32), pltpu.VMEM((de, d), F32),
                        pltpu.VMEM((d, de2), BF16), pltpu.VMEM((de, d), BF16),
                        pltpu.SemaphoreType.DMA((2,)), pltpu.SemaphoreType.DMA((2,)),
                        pltpu.SemaphoreType.DMA((2,))],
    )
    return pl.pallas_call(
        functools.partial(_expert_kernel, layer),
        grid_spec=grid_spec,
        out_shape=jax.ShapeDtypeStruct(xs.shape, F32),
        compiler_params=_cparams(("arbitrary",)),
        name="moe_experts",
    )(*work, xs, w_gu, b_gu[layer].reshape(n_exp, 1, de2), w_down, b_down[layer].reshape(n_exp, 1, d))


def _combine_kernel(row_of_tile, dest_ref, next_ref, gate_ref, x_ref, mod_ref, yb_hbm, o_ref, buf, sem):
    i = pl.program_id(0)
    n_tiles = pl.num_programs(0)
    tm, d = x_ref.shape
    half = i % 2

    def fetch(idx_ref, h):
        def start(t, c):
            slot = pl.ds(pl.multiple_of(t * SUBLANES, SUBLANES), SUBLANES)
            for k in range(TOP_K):
                row = pl.ds(pl.multiple_of(idx_ref[t * TOP_K + k] * SUBLANES, SUBLANES), SUBLANES)
                pltpu.make_async_copy(yb_hbm.at[row], buf.at[h, k, slot], sem.at[h]).start(priority=k % 2)
            return c

        lax.fori_loop(0, tm, start, 0, unroll=ROW_DMA_UNROLL)

    @pl.when(i == 0)
    def _():
        fetch(dest_ref, 0)

    @pl.when(i + 1 < n_tiles)
    def _():
        fetch(next_ref, 1 - half)

    g2 = mod_ref[pl.ds(row_of_tile(i), 1), pl.ds(5 * d, d)]
    gates = jnp.concatenate([gate_ref[...], jnp.zeros((LANES - 8, tm), F32)], axis=0).T
    for k in range(TOP_K):
        pltpu.make_async_copy(yb_hbm.at[pl.ds(0, tm * SUBLANES)], buf.at[half, k], sem.at[half]).wait()
    y = _load_tile_rows(buf.at[half, 0], tm) * gates[:, 0:1]
    for k in range(1, TOP_K):
        y = y + _load_tile_rows(buf.at[half, k], tm) * gates[:, k:k + 1]
    o_ref[...] = x_ref[...] + g2 * y


def _combine(x_new, yb, dest, gates, mod, row_of_tile):
    t, d = x_new.shape
    last = t // TM - 1
    return pl.pallas_call(
        functools.partial(_combine_kernel, row_of_tile),
        grid=(t // TM,),
        in_specs=[
            pl.BlockSpec((TM * TOP_K,), lambda i: (i,), memory_space=pltpu.SMEM),
            pl.BlockSpec((TM * TOP_K,), lambda i: (jnp.minimum(i + 1, last),), memory_space=pltpu.SMEM),
            pl.BlockSpec((8, TM), lambda i: (0, i)),
            pl.BlockSpec((TM, d), lambda i: (i, 0)),
            _const_spec((MOD_ROWS, N_MOD * d)),
            pl.BlockSpec(memory_space=pl.ANY),
        ],
        out_specs=pl.BlockSpec((TM, d), lambda i: (i, 0)),
        out_shape=jax.ShapeDtypeStruct((t, d), F32),
        scratch_shapes=[pltpu.VMEM((2, TOP_K, TM * SUBLANES, LANES), F32),
                        pltpu.SemaphoreType.DMA((2,))],
        compiler_params=_cparams(("arbitrary",)),
        name="moe_combine",
    )(dest, dest, gates, x_new, mod, yb)


def _expert_blocks(counts, n_assign):
    n_exp = counts.shape[0]
    n_blocks = n_assign // TE + n_exp
    padded = (counts + TE - 1) // TE * TE
    pad_end = jnp.cumsum(padded)
    pad_start = (pad_end - padded).astype(jnp.int32)
    used_blocks = pad_end[-1] // TE
    w = jnp.arange(n_blocks, dtype=jnp.int32)
    used = w < used_blocks
    ex = jnp.minimum(jnp.sum(w[:, None] * TE >= pad_end[None, :], axis=1), n_exp - 1).astype(jnp.int32)
    ex = jnp.where(used, ex, jnp.max(jnp.where(used, ex, 0)))
    ids = jnp.arange(n_exp, dtype=jnp.int32)
    later = jnp.where((ids[None, :] > ids[:, None]) & (counts[None, :] > 0), ids[None, :], n_exp)
    nxt_of = jnp.min(later, axis=1)
    nxt = jnp.sum(jnp.where(ex[:, None] == ids[None, :], nxt_of[None, :], 0), axis=1)
    nxt = jnp.where(nxt < n_exp, nxt, -1).astype(jnp.int32)
    tail_blk = jnp.where(counts > 0, pad_end // TE - 1, -1).astype(jnp.int32)
    work = (ex, used.astype(jnp.int32), nxt)
    return work, pad_start, tail_blk, used_blocks.astype(jnp.int32).reshape(1), n_blocks


def _moe(x_new, f, idx, gates, rank, cnt, mod, row_of_tile, layer, w_gu, b_gu, w_down, b_down):
    t = idx.shape[1]
    work, pad_start, tail_blk, used_blocks, n_blocks = _expert_blocks(cnt[:, 0], t * TOP_K)
    ids = jnp.arange(pad_start.shape[0], dtype=jnp.int32)
    dest = rank + jnp.sum(jnp.where(idx[..., None] == ids, pad_start, 0), axis=-1)
    dest = dest.T.reshape(-1)
    xs = _dispatch(f, dest, tail_blk, used_blocks, n_blocks)
    yb = _experts(xs.reshape(-1, SUBLANES, LANES), work, layer, w_gu, b_gu, w_down, b_down)
    return _combine(x_new, yb.reshape(-1, LANES), dest, gates, mod, row_of_tile)


def kernel(x, c, ctx, c_ctx, ada_w, ada_b, norm_mix, norm_ffn, gm_w_in, gm_b_in, gm_v_gain, gm_w_s,
           gm_b_s, gm_w_out, at_w_qkv, at_q_gain, at_k_gain, at_w_o, moe_router_w, moe_router_b,
           moe_w_gu, moe_b_gu, moe_w_down, moe_b_down):
    n_samples, n_lat, d = x.shape
    n_ctx = ctx.shape[1]
    assert d == SUBLANES * LANES, "MoE row movement assumes one f32 tile per token row"
    assert n_lat % TM == 0 and n_ctx % TM == 0 and n_samples < MOD_ROWS
    assert n_lat % TG == 0 and (n_samples * n_ctx) % TG == 0 and TG % CHUNK == 0
    assert n_lat % TA == 0 and (n_samples * n_ctx) % TA == 0
    assert n_lat % TQ == 0 and (n_samples * n_ctx) % TQ == 0
    assert (n_samples * n_ctx) % n_lat == 0, "latent K/V blocks are addressed in n_lat-row blocks"
    assert (n_samples * n_lat * TOP_K) % TE == 0 and (n_samples * n_ctx * TOP_K) % TE == 0
    n_ctx_tiles = n_samples * n_ctx // TM
    tps = n_lat // TM

    mods = _ada_table(c, c_ctx, ada_w, ada_b)

    def row_all(i):
        return jnp.where(i < n_ctx_tiles, n_samples, (i - n_ctx_tiles) // tps)

    x_new, f, idx, gates, rank, cnt = _gmlp_layer(
        ctx.reshape(-1, d), x.reshape(-1, d), mods[0], n_samples * n_ctx // TG, n_lat // TG, n_samples, norm_mix[0], gm_w_in[0], gm_b_in[0],
        gm_v_gain[0], gm_w_s[0], gm_b_s[0], gm_w_out[0], norm_ffn[0], moe_router_w[0],
        moe_router_b[0])
    x_all = _moe(x_new, f, idx, gates, rank, cnt, mods[0], row_all,
                 0, moe_w_gu, moe_b_gu, moe_w_down, moe_b_down)

    q, k_all, v_all = _qkv_layer(x_all, mods[1], n_samples * n_ctx // TQ, n_lat // TQ, n_samples, n_lat,
                                 norm_mix[1], at_w_qkv[0], at_q_gain[0], at_k_gain[0])
    x_new, f, idx, gates, rank, cnt = _attn_layer(
        x_all, q, k_all, v_all, mods[1], n_samples * n_ctx // TA, n_lat // TA, n_samples, n_ctx, at_w_o[0],
        norm_ffn[1],
        moe_router_w[1], moe_router_b[1])
    out = _moe(x_new, f, idx, gates, rank, cnt, mods[1], lambda i: i // tps,
               1, moe_w_gu, moe_b_gu, moe_w_down, moe_b_down)
    return out.reshape(n_samples, n_lat, d)
```

```python
import functools

import jax
import jax.numpy as jnp
from jax import lax
from jax.experimental import pallas as pl
from jax.experimental.pallas import tpu as pltpu

F32 = jnp.float32
BF16 = jnp.bfloat16
HIGHEST = lax.Precision.HIGHEST

GRID_W = 64
N_MOD = 6
NORM_EPS = 1e-6
CHUNK = 128
GM_GROUPS = 8
HEAD_DIM = 128
N_KV_HEADS = 2
AXIS_DIM = HEAD_DIM // 2
ROPE_THETA = 10000.0
TOP_K = 4
SWIGLU_LIMIT = 7.0
SWIGLU_ALPHA = 1.702

TM = 256
TQ = 256
TG = 512
TA = 512
TE = 512
ROW_DMA_UNROLL = 8
ATTN_HEADS_PER_DOT = 2
MOD_ROWS = 16
V7X_VMEM_LIMIT = 56 * 1024 * 1024


def _cparams(sem, vmem=V7X_VMEM_LIMIT):
    return pltpu.CompilerParams(dimension_semantics=sem, vmem_limit_bytes=vmem)


def _const_spec(shape):
    nd = len(shape)
    return pl.BlockSpec(shape, lambda *_: (0,) * nd, pipeline_mode=pl.Buffered(1))


LANES = 128
SUBLANES = 8


def _load_tile_rows(ref, n):
    return jnp.concatenate([ref[pl.ds(s, n, stride=SUBLANES), :] for s in range(SUBLANES)], axis=1)


def _store_tile_rows(ref, val):
    n = val.shape[0]
    for s in range(SUBLANES):
        ref[pl.ds(s, n, stride=SUBLANES), :] = val[:, s * LANES:(s + 1) * LANES]


def _sigmoid(x):
    return 1.0 / (1.0 + jnp.exp(-x))


def _rms(x):
    return x * lax.rsqrt(jnp.mean(x * x, axis=-1, keepdims=True) + NORM_EPS)


def _ada_kernel(s_ref, w_ref, b_ref, o_ref):
    s = s_ref[...]
    s = s * _sigmoid(s)
    o_ref[0] = jnp.dot(s, w_ref[0], precision=HIGHEST, preferred_element_type=F32) + b_ref[0]


def _ada_table(c, c_ctx, ada_w, ada_b):
    depth, d, n = ada_w.shape
    b = c.shape[0]
    s = jnp.concatenate([c, c_ctx[None, :], jnp.zeros((MOD_ROWS - b - 1, d), F32)], axis=0)
    tn = 1536
    return pl.pallas_call(
        _ada_kernel,
        grid=(depth, n // tn),
        in_specs=[
            pl.BlockSpec((MOD_ROWS, d), lambda i, j: (0, 0)),
            pl.BlockSpec((1, d, tn), lambda i, j: (i, 0, j)),
            pl.BlockSpec((1, 1, tn), lambda i, j: (i, 0, j)),
        ],
        out_specs=pl.BlockSpec((1, MOD_ROWS, tn), lambda i, j: (i, 0, j)),
        out_shape=jax.ShapeDtypeStruct((depth, MOD_ROWS, n), F32),
        compiler_params=_cparams(("arbitrary", "arbitrary")),
        name="ada_table",
    )(s, ada_w, ada_b.reshape(depth, 1, n))


def _mod_slices(mod_ref, row, d, first):
    return [mod_ref[pl.ds(row, 1), pl.ds((first + k) * d, d)] for k in range(3)]


def _router_epilogue(step, x_new, sh2, sc2, nf_ref, rwt_ref, rb_ref,
                     f_ref, idx_ref, gate_ref, rank_ref, cnt_ref, base_ref):
    tm = x_new.shape[0]
    n_exp = rwt_ref.shape[0] // 2
    f = _rms(x_new) * nf_ref[...] * (1.0 + sc2) + sh2
    _store_tile_rows(f_ref, f)

    nt = (((1,), (1,)), ((), ()))
    f_hi = f.astype(BF16)
    f_lo = (f - f_hi.astype(F32)).astype(BF16)
    l_hi = lax.dot_general(rwt_ref[...], f_hi, nt, preferred_element_type=F32)
    l_lo = lax.dot_general(rwt_ref[:n_exp, :], f_lo, nt, preferred_element_type=F32)
    logits = l_hi[:n_exp, :] + l_hi[n_exp:, :] + l_lo + rb_ref[...]
    eid = lax.broadcasted_iota(jnp.int32, (n_exp, tm), 0).astype(F32)

    @pl.when(step == 0)
    def _():
        base_ref[...] = jnp.zeros_like(base_ref)

    r_io = lax.broadcasted_iota(jnp.int32, (tm, tm), 0)
    c_io = lax.broadcasted_iota(jnp.int32, (tm, tm), 1)
    before = jnp.where(r_io < c_io, 1.0, 0.0).astype(BF16)
    ones = jnp.ones((tm, tm), BF16)

    vals, idxs, hits = [], [], []
    l = logits
    for _ in range(TOP_K):
        m = jnp.max(l, axis=0, keepdims=True)
        sel = jnp.min(jnp.where(l == m, eid, float(n_exp)), axis=0, keepdims=True)
        hit = eid == sel
        l = jnp.where(hit, -jnp.inf, l)
        vals.append(m)
        idxs.append(sel)
        hits.append(hit)
    onehot = jnp.concatenate([jnp.where(h, 1.0, 0.0) for h in hits], axis=0).astype(BF16)
    prefix = jnp.dot(onehot, before, preferred_element_type=F32)
    count = jnp.dot(onehot, ones, preferred_element_type=F32)
    base = base_ref[...]
    ranks = []
    for k, hit in enumerate(hits):
        pk = prefix[k * n_exp:(k + 1) * n_exp, :]
        ranks.append(jnp.sum(jnp.where(hit, base + pk, 0.0), axis=0, keepdims=True))
        base = base + count[k * n_exp:(k + 1) * n_exp, :]
    base_ref[...] = base
    es = [jnp.exp(v - vals[0]) for v in vals]
    tot = es[0] + es[1] + es[2] + es[3]
    zero = jnp.zeros_like(tot)
    gate_ref[...] = jnp.concatenate([e / tot for e in es] + [zero] * (8 - TOP_K), axis=0)
    idx_ref[...] = jnp.concatenate(idxs, axis=0).astype(jnp.int32)
    rank_ref[...] = jnp.concatenate(ranks, axis=0).astype(jnp.int32)
    cnt_ref[...] = base_ref[:, :128].astype(jnp.int32)


def _router_weights(rw):
    hi = rw.T.astype(BF16)
    lo = (rw.T - hi.astype(F32)).astype(BF16)
    return jnp.concatenate([hi, lo], axis=0)


def _router_out(t, d, n_exp):
    shapes = (
        jax.ShapeDtypeStruct((t * SUBLANES, LANES), F32),
        jax.ShapeDtypeStruct((TOP_K, t), jnp.int32),
        jax.ShapeDtypeStruct((8, t), F32),
        jax.ShapeDtypeStruct((TOP_K, t), jnp.int32),
        jax.ShapeDtypeStruct((n_exp, 128), jnp.int32),
    )
    return shapes


def _router_out_specs(tile_of, tm, d, n_exp):
    return (
        pl.BlockSpec((tm * SUBLANES, LANES), lambda *g: (tile_of(*g), 0)),
        pl.BlockSpec((TOP_K, tm), lambda *g: (0, tile_of(*g))),
        pl.BlockSpec((8, tm), lambda *g: (0, tile_of(*g))),
        pl.BlockSpec((TOP_K, tm), lambda *g: (0, tile_of(*g))),
        pl.BlockSpec((n_exp, 128), lambda *g: (0, 0)),
    )


def _gmlp_kernel(n_ctx_tiles, tiles_per_sample, n_samples,
                 c_ref, x_ref, mod_ref, nm_ref, win_ref, bin_ref, vg_ref, ws_ref, bs_ref, wout_ref,
                 nf_ref, rwt_ref, rb_ref,
                 xo_ref, f_ref, idx_ref, gate_ref, rank_ref, cnt_ref, base_ref):
    i = pl.program_id(0)
    d = x_ref.shape[1]
    gw = wout_ref.shape[0]
    gc = gw // GM_GROUPS
    row = jnp.where(i < n_ctx_tiles, n_samples, (i - n_ctx_tiles) // tiles_per_sample)
    sh1, sc1, g1 = _mod_slices(mod_ref, row, d, 0)
    sh2, sc2, _ = _mod_slices(mod_ref, row, d, 3)

    x = jnp.where(i < n_ctx_tiles, c_ref[...], x_ref[...])
    h = _rms(x) * nm_ref[...] * (1.0 + sc1) + sh1
    hb = h.astype(BF16)

    def proj(col):
        a = jnp.dot(hb, win_ref[:, col:col + gc], preferred_element_type=F32) + bin_ref[:, col:col + gc]
        return 0.5 * a * (1.0 + lax.erf(a * (2.0 ** -0.5)))

    u = [proj(g * gc) for g in range(GM_GROUPS)]
    v = [proj(gw + g * gc) for g in range(GM_GROUPS)]
    ssq = v[0] * v[0]
    for g in range(1, GM_GROUPS):
        ssq = ssq + v[g] * v[g]
    inv = lax.rsqrt(jnp.sum(ssq, axis=-1, keepdims=True) * (1.0 / gw) + NORM_EPS)
    cols = []
    for g in range(GM_GROUPS):
        vn = (v[g] * inv * vg_ref[:, g * gc:(g + 1) * gc]).astype(BF16)
        s = jnp.concatenate(
            [jnp.dot(ws_ref[g], vn[c * CHUNK:(c + 1) * CHUNK, :], preferred_element_type=F32)
             + bs_ref[:, g * gc:(g + 1) * gc] for c in range(x.shape[0] // CHUNK)], axis=0)
        cols.append((u[g] * s).astype(BF16))
    z = jnp.concatenate(cols, axis=1)
    y = jnp.dot(z, wout_ref[...], preferred_element_type=F32)
    x_new = x + g1 * y
    xo_ref[...] = x_new
    _router_epilogue(i, x_new, sh2, sc2, nf_ref, rwt_ref, rb_ref,
                     f_ref, idx_ref, gate_ref, rank_ref, cnt_ref, base_ref)


def _gmlp_layer(ctx2d, x2d, mod, n_ctx_tiles, tiles_per_sample, n_samples,
                nm, w_in, b_in, v_gain, w_s, b_s, w_out, nf, rw, rb):
    d = x2d.shape[1]
    t = ctx2d.shape[0] + x2d.shape[0]
    gw = w_out.shape[0]
    n_exp = rw.shape[1]
    gc = gw // GM_GROUPS
    bs_full = jnp.repeat(b_s.T, gc, axis=1)
    tile = lambda i: i
    outs = pl.pallas_call(
        functools.partial(_gmlp_kernel, n_ctx_tiles, tiles_per_sample, n_samples),
        grid=(t // TG,),
        in_specs=[
            pl.BlockSpec((TG, d), lambda i: (jnp.minimum(i, n_ctx_tiles - 1), 0)),
            pl.BlockSpec((TG, d), lambda i: (jnp.maximum(i - n_ctx_tiles, 0), 0)),
            _const_spec((MOD_ROWS, N_MOD * d)),
            _const_spec((1, d)),
            _const_spec((d, 2 * gw)),
            _const_spec((1, 2 * gw)),
            _const_spec((1, gw)),
            _const_spec((GM_GROUPS, CHUNK, CHUNK)),
            _const_spec((CHUNK, gw)),
            _const_spec((gw, d)),
            _const_spec((1, d)),
            _const_spec((2 * n_exp, d)),
            _const_spec((n_exp, 1)),
        ],
        out_specs=(pl.BlockSpec((TG, d), lambda i: (i, 0)),) + _router_out_specs(tile, TG, d, n_exp),
        out_shape=(jax.ShapeDtypeStruct((t, d), F32),) + _router_out(t, d, n_exp),
        scratch_shapes=[pltpu.VMEM((n_exp, TG), F32)],
        compiler_params=_cparams(("arbitrary",)),
        name="gmlp_mixer",
    )(ctx2d, x2d, mod, nm.reshape(1, d), w_in.astype(BF16), b_in.reshape(1, -1), v_gain.reshape(1, gw),
      w_s.astype(BF16), bs_full, w_out.astype(BF16), nf.reshape(1, d), _router_weights(rw), rb.reshape(n_exp, 1))
    return outs


def _qkv_kernel(n_ctx_tiles, tiles_per_sample, n_samples,
                x_ref, mod_ref, nm_ref, w_ref, qg_ref, kg_ref, cos_ref, sin_ref,
                q_ref, k_ref, v_ref):
    i = pl.program_id(0)
    d = x_ref.shape[1]
    nq = q_ref.shape[1]
    nkv = k_ref.shape[1]
    row = jnp.where(i < n_ctx_tiles, n_samples, (i - n_ctx_tiles) // tiles_per_sample)
    sh1, sc1, _ = _mod_slices(mod_ref, row, d, 0)
    h = _rms(x_ref[...]) * nm_ref[...] * (1.0 + sc1) + sh1
    qkv = jnp.dot(h.astype(BF16), w_ref[...], preferred_element_type=F32)
    cos = cos_ref[...]
    sin = sin_ref[...]
    half = AXIS_DIM // 2
    lane = lax.broadcasted_iota(jnp.int32, (x_ref.shape[0], HEAD_DIM), 1)
    first_half = (lane % AXIS_DIM) < half

    def head(xh, gain):
        xh = _rms(xh) * gain
        partner = jnp.where(first_half, pltpu.roll(xh, HEAD_DIM - half, 1), pltpu.roll(xh, half, 1))
        return xh * cos + partner * sin

    q = [head(qkv[:, j * HEAD_DIM:(j + 1) * HEAD_DIM], qg_ref[...]) for j in range(nq // HEAD_DIM)]
    k = [head(qkv[:, nq + j * HEAD_DIM:nq + (j + 1) * HEAD_DIM], kg_ref[...])
         for j in range(nkv // HEAD_DIM)]
    q_ref[...] = jnp.concatenate(q, axis=1).astype(BF16)
    k_ref[...] = jnp.concatenate(k, axis=1).astype(BF16)
    v_ref[...] = qkv[:, nq + nkv:].astype(BF16)


def _rope_tables(n_lat):
    rows = n_lat // GRID_W
    row = jnp.repeat(jnp.arange(rows, dtype=jnp.int32), GRID_W).astype(F32)
    col = jnp.tile(jnp.arange(GRID_W, dtype=jnp.int32), rows).astype(F32)
    inv_freq = 1.0 / (ROPE_THETA ** (jnp.arange(0, AXIS_DIM, 2, dtype=F32) / AXIS_DIM))
    ang_r = row[:, None] * inv_freq
    ang_c = col[:, None] * inv_freq
    cos = jnp.concatenate([jnp.cos(ang_r)] * 2 + [jnp.cos(ang_c)] * 2, axis=1)
    sin = jnp.concatenate([-jnp.sin(ang_r), jnp.sin(ang_r), -jnp.sin(ang_c), jnp.sin(ang_c)], axis=1)
    cos = jnp.concatenate([jnp.ones((TQ, HEAD_DIM), F32), cos], axis=0)
    sin = jnp.concatenate([jnp.zeros((TQ, HEAD_DIM), F32), sin], axis=0)
    return cos, sin


def _qkv_layer(x_all, mod, n_ctx_tiles, tiles_per_sample, n_samples, n_lat,
               nm, w_qkv, q_gain, k_gain):
    t, d = x_all.shape
    nqkv = w_qkv.shape[1]
    nkv = N_KV_HEADS * HEAD_DIM
    nq = nqkv - 2 * nkv
    cos, sin = _rope_tables(n_lat)
    tps = tiles_per_sample

    def pos_block(i):
        return jnp.where(i < n_ctx_tiles, 0, 1 + (i - n_ctx_tiles) % tps)

    return pl.pallas_call(
        functools.partial(_qkv_kernel, n_ctx_tiles, tiles_per_sample, n_samples),
        grid=(t // TQ,),
        in_specs=[
            pl.BlockSpec((TQ, d), lambda i: (i, 0)),
            _const_spec((MOD_ROWS, N_MOD * d)),
            _const_spec((1, d)),
            _const_spec((d, nqkv)),
            _const_spec((1, HEAD_DIM)),
            _const_spec((1, HEAD_DIM)),
            pl.BlockSpec((TQ, HEAD_DIM), lambda i: (pos_block(i), 0)),
            pl.BlockSpec((TQ, HEAD_DIM), lambda i: (pos_block(i), 0)),
        ],
        out_specs=(
            pl.BlockSpec((TQ, nq), lambda i: (i, 0)),
            pl.BlockSpec((TQ, nkv), lambda i: (i, 0)),
            pl.BlockSpec((TQ, nkv), lambda i: (i, 0)),
        ),
        out_shape=(
            jax.ShapeDtypeStruct((t, nq), BF16),
            jax.ShapeDtypeStruct((t, nkv), BF16),
            jax.ShapeDtypeStruct((t, nkv), BF16),
        ),
        compiler_params=_cparams(("arbitrary",)),
        name="qkv_rope",
    )(x_all, mod, nm.reshape(1, d), w_qkv.astype(BF16), q_gain.reshape(1, HEAD_DIM),
      k_gain.reshape(1, HEAD_DIM), cos, sin)


def _attn_kernel(tiles_per_sample,
                 x_ref, q_ref, kc_ref, kl_ref, vc_ref, vl_ref, mod_ref, wo_ref, nf_ref, rwt_ref, rb_ref,
                 xo_ref, f_ref, idx_ref, gate_ref, rank_ref, cnt_ref, base_ref):
    b = pl.program_id(0)
    j = pl.program_id(1)
    d = x_ref.shape[1]
    n_heads = q_ref.shape[1] // HEAD_DIM
    group = n_heads // N_KV_HEADS
    _, _, g1 = _mod_slices(mod_ref, b, d, 0)
    sh2, sc2, _ = _mod_slices(mod_ref, b, d, 3)
    tq = q_ref.shape[0]
    exp2_scale = (HEAD_DIM ** -0.5) * 1.4426950408889634
    outs = []
    hpd = ATTN_HEADS_PER_DOT
    for h0 in range(0, n_heads, hpd):
        g = h0 // group
        qg = jnp.concatenate([q_ref[:, h * HEAD_DIM:(h + 1) * HEAD_DIM]
                              for h in range(h0, h0 + hpd)], axis=0)
        cols = slice(g * HEAD_DIM, (g + 1) * HEAD_DIM)
        kg = jnp.concatenate([kc_ref[:, cols], kl_ref[:, cols]], axis=0)
        vg = jnp.concatenate([vc_ref[:, cols], vl_ref[:, cols]], axis=0)
        s = lax.dot_general(qg, kg, (((1,), (1,)), ((), ())), preferred_element_type=F32)
        p = jnp.exp2((s - jnp.max(s, axis=-1, keepdims=True)) * exp2_scale)
        den = jnp.sum(p, axis=-1, keepdims=True)
        pb = p.astype(BF16)
        for h in range(hpd):
            rows = slice(h * tq, (h + 1) * tq)
            o = jnp.dot(pb[rows, :], vg, preferred_element_type=F32) / den[rows, :]
            outs.append(o.astype(BF16))
    o_all = jnp.concatenate(outs, axis=1)
    y = jnp.dot(o_all, wo_ref[...], preferred_element_type=F32)
    x_new = x_ref[...] + g1 * y
    xo_ref[...] = x_new
    _router_epilogue(b * tiles_per_sample + j, x_new, sh2, sc2, nf_ref, rwt_ref, rb_ref,
                     f_ref, idx_ref, gate_ref, rank_ref, cnt_ref, base_ref)


def _attn_layer(x_all, q, k_all, v_all, mod, n_ctx_tiles, tiles_per_sample, n_samples, n_ctx,
                w_o, nf, rw, rb):
    d = x_all.shape[1]
    t_lat = n_samples * tiles_per_sample * TA
    n_exp = rw.shape[1]
    nq = q.shape[1]
    nkv = k_all.shape[1]
    n_lat = tiles_per_sample * TA
    lat_block0 = n_samples * n_ctx // n_lat
    ctx_kv = pl.BlockSpec((n_ctx, nkv), lambda b, j: (b, 0))
    lat_kv = pl.BlockSpec((n_lat, nkv), lambda b, j: (lat_block0 + b, 0))
    tps = tiles_per_sample
    lat_tile = lambda b, j: n_ctx_tiles + b * tps + j
    tile = lambda b, j: b * tps + j
    return pl.pallas_call(
        functools.partial(_attn_kernel, tiles_per_sample),
        grid=(n_samples, tps),
        in_specs=[
            pl.BlockSpec((TA, d), lambda b, j: (lat_tile(b, j), 0)),
            pl.BlockSpec((TA, nq), lambda b, j: (lat_tile(b, j), 0)),
            ctx_kv, lat_kv, ctx_kv, lat_kv,
            _const_spec((MOD_ROWS, N_MOD * d)),
            _const_spec((nq, d)),
            _const_spec((1, d)),
            _const_spec((2 * n_exp, d)),
            _const_spec((n_exp, 1)),
        ],
        out_specs=(pl.BlockSpec((TA, d), lambda b, j: (tile(b, j), 0)),)
        + _router_out_specs(tile, TA, d, n_exp),
        out_shape=(jax.ShapeDtypeStruct((t_lat, d), F32),) + _router_out(t_lat, d, n_exp),
        scratch_shapes=[pltpu.VMEM((n_exp, TA), F32)],
        compiler_params=_cparams(("arbitrary", "arbitrary")),
        name="attn_mixer",
    )(x_all, q, k_all, k_all, v_all, v_all, mod, w_o.astype(BF16), nf.reshape(1, d), _router_weights(rw),
      rb.reshape(n_exp, 1))


def _dispatch_kernel(tail_ref, used_ref, dest_ref, f_hbm, xs_hbm, zeros, fbuf, sem, zsem, fsem):
    i = pl.program_id(0)
    n_tiles = pl.num_programs(0)
    tm = dest_ref.shape[0] // TOP_K
    tile_rows = tm * SUBLANES
    n_exp = tail_ref.shape[0]
    block_rows = zeros.shape[0]
    n_blocks = xs_hbm.shape[0] // block_rows

    def load(tile):
        src = f_hbm.at[pl.ds(pl.multiple_of(tile * tile_rows, tile_rows), tile_rows)]
        return pltpu.make_async_copy(src, fbuf.at[tile % 3], fsem.at[tile % 3])

    def drain(tile):
        for k in range(TOP_K):
            pltpu.make_async_copy(fbuf.at[0], xs_hbm.at[pl.ds(0, tile_rows)], sem.at[tile % 2]).wait()

    def fill(blk, fsem_):
        start = pl.multiple_of(blk * block_rows, block_rows)
        return pltpu.make_async_copy(zeros, xs_hbm.at[pl.ds(start, block_rows)], fsem_)

    def tails(fn):
        def body(e, c):
            @pl.when(tail_ref[e] >= 0)
            def _():
                fn(fill(tail_ref[e], zsem.at[0]))
            return c

        lax.fori_loop(0, n_exp, body, 0)

    def spares(fn):
        def body(j, c):
            @pl.when(used_ref[0] + j < n_blocks)
            def _():
                fn(fill(used_ref[0] + j, zsem.at[1]))
            return c

        lax.fori_loop(0, n_exp, body, 0)

    @pl.when(i == 0)
    def _():
        load(0).start()
        zeros[...] = jnp.zeros_like(zeros)
        tails(lambda c: c.start())
        spares(lambda c: c.start())
        tails(lambda c: c.wait())

    @pl.when(i + 1 < n_tiles)
    def _():
        load(i + 1).start()

    load(i).wait()
    tile = fbuf.at[i % 3]

    def start(t, c):
        src = tile.at[pl.ds(pl.multiple_of(t * SUBLANES, SUBLANES), SUBLANES)]
        for k in range(TOP_K):
            row = xs_hbm.at[pl.ds(pl.multiple_of(dest_ref[k * tm + t] * SUBLANES, SUBLANES), SUBLANES)]
            pltpu.make_async_copy(src, row, sem.at[i % 2]).start(priority=k % 2)
        return c

    lax.fori_loop(0, tm, start, 0, unroll=ROW_DMA_UNROLL)

    @pl.when(i >= 1)
    def _():
        drain(i - 1)

    @pl.when(i == n_tiles - 1)
    def _():
        drain(i)
        spares(lambda c: c.wait())


def _dispatch(f, dest, tail_blk, used_blocks, n_blocks):
    t = dest.shape[0] // TOP_K
    grid_spec = pltpu.PrefetchScalarGridSpec(
        num_scalar_prefetch=2,
        grid=(t // TM,),
        in_specs=[pl.BlockSpec((TM * TOP_K,), lambda i, *_: (i,), memory_space=pltpu.SMEM),
                  pl.BlockSpec(memory_space=pl.ANY)],
        out_specs=pl.BlockSpec(memory_space=pl.ANY),
        scratch_shapes=[pltpu.VMEM((TE * SUBLANES, LANES), F32),
                        pltpu.VMEM((3, TM * SUBLANES, LANES), F32),
                        pltpu.SemaphoreType.DMA((2,)), pltpu.SemaphoreType.DMA((2,)),
                        pltpu.SemaphoreType.DMA((3,))],
    )
    return pl.pallas_call(
        _dispatch_kernel,
        grid_spec=grid_spec,
        out_shape=jax.ShapeDtypeStruct((n_blocks * TE * SUBLANES, LANES), F32),
        compiler_params=_cparams(("arbitrary",)),
        name="moe_dispatch",
    )(tail_blk, used_blocks, dest, f)


def _expert_kernel(layer, exp_ref, used_ref, nxt_ref,
                   xs_hbm, wgu_hbm, bgu_ref, wd_hbm, bd_ref, yb_hbm,
                   xbuf, ybuf, gu_stage, d_stage, wgu_b, wd_b, wsem, xsem, ysem):
    w = pl.program_id(0)
    n_blocks = pl.num_programs(0)
    te = xbuf.shape[1]
    de = wd_b.shape[0]
    slot = w % 2
    new_expert = (w == 0) | (exp_ref[w] != exp_ref[jnp.maximum(w - 1, 0)])

    def x_copies(blk, sl):
        rows = pl.ds(pl.multiple_of(blk * te, te), te)
        return [pltpu.make_async_copy(xs_hbm.at[rows, s], xbuf.at[sl, :, pl.ds(s * LANES, LANES)],
                                      xsem.at[sl]) for s in range(SUBLANES)]

    def y_copies(blk, sl):
        rows = pl.ds(pl.multiple_of(blk * te, te), te)
        return [pltpu.make_async_copy(ybuf.at[sl, :, pl.ds(s * LANES, LANES)], yb_hbm.at[rows, s],
                                      ysem.at[sl]) for s in range(SUBLANES)]

    def weight_copies(e):
        return (pltpu.make_async_copy(wgu_hbm.at[layer, e], gu_stage, wsem.at[0]),
                pltpu.make_async_copy(wd_hbm.at[layer, e], d_stage, wsem.at[1]))

    @pl.when(w == 0)
    def _():
        for c in x_copies(0, 0):
            c.start()
        for c in weight_copies(exp_ref[0]):
            c.start()

    @pl.when(w + 1 < n_blocks)
    def _():
        for c in x_copies(w + 1, 1 - slot):
            c.start()

    @pl.when(new_expert)
    def _():
        for c in weight_copies(exp_ref[w]):
            c.wait()
        wgu_b[...] = gu_stage[...].astype(BF16)
        wd_b[...] = d_stage[...].astype(BF16)

        @pl.when(nxt_ref[w] >= 0)
        def _():
            for c in weight_copies(nxt_ref[w]):
                c.start()

    for c in x_copies(w, slot):
        c.wait()

    @pl.when(w >= 2)
    def _():
        for c in y_copies(w - 2, slot):
            c.wait()

    @pl.when(used_ref[w] == 0)
    def _():
        ybuf[slot] = jnp.zeros(ybuf.shape[1:], F32)

    @pl.when(used_ref[w] != 0)
    def _():
        x = xbuf[slot].astype(BF16)
        gu = jnp.dot(x, wgu_b[...], preferred_element_type=F32) + bgu_ref[0]
        g = jnp.minimum(gu[:, :de], SWIGLU_LIMIT)
        u = jnp.clip(gu[:, de:], -SWIGLU_LIMIT, SWIGLU_LIMIT)
        act = g * _sigmoid(SWIGLU_ALPHA * g) * (u + 1.0)
        ybuf[slot] = jnp.dot(act.astype(BF16), wd_b[...], preferred_element_type=F32) + bd_ref[0]

    for c in y_copies(w, slot):
        c.start()

    @pl.when(w == n_blocks - 1)
    def _():
        for c in y_copies(w, slot):
            c.wait()

        @pl.when(w >= 1)
        def _():
            for c in y_copies(w - 1, 1 - slot):
                c.wait()


def _experts(xs, work, layer, w_gu, b_gu, w_down, b_down):
    _, n_exp, d, de2 = w_gu.shape
    de = w_down.shape[2]
    n_blocks = work[0].shape[0]
    bias = lambda w, ex, used, nxt: (ex[w], 0, 0)
    grid_spec = pltpu.PrefetchScalarGridSpec(
        num_scalar_prefetch=3,
        grid=(n_blocks,),
        in_specs=[
            pl.BlockSpec(memory_space=pl.ANY),
            pl.BlockSpec(memory_space=pl.ANY),
            pl.BlockSpec((1, 1, de2), bias),
            pl.BlockSpec(memory_space=pl.ANY),
            pl.BlockSpec((1, 1, d), bias),
        ],
        out_specs=pl.BlockSpec(memory_space=pl.ANY),
        scratch_shapes=[pltpu.VMEM((2, TE, d), F32), pltpu.VMEM((2, TE, d), F32),
                        pltpu.VMEM((d, de2), F32), pltpu.VMEM((de, d), F32),
                        pltpu.VMEM((d, de2), BF16), pltpu.VMEM((de, d), BF16),
                        pltpu.SemaphoreType.DMA((2,)), pltpu.SemaphoreType.DMA((2,)),
                        pltpu.SemaphoreType.DMA((2,))],
    )
    return pl.pallas_call(
        functools.partial(_expert_kernel, layer),
        grid_spec=grid_spec,
        out_shape=jax.ShapeDtypeStruct(xs.shape, F32),
        compiler_params=_cparams(("arbitrary",)),
        name="moe_experts",
    )(*work, xs, w_gu, b_gu[layer].reshape(n_exp, 1, de2), w_down, b_down[layer].reshape(n_exp, 1, d))


def _combine_kernel(row_of_tile, dest_ref, next_ref, gate_ref, x_ref, mod_ref, yb_hbm, o_ref, buf, sem):
    i = pl.program_id(0)
    n_tiles = pl.num_programs(0)
    tm, d = x_ref.shape
    half = i % 2

    def fetch(idx_ref, h):
        def start(t, c):
            slot = pl.ds(pl.multiple_of(t * SUBLANES, SUBLANES), SUBLANES)
            for k in range(TOP_K):
                row = pl.ds(pl.multiple_of(idx_ref[k * tm + t] * SUBLANES, SUBLANES), SUBLANES)
                pltpu.make_async_copy(yb_hbm.at[row], buf.at[h, k, slot], sem.at[h]).start(priority=k % 2)
            return c

        lax.fori_loop(0, tm, start, 0, unroll=ROW_DMA_UNROLL)

    @pl.when(i == 0)
    def _():
        fetch(dest_ref, 0)

    @pl.when(i + 1 < n_tiles)
    def _():
        fetch(next_ref, 1 - half)

    g2 = mod_ref[pl.ds(row_of_tile(i), 1), pl.ds(5 * d, d)]
    gates = jnp.concatenate([gate_ref[...], jnp.zeros((LANES - 8, tm), F32)], axis=0).T
    for k in range(TOP_K):
        pltpu.make_async_copy(yb_hbm.at[pl.ds(0, tm * SUBLANES)], buf.at[half, k], sem.at[half]).wait()
    y = _load_tile_rows(buf.at[half, 0], tm) * gates[:, 0:1]
    for k in range(1, TOP_K):
        y = y + _load_tile_rows(buf.at[half, k], tm) * gates[:, k:k + 1]
    o_ref[...] = x_ref[...] + g2 * y


def _combine(x_new, yb, dest, gates, mod, row_of_tile):
    t, d = x_new.shape
    last = t // TM - 1
    return pl.pallas_call(
        functools.partial(_combine_kernel, row_of_tile),
        grid=(t // TM,),
        in_specs=[
            pl.BlockSpec((TM * TOP_K,), lambda i: (i,), memory_space=pltpu.SMEM),
            pl.BlockSpec((TM * TOP_K,), lambda i: (jnp.minimum(i + 1, last),), memory_space=pltpu.SMEM),
            pl.BlockSpec((8, TM), lambda i: (0, i)),
            pl.BlockSpec((TM, d), lambda i: (i, 0)),
            _const_spec((MOD_ROWS, N_MOD * d)),
            pl.BlockSpec(memory_space=pl.ANY),
        ],
        out_specs=pl.BlockSpec((TM, d), lambda i: (i, 0)),
        out_shape=jax.ShapeDtypeStruct((t, d), F32),
        scratch_shapes=[pltpu.VMEM((2, TOP_K, TM * SUBLANES, LANES), F32),
                        pltpu.SemaphoreType.DMA((2,))],
        compiler_params=_cparams(("arbitrary",)),
        name="moe_combine",
    )(dest, dest, gates, x_new, mod, yb)


def _expert_blocks(counts, n_assign):
    n_exp = counts.shape[0]
    n_blocks = n_assign // TE + n_exp
    padded = (counts + TE - 1) // TE * TE
    pad_end = jnp.cumsum(padded)
    pad_start = (pad_end - padded).astype(jnp.int32)
    used_blocks = pad_end[-1] // TE
    w = jnp.arange(n_blocks, dtype=jnp.int32)
    used = w < used_blocks
    ex = jnp.minimum(jnp.sum(w[:, None] * TE >= pad_end[None, :], axis=1), n_exp - 1).astype(jnp.int32)
    ex = jnp.where(used, ex, jnp.max(jnp.where(used, ex, 0)))
    ids = jnp.arange(n_exp, dtype=jnp.int32)
    later = jnp.where((ids[None, :] > ids[:, None]) & (counts[None, :] > 0), ids[None, :], n_exp)
    nxt_of = jnp.min(later, axis=1)
    nxt = jnp.sum(jnp.where(ex[:, None] == ids[None, :], nxt_of[None, :], 0), axis=1)
    nxt = jnp.where(nxt < n_exp, nxt, -1).astype(jnp.int32)
    tail_blk = jnp.where(counts > 0, pad_end // TE - 1, -1).astype(jnp.int32)
    work = (ex, used.astype(jnp.int32), nxt)
    return work, pad_start, tail_blk, used_blocks.astype(jnp.int32).reshape(1), n_blocks


def _moe(x_new, f, idx, gates, rank, cnt, mod, row_of_tile, layer, w_gu, b_gu, w_down, b_down):
    t = idx.shape[1]
    work, pad_start, tail_blk, used_blocks, n_blocks = _expert_blocks(cnt[:, 0], t * TOP_K)
    ids = jnp.arange(pad_start.shape[0], dtype=jnp.int32)
    dest = rank + jnp.sum(jnp.where(idx[..., None] == ids, pad_start, 0), axis=-1)
    dest = dest.reshape(TOP_K, t // TM, TM).transpose(1, 0, 2).reshape(-1)
    xs = _dispatch(f, dest, tail_blk, used_blocks, n_blocks)
    yb = _experts(xs.reshape(-1, SUBLANES, LANES), work, layer, w_gu, b_gu, w_down, b_down)
    return _combine(x_new, yb.reshape(-1, LANES), dest, gates, mod, row_of_tile)


def kernel(x, c, ctx, c_ctx, ada_w, ada_b, norm_mix, norm_ffn, gm_w_in, gm_b_in, gm_v_gain, gm_w_s,
           gm_b_s, gm_w_out, at_w_qkv, at_q_gain, at_k_gain, at_w_o, moe_router_w, moe_router_b,
           moe_w_gu, moe_b_gu, moe_w_down, moe_b_down):
    n_samples, n_lat, d = x.shape
    n_ctx = ctx.shape[1]
    assert d == SUBLANES * LANES, "MoE row movement assumes one f32 tile per token row"
    assert n_lat % TM == 0 and n_ctx % TM == 0 and n_samples < MOD_ROWS
    assert n_lat % TG == 0 and (n_samples * n_ctx) % TG == 0 and TG % CHUNK == 0
    assert n_lat % TA == 0 and (n_samples * n_ctx) % TA == 0
    assert n_lat % TQ == 0 and (n_samples * n_ctx) % TQ == 0
    assert (n_samples * n_ctx) % n_lat == 0, "latent K/V blocks are addressed in n_lat-row blocks"
    assert (n_samples * n_lat * TOP_K) % TE == 0 and (n_samples * n_ctx * TOP_K) % TE == 0
    n_ctx_tiles = n_samples * n_ctx // TM
    tps = n_lat // TM

    mods = _ada_table(c, c_ctx, ada_w, ada_b)

    def row_all(i):
        return jnp.where(i < n_ctx_tiles, n_samples, (i - n_ctx_tiles) // tps)

    x_new, f, idx, gates, rank, cnt = _gmlp_layer(
        ctx.reshape(-1, d), x.reshape(-1, d), mods[0], n_samples * n_ctx // TG, n_lat // TG, n_samples, norm_mix[0], gm_w_in[0], gm_b_in[0],
        gm_v_gain[0], gm_w_s[0], gm_b_s[0], gm_w_out[0], norm_ffn[0], moe_router_w[0],
        moe_router_b[0])
    x_all = _moe(x_new, f, idx, gates, rank, cnt, mods[0], row_all,
                 0, moe_w_gu, moe_b_gu, moe_w_down, moe_b_down)

    q, k_all, v_all = _qkv_layer(x_all, mods[1], n_samples * n_ctx // TQ, n_lat // TQ, n_samples, n_lat,
                                 norm_mix[1], at_w_qkv[0], at_q_gain[0], at_k_gain[0])
    x_new, f, idx, gates, rank, cnt = _attn_layer(
        x_all, q, k_all, v_all, mods[1], n_samples * n_ctx // TA, n_lat // TA, n_samples, n_ctx, at_w_o[0],
        norm_ffn[1],
        moe_router_w[1], moe_router_b[1])
    out = _moe(x_new, f, idx, gates, rank, cnt, mods[1], lambda i: i // tps,
               1, moe_w_gu, moe_b_gu, moe_w_down, moe_b_down)
    return out.reshape(n_samples, n_lat, d)
```

```python
import functools

import jax
import jax.numpy as jnp
from jax import lax
from jax.experimental import pallas as pl
from jax.experimental.pallas import tpu as pltpu

F32 = jnp.float32
BF16 = jnp.bfloat16
HIGHEST = lax.Precision.HIGHEST

GRID_W = 64
N_MOD = 6
NORM_EPS = 1e-6
CHUNK = 128
GM_GROUPS = 8
HEAD_DIM = 128
N_KV_HEADS = 2
AXIS_DIM = HEAD_DIM // 2
ROPE_THETA = 10000.0
TOP_K = 4
SWIGLU_LIMIT = 7.0
SWIGLU_ALPHA = 1.702

TM = 512
TQ = 256
TG = 512
TA = 512
TE = 512
ROW_DMA_UNROLL = 8
ATTN_HEADS_PER_DOT = 2
MOD_ROWS = 16
V7X_VMEM_LIMIT = 56 * 1024 * 1024


def _cparams(sem, vmem=V7X_VMEM_LIMIT):
    return pltpu.CompilerParams(dimension_semantics=sem, vmem_limit_bytes=vmem)


def _const_spec(shape):
    nd = len(shape)
    return pl.BlockSpec(shape, lambda *_: (0,) * nd, pipeline_mode=pl.Buffered(1))


LANES = 128
SUBLANES = 8


def _load_tile_rows(ref, n):
    return jnp.concatenate([ref[pl.ds(s, n, stride=SUBLANES), :] for s in range(SUBLANES)], axis=1)


def _store_tile_rows(ref, val):
    n = val.shape[0]
    for s in range(SUBLANES):
        ref[pl.ds(s, n, stride=SUBLANES), :] = val[:, s * LANES:(s + 1) * LANES]


def _sigmoid(x):
    return 1.0 / (1.0 + jnp.exp(-x))


def _rms(x):
    return x * lax.rsqrt(jnp.mean(x * x, axis=-1, keepdims=True) + NORM_EPS)


def _ada_kernel(s_ref, w_ref, b_ref, o_ref):
    s = s_ref[...]
    s = s * _sigmoid(s)
    o_ref[0] = jnp.dot(s, w_ref[0], precision=HIGHEST, preferred_element_type=F32) + b_ref[0]


def _ada_table(c, c_ctx, ada_w, ada_b):
    depth, d, n = ada_w.shape
    b = c.shape[0]
    s = jnp.concatenate([c, c_ctx[None, :], jnp.zeros((MOD_ROWS - b - 1, d), F32)], axis=0)
    tn = 1536
    return pl.pallas_call(
        _ada_kernel,
        grid=(depth, n // tn),
        in_specs=[
            pl.BlockSpec((MOD_ROWS, d), lambda i, j: (0, 0)),
            pl.BlockSpec((1, d, tn), lambda i, j: (i, 0, j)),
            pl.BlockSpec((1, 1, tn), lambda i, j: (i, 0, j)),
        ],
        out_specs=pl.BlockSpec((1, MOD_ROWS, tn), lambda i, j: (i, 0, j)),
        out_shape=jax.ShapeDtypeStruct((depth, MOD_ROWS, n), F32),
        compiler_params=_cparams(("arbitrary", "arbitrary")),
        name="ada_table",
    )(s, ada_w, ada_b.reshape(depth, 1, n))


def _mod_slices(mod_ref, row, d, first):
    return [mod_ref[pl.ds(row, 1), pl.ds((first + k) * d, d)] for k in range(3)]


def _router_epilogue(step, x_new, sh2, sc2, nf_ref, rwt_ref, rb_ref,
                     f_ref, idx_ref, gate_ref, rank_ref, cnt_ref, base_ref):
    tm = x_new.shape[0]
    n_exp = rwt_ref.shape[0] // 2
    f = _rms(x_new) * nf_ref[...] * (1.0 + sc2) + sh2
    _store_tile_rows(f_ref, f)

    nt = (((1,), (1,)), ((), ()))
    f_hi = f.astype(BF16)
    f_lo = (f - f_hi.astype(F32)).astype(BF16)
    l_hi = lax.dot_general(rwt_ref[...], f_hi, nt, preferred_element_type=F32)
    l_lo = lax.dot_general(rwt_ref[:n_exp, :], f_lo, nt, preferred_element_type=F32)
    logits = l_hi[:n_exp, :] + l_hi[n_exp:, :] + l_lo + rb_ref[...]
    eid = lax.broadcasted_iota(jnp.int32, (n_exp, tm), 0).astype(F32)

    @pl.when(step == 0)
    def _():
        base_ref[...] = jnp.zeros_like(base_ref)

    r_io = lax.broadcasted_iota(jnp.int32, (tm, tm), 0)
    c_io = lax.broadcasted_iota(jnp.int32, (tm, tm), 1)
    before = jnp.where(r_io < c_io, 1.0, 0.0).astype(BF16)
    ones = jnp.ones((tm, tm), BF16)

    vals, idxs, hits = [], [], []
    l = logits
    for _ in range(TOP_K):
        m = jnp.max(l, axis=0, keepdims=True)
        sel = jnp.min(jnp.where(l == m, eid, float(n_exp)), axis=0, keepdims=True)
        hit = eid == sel
        l = jnp.where(hit, -jnp.inf, l)
        vals.append(m)
        idxs.append(sel)
        hits.append(hit)
    onehot = jnp.concatenate([jnp.where(h, 1.0, 0.0) for h in hits], axis=0).astype(BF16)
    prefix = jnp.dot(onehot, before, preferred_element_type=F32)
    count = jnp.dot(onehot, ones, preferred_element_type=F32)
    base = base_ref[...]
    ranks = []
    for k, hit in enumerate(hits):
        pk = prefix[k * n_exp:(k + 1) * n_exp, :]
        ranks.append(jnp.sum(jnp.where(hit, base + pk, 0.0), axis=0, keepdims=True))
        base = base + count[k * n_exp:(k + 1) * n_exp, :]
    base_ref[...] = base
    es = [jnp.exp(v - vals[0]) for v in vals]
    tot = es[0] + es[1] + es[2] + es[3]
    zero = jnp.zeros_like(tot)
    gate_ref[...] = jnp.concatenate([e / tot for e in es] + [zero] * (8 - TOP_K), axis=0)
    idx_ref[...] = jnp.concatenate(idxs, axis=0).astype(jnp.int32)
    rank_ref[...] = jnp.concatenate(ranks, axis=0).astype(jnp.int32)
    cnt_ref[...] = base_ref[:, :128].astype(jnp.int32)


def _router_weights(rw):
    hi = rw.T.astype(BF16)
    lo = (rw.T - hi.astype(F32)).astype(BF16)
    return jnp.concatenate([hi, lo], axis=0)


def _router_out(t, d, n_exp):
    shapes = (
        jax.ShapeDtypeStruct((t * SUBLANES, LANES), F32),
        jax.ShapeDtypeStruct((TOP_K, t), jnp.int32),
        jax.ShapeDtypeStruct((8, t), F32),
        jax.ShapeDtypeStruct((TOP_K, t), jnp.int32),
        jax.ShapeDtypeStruct((n_exp, 128), jnp.int32),
    )
    return shapes


def _router_out_specs(tile_of, tm, d, n_exp):
    return (
        pl.BlockSpec((tm * SUBLANES, LANES), lambda *g: (tile_of(*g), 0)),
        pl.BlockSpec((TOP_K, tm), lambda *g: (0, tile_of(*g))),
        pl.BlockSpec((8, tm), lambda *g: (0, tile_of(*g))),
        pl.BlockSpec((TOP_K, tm), lambda *g: (0, tile_of(*g))),
        pl.BlockSpec((n_exp, 128), lambda *g: (0, 0)),
    )


def _gmlp_kernel(n_ctx_tiles, tiles_per_sample, n_samples,
                 c_ref, x_ref, mod_ref, nm_ref, win_ref, bin_ref, vg_ref, ws_ref, bs_ref, wout_ref,
                 nf_ref, rwt_ref, rb_ref,
                 xo_ref, f_ref, idx_ref, gate_ref, rank_ref, cnt_ref, base_ref):
    i = pl.program_id(0)
    d = x_ref.shape[1]
    gw = wout_ref.shape[0]
    gc = gw // GM_GROUPS
    row = jnp.where(i < n_ctx_tiles, n_samples, (i - n_ctx_tiles) // tiles_per_sample)
    sh1, sc1, g1 = _mod_slices(mod_ref, row, d, 0)
    sh2, sc2, _ = _mod_slices(mod_ref, row, d, 3)

    x = jnp.where(i < n_ctx_tiles, c_ref[...], x_ref[...])
    h = _rms(x) * nm_ref[...] * (1.0 + sc1) + sh1
    hb = h.astype(BF16)

    def proj(col):
        a = jnp.dot(hb, win_ref[:, col:col + gc], preferred_element_type=F32) + bin_ref[:, col:col + gc]
        return 0.5 * a * (1.0 + lax.erf(a * (2.0 ** -0.5)))

    u = [proj(g * gc) for g in range(GM_GROUPS)]
    v = [proj(gw + g * gc) for g in range(GM_GROUPS)]
    ssq = v[0] * v[0]
    for g in range(1, GM_GROUPS):
        ssq = ssq + v[g] * v[g]
    inv = lax.rsqrt(jnp.sum(ssq, axis=-1, keepdims=True) * (1.0 / gw) + NORM_EPS)
    cols = []
    for g in range(GM_GROUPS):
        vn = (v[g] * inv * vg_ref[:, g * gc:(g + 1) * gc]).astype(BF16)
        s = jnp.concatenate(
            [jnp.dot(ws_ref[g], vn[c * CHUNK:(c + 1) * CHUNK, :], preferred_element_type=F32)
             + bs_ref[:, g * gc:(g + 1) * gc] for c in range(x.shape[0] // CHUNK)], axis=0)
        cols.append((u[g] * s).astype(BF16))
    z = jnp.concatenate(cols, axis=1)
    y = jnp.dot(z, wout_ref[...], preferred_element_type=F32)
    x_new = x + g1 * y
    xo_ref[...] = x_new
    _router_epilogue(i, x_new, sh2, sc2, nf_ref, rwt_ref, rb_ref,
                     f_ref, idx_ref, gate_ref, rank_ref, cnt_ref, base_ref)


def _gmlp_layer(ctx2d, x2d, mod, n_ctx_tiles, tiles_per_sample, n_samples,
                nm, w_in, b_in, v_gain, w_s, b_s, w_out, nf, rw, rb):
    d = x2d.shape[1]
    t = ctx2d.shape[0] + x2d.shape[0]
    gw = w_out.shape[0]
    n_exp = rw.shape[1]
    gc = gw // GM_GROUPS
    bs_full = jnp.repeat(b_s.T, gc, axis=1)
    tile = lambda i: i
    outs = pl.pallas_call(
        functools.partial(_gmlp_kernel, n_ctx_tiles, tiles_per_sample, n_samples),
        grid=(t // TG,),
        in_specs=[
            pl.BlockSpec((TG, d), lambda i: (jnp.minimum(i, n_ctx_tiles - 1), 0)),
            pl.BlockSpec((TG, d), lambda i: (jnp.maximum(i - n_ctx_tiles, 0), 0)),
            _const_spec((MOD_ROWS, N_MOD * d)),
            _const_spec((1, d)),
            _const_spec((d, 2 * gw)),
            _const_spec((1, 2 * gw)),
            _const_spec((1, gw)),
            _const_spec((GM_GROUPS, CHUNK, CHUNK)),
            _const_spec((CHUNK, gw)),
            _const_spec((gw, d)),
            _const_spec((1, d)),
            _const_spec((2 * n_exp, d)),
            _const_spec((n_exp, 1)),
        ],
        out_specs=(pl.BlockSpec((TG, d), lambda i: (i, 0)),) + _router_out_specs(tile, TG, d, n_exp),
        out_shape=(jax.ShapeDtypeStruct((t, d), F32),) + _router_out(t, d, n_exp),
        scratch_shapes=[pltpu.VMEM((n_exp, TG), F32)],
        compiler_params=_cparams(("arbitrary",)),
        name="gmlp_mixer",
    )(ctx2d, x2d, mod, nm.reshape(1, d), w_in.astype(BF16), b_in.reshape(1, -1), v_gain.reshape(1, gw),
      w_s.astype(BF16), bs_full, w_out.astype(BF16), nf.reshape(1, d), _router_weights(rw), rb.reshape(n_exp, 1))
    return outs


def _qkv_kernel(n_ctx_tiles, tiles_per_sample, n_samples,
                x_ref, mod_ref, nm_ref, w_ref, qg_ref, kg_ref, cos_ref, sin_ref,
                q_ref, k_ref, v_ref):
    i = pl.program_id(0)
    d = x_ref.shape[1]
    nq = q_ref.shape[1]
    nkv = k_ref.shape[1]
    row = jnp.where(i < n_ctx_tiles, n_samples, (i - n_ctx_tiles) // tiles_per_sample)
    sh1, sc1, _ = _mod_slices(mod_ref, row, d, 0)
    h = _rms(x_ref[...]) * nm_ref[...] * (1.0 + sc1) + sh1
    qkv = jnp.dot(h.astype(BF16), w_ref[...], preferred_element_type=F32)
    cos = cos_ref[...]
    sin = sin_ref[...]
    half = AXIS_DIM // 2
    lane = lax.broadcasted_iota(jnp.int32, (x_ref.shape[0], HEAD_DIM), 1)
    first_half = (lane % AXIS_DIM) < half

    def head(xh, gain):
        xh = _rms(xh) * gain
        partner = jnp.where(first_half, pltpu.roll(xh, HEAD_DIM - half, 1), pltpu.roll(xh, half, 1))
        return xh * cos + partner * sin

    q = [head(qkv[:, j * HEAD_DIM:(j + 1) * HEAD_DIM], qg_ref[...]) for j in range(nq // HEAD_DIM)]
    k = [head(qkv[:, nq + j * HEAD_DIM:nq + (j + 1) * HEAD_DIM], kg_ref[...])
         for j in range(nkv // HEAD_DIM)]
    q_ref[...] = jnp.concatenate(q, axis=1).astype(BF16)
    k_ref[...] = jnp.concatenate(k, axis=1).astype(BF16)
    v_ref[...] = qkv[:, nq + nkv:].astype(BF16)


def _rope_tables(n_lat):
    rows = n_lat // GRID_W
    row = jnp.repeat(jnp.arange(rows, dtype=jnp.int32), GRID_W).astype(F32)
    col = jnp.tile(jnp.arange(GRID_W, dtype=jnp.int32), rows).astype(F32)
    inv_freq = 1.0 / (ROPE_THETA ** (jnp.arange(0, AXIS_DIM, 2, dtype=F32) / AXIS_DIM))
    ang_r = row[:, None] * inv_freq
    ang_c = col[:, None] * inv_freq
    cos = jnp.concatenate([jnp.cos(ang_r)] * 2 + [jnp.cos(ang_c)] * 2, axis=1)
    sin = jnp.concatenate([-jnp.sin(ang_r), jnp.sin(ang_r), -jnp.sin(ang_c), jnp.sin(ang_c)], axis=1)
    cos = jnp.concatenate([jnp.ones((TQ, HEAD_DIM), F32), cos], axis=0)
    sin = jnp.concatenate([jnp.zeros((TQ, HEAD_DIM), F32), sin], axis=0)
    return cos, sin


def _qkv_layer(x_all, mod, n_ctx_tiles, tiles_per_sample, n_samples, n_lat,
               nm, w_qkv, q_gain, k_gain):
    t, d = x_all.shape
    nqkv = w_qkv.shape[1]
    nkv = N_KV_HEADS * HEAD_DIM
    nq = nqkv - 2 * nkv
    cos, sin = _rope_tables(n_lat)
    tps = tiles_per_sample

    def pos_block(i):
        return jnp.where(i < n_ctx_tiles, 0, 1 + (i - n_ctx_tiles) % tps)

    return pl.pallas_call(
        functools.partial(_qkv_kernel, n_ctx_tiles, tiles_per_sample, n_samples),
        grid=(t // TQ,),
        in_specs=[
            pl.BlockSpec((TQ, d), lambda i: (i, 0)),
            _const_spec((MOD_ROWS, N_MOD * d)),
            _const_spec((1, d)),
            _const_spec((d, nqkv)),
            _const_spec((1, HEAD_DIM)),
            _const_spec((1, HEAD_DIM)),
            pl.BlockSpec((TQ, HEAD_DIM), lambda i: (pos_block(i), 0)),
            pl.BlockSpec((TQ, HEAD_DIM), lambda i: (pos_block(i), 0)),
        ],
        out_specs=(
            pl.BlockSpec((TQ, nq), lambda i: (i, 0)),
            pl.BlockSpec((TQ, nkv), lambda i: (i, 0)),
            pl.BlockSpec((TQ, nkv), lambda i: (i, 0)),
        ),
        out_shape=(
            jax.ShapeDtypeStruct((t, nq), BF16),
            jax.ShapeDtypeStruct((t, nkv), BF16),
            jax.ShapeDtypeStruct((t, nkv), BF16),
        ),
        compiler_params=_cparams(("arbitrary",)),
        name="qkv_rope",
    )(x_all, mod, nm.reshape(1, d), w_qkv.astype(BF16), q_gain.reshape(1, HEAD_DIM),
      k_gain.reshape(1, HEAD_DIM), cos, sin)


def _attn_kernel(tiles_per_sample,
                 x_ref, q_ref, kc_ref, kl_ref, vc_ref, vl_ref, mod_ref, wo_ref, nf_ref, rwt_ref, rb_ref,
                 xo_ref, f_ref, idx_ref, gate_ref, rank_ref, cnt_ref, base_ref):
    b = pl.program_id(0)
    j = pl.program_id(1)
    d = x_ref.shape[1]
    n_heads = q_ref.shape[1] // HEAD_DIM
    group = n_heads // N_KV_HEADS
    _, _, g1 = _mod_slices(mod_ref, b, d, 0)
    sh2, sc2, _ = _mod_slices(mod_ref, b, d, 3)
    tq = q_ref.shape[0]
    exp2_scale = (HEAD_DIM ** -0.5) * 1.4426950408889634
    outs = []
    hpd = ATTN_HEADS_PER_DOT
    for h0 in range(0, n_heads, hpd):
        g = h0 // group
        qg = jnp.concatenate([q_ref[:, h * HEAD_DIM:(h + 1) * HEAD_DIM]
                              for h in range(h0, h0 + hpd)], axis=0)
        cols = slice(g * HEAD_DIM, (g + 1) * HEAD_DIM)
        kg = jnp.concatenate([kc_ref[:, cols], kl_ref[:, cols]], axis=0)
        vg = jnp.concatenate([vc_ref[:, cols], vl_ref[:, cols]], axis=0)
        s = lax.dot_general(qg, kg, (((1,), (1,)), ((), ())), preferred_element_type=F32)
        p = jnp.exp2((s - jnp.max(s, axis=-1, keepdims=True)) * exp2_scale)
        den = jnp.sum(p, axis=-1, keepdims=True)
        pb = p.astype(BF16)
        for h in range(hpd):
            rows = slice(h * tq, (h + 1) * tq)
            o = jnp.dot(pb[rows, :], vg, preferred_element_type=F32) / den[rows, :]
            outs.append(o.astype(BF16))
    o_all = jnp.concatenate(outs, axis=1)
    y = jnp.dot(o_all, wo_ref[...], preferred_element_type=F32)
    x_new = x_ref[...] + g1 * y
    xo_ref[...] = x_new
    _router_epilogue(b * tiles_per_sample + j, x_new, sh2, sc2, nf_ref, rwt_ref, rb_ref,
                     f_ref, idx_ref, gate_ref, rank_ref, cnt_ref, base_ref)


def _attn_layer(x_all, q, k_all, v_all, mod, n_ctx_tiles, tiles_per_sample, n_samples, n_ctx,
                w_o, nf, rw, rb):
    d = x_all.shape[1]
    t_lat = n_samples * tiles_per_sample * TA
    n_exp = rw.shape[1]
    nq = q.shape[1]
    nkv = k_all.shape[1]
    n_lat = tiles_per_sample * TA
    lat_block0 = n_samples * n_ctx // n_lat
    ctx_kv = pl.BlockSpec((n_ctx, nkv), lambda b, j: (b, 0))
    lat_kv = pl.BlockSpec((n_lat, nkv), lambda b, j: (lat_block0 + b, 0))
    tps = tiles_per_sample
    lat_tile = lambda b, j: n_ctx_tiles + b * tps + j
    tile = lambda b, j: b * tps + j
    return pl.pallas_call(
        functools.partial(_attn_kernel, tiles_per_sample),
        grid=(n_samples, tps),
        in_specs=[
            pl.BlockSpec((TA, d), lambda b, j: (lat_tile(b, j), 0)),
            pl.BlockSpec((TA, nq), lambda b, j: (lat_tile(b, j), 0)),
            ctx_kv, lat_kv, ctx_kv, lat_kv,
            _const_spec((MOD_ROWS, N_MOD * d)),
            _const_spec((nq, d)),
            _const_spec((1, d)),
            _const_spec((2 * n_exp, d)),
            _const_spec((n_exp, 1)),
        ],
        out_specs=(pl.BlockSpec((TA, d), lambda b, j: (tile(b, j), 0)),)
        + _router_out_specs(tile, TA, d, n_exp),
        out_shape=(jax.ShapeDtypeStruct((t_lat, d), F32),) + _router_out(t_lat, d, n_exp),
        scratch_shapes=[pltpu.VMEM((n_exp, TA), F32)],
        compiler_params=_cparams(("arbitrary", "arbitrary")),
        name="attn_mixer",
    )(x_all, q, k_all, k_all, v_all, v_all, mod, w_o.astype(BF16), nf.reshape(1, d), _router_weights(rw),
      rb.reshape(n_exp, 1))


def _dispatch_kernel(tail_ref, used_ref, dest_ref, f_hbm, xs_hbm, zeros, fbuf, sem, zsem, fsem):
    i = pl.program_id(0)
    n_tiles = pl.num_programs(0)
    tm = dest_ref.shape[0] // TOP_K
    tile_rows = tm * SUBLANES
    n_exp = tail_ref.shape[0]
    block_rows = zeros.shape[0]
    n_blocks = xs_hbm.shape[0] // block_rows

    def load(tile):
        src = f_hbm.at[pl.ds(pl.multiple_of(tile * tile_rows, tile_rows), tile_rows)]
        return pltpu.make_async_copy(src, fbuf.at[tile % 3], fsem.at[tile % 3])

    def drain(tile):
        for k in range(TOP_K):
            pltpu.make_async_copy(fbuf.at[0], xs_hbm.at[pl.ds(0, tile_rows)], sem.at[tile % 2]).wait()

    def fill(blk, fsem_):
        start = pl.multiple_of(blk * block_rows, block_rows)
        return pltpu.make_async_copy(zeros, xs_hbm.at[pl.ds(start, block_rows)], fsem_)

    def tails(fn):
        def body(e, c):
            @pl.when(tail_ref[e] >= 0)
            def _():
                fn(fill(tail_ref[e], zsem.at[0]))
            return c

        lax.fori_loop(0, n_exp, body, 0)

    def spares(fn):
        def body(j, c):
            @pl.when(used_ref[0] + j < n_blocks)
            def _():
                fn(fill(used_ref[0] + j, zsem.at[1]))
            return c

        lax.fori_loop(0, n_exp, body, 0)

    @pl.when(i == 0)
    def _():
        load(0).start()
        zeros[...] = jnp.zeros_like(zeros)
        tails(lambda c: c.start())
        spares(lambda c: c.start())
        tails(lambda c: c.wait())

    @pl.when(i + 1 < n_tiles)
    def _():
        load(i + 1).start()

    load(i).wait()
    tile = fbuf.at[i % 3]

    def start(t, c):
        src = tile.at[pl.ds(pl.multiple_of(t * SUBLANES, SUBLANES), SUBLANES)]
        for k in range(TOP_K):
            row = xs_hbm.at[pl.ds(pl.multiple_of(dest_ref[k * tm + t] * SUBLANES, SUBLANES), SUBLANES)]
            pltpu.make_async_copy(src, row, sem.at[i % 2]).start(priority=k % 2)
        return c

    lax.fori_loop(0, tm, start, 0, unroll=ROW_DMA_UNROLL)

    @pl.when(i >= 1)
    def _():
        drain(i - 1)

    @pl.when(i == n_tiles - 1)
    def _():
        drain(i)
        spares(lambda c: c.wait())


def _dispatch(f, dest, tail_blk, used_blocks, n_blocks):
    t = dest.shape[0] // TOP_K
    grid_spec = pltpu.PrefetchScalarGridSpec(
        num_scalar_prefetch=2,
        grid=(t // TM,),
        in_specs=[pl.BlockSpec((TM * TOP_K,), lambda i, *_: (i,), memory_space=pltpu.SMEM),
                  pl.BlockSpec(memory_space=pl.ANY)],
        out_specs=pl.BlockSpec(memory_space=pl.ANY),
        scratch_shapes=[pltpu.VMEM((TE * SUBLANES, LANES), F32),
                        pltpu.VMEM((3, TM * SUBLANES, LANES), F32),
                        pltpu.SemaphoreType.DMA((2,)), pltpu.SemaphoreType.DMA((2,)),
                        pltpu.SemaphoreType.DMA((3,))],
    )
    return pl.pallas_call(
        _dispatch_kernel,
        grid_spec=grid_spec,
        out_shape=jax.ShapeDtypeStruct((n_blocks * TE * SUBLANES, LANES), F32),
        compiler_params=_cparams(("arbitrary",)),
        name="moe_dispatch",
    )(tail_blk, used_blocks, dest, f)


def _expert_kernel(layer, exp_ref, used_ref, nxt_ref,
                   xs_hbm, wgu_hbm, bgu_ref, wd_hbm, bd_ref, yb_hbm,
                   xbuf, ybuf, gu_stage, d_stage, wgu_b, wd_b, wsem, xsem, ysem):
    w = pl.program_id(0)
    n_blocks = pl.num_programs(0)
    te = xbuf.shape[1]
    de = wd_b.shape[0]
    slot = w % 2
    new_expert = (w == 0) | (exp_ref[w] != exp_ref[jnp.maximum(w - 1, 0)])

    def x_copies(blk, sl):
        rows = pl.ds(pl.multiple_of(blk * te, te), te)
        return [pltpu.make_async_copy(xs_hbm.at[rows, s], xbuf.at[sl, :, pl.ds(s * LANES, LANES)],
                                      xsem.at[sl]) for s in range(SUBLANES)]

    def y_copies(blk, sl):
        rows = pl.ds(pl.multiple_of(blk * te, te), te)
        return [pltpu.make_async_copy(ybuf.at[sl, :, pl.ds(s * LANES, LANES)], yb_hbm.at[rows, s],
                                      ysem.at[sl]) for s in range(SUBLANES)]

    def weight_copies(e):
        return (pltpu.make_async_copy(wgu_hbm.at[layer, e], gu_stage, wsem.at[0]),
                pltpu.make_async_copy(wd_hbm.at[layer, e], d_stage, wsem.at[1]))

    @pl.when(w == 0)
    def _():
        for c in x_copies(0, 0):
            c.start()
        for c in weight_copies(exp_ref[0]):
            c.start()

    @pl.when(w + 1 < n_blocks)
    def _():
        for c in x_copies(w + 1, 1 - slot):
            c.start()

    @pl.when(new_expert)
    def _():
        for c in weight_copies(exp_ref[w]):
            c.wait()
        wgu_b[...] = gu_stage[...].astype(BF16)
        wd_b[...] = d_stage[...].astype(BF16)

        @pl.when(nxt_ref[w] >= 0)
        def _():
            for c in weight_copies(nxt_ref[w]):
                c.start()

    for c in x_copies(w, slot):
        c.wait()

    @pl.when(w >= 2)
    def _():
        for c in y_copies(w - 2, slot):
            c.wait()

    @pl.when(used_ref[w] == 0)
    def _():
        ybuf[slot] = jnp.zeros(ybuf.shape[1:], F32)

    @pl.when(used_ref[w] != 0)
    def _():
        x = xbuf[slot].astype(BF16)
        gu = jnp.dot(x, wgu_b[...], preferred_element_type=F32) + bgu_ref[0]
        g = jnp.minimum(gu[:, :de], SWIGLU_LIMIT)
        u = jnp.clip(gu[:, de:], -SWIGLU_LIMIT, SWIGLU_LIMIT)
        act = g * _sigmoid(SWIGLU_ALPHA * g) * (u + 1.0)
        ybuf[slot] = jnp.dot(act.astype(BF16), wd_b[...], preferred_element_type=F32) + bd_ref[0]

    for c in y_copies(w, slot):
        c.start()

    @pl.when(w == n_blocks - 1)
    def _():
        for c in y_copies(w, slot):
            c.wait()

        @pl.when(w >= 1)
        def _():
            for c in y_copies(w - 1, 1 - slot):
                c.wait()


def _experts(xs, work, layer, w_gu, b_gu, w_down, b_down):
    _, n_exp, d, de2 = w_gu.shape
    de = w_down.shape[2]
    n_blocks = work[0].shape[0]
    bias = lambda w, ex, used, nxt: (ex[w], 0, 0)
    grid_spec = pltpu.PrefetchScalarGridSpec(
        num_scalar_prefetch=3,
        grid=(n_blocks,),
        in_specs=[
            pl.BlockSpec(memory_space=pl.ANY),
            pl.BlockSpec(memory_space=pl.ANY),
            pl.BlockSpec((1, 1, de2), bias),
            pl.BlockSpec(memory_space=pl.ANY),
            pl.BlockSpec((1, 1, d), bias),
        ],
        out_specs=pl.BlockSpec(memory_space=pl.ANY),
        scratch_shapes=[pltpu.VMEM((2, TE, d), F32), pltpu.VMEM((2, TE, d), F32),
                        pltpu.VMEM((d, de2), F32), pltpu.VMEM((de, d), F32),
                        pltpu.VMEM((d, de2), BF16), pltpu.VMEM((de, d), BF16),
                        pltpu.SemaphoreType.DMA((2,)), pltpu.SemaphoreType.DMA((2,)),
                        pltpu.SemaphoreType.DMA((2,))],
    )
    return pl.pallas_call(
        functools.partial(_expert_kernel, layer),
        grid_spec=grid_spec,
        out_shape=jax.ShapeDtypeStruct(xs.shape, F32),
        compiler_params=_cparams(("arbitrary",)),
        name="moe_experts",
    )(*work, xs, w_gu, b_gu[layer].reshape(n_exp, 1, de2), w_down, b_down[layer].reshape(n_exp, 1, d))


def _combine_kernel(row_of_tile, dest_ref, next_ref, gate_ref, x_ref, mod_ref, yb_hbm, o_ref, buf, sem):
    i = pl.program_id(0)
    n_tiles = pl.num_programs(0)
    tm, d = x_ref.shape
    half = i % 2

    def fetch(idx_ref, h):
        def start(t, c):
            slot = pl.ds(pl.multiple_of(t * SUBLANES, SUBLANES), SUBLANES)
            for k in range(TOP_K):
                row = pl.ds(pl.multiple_of(idx_ref[k * tm + t] * SUBLANES, SUBLANES), SUBLANES)
                pltpu.make_async_copy(yb_hbm.at[row], buf.at[h, k, slot], sem.at[h]).start(priority=k % 2)
            return c

        lax.fori_loop(0, tm, start, 0, unroll=ROW_DMA_UNROLL)

    @pl.when(i == 0)
    def _():
        fetch(dest_ref, 0)

    @pl.when(i + 1 < n_tiles)
    def _():
        fetch(next_ref, 1 - half)

    g2 = mod_ref[pl.ds(row_of_tile(i), 1), pl.ds(5 * d, d)]
    gates = jnp.concatenate([gate_ref[...], jnp.zeros((LANES - 8, tm), F32)], axis=0).T
    for k in range(TOP_K):
        pltpu.make_async_copy(yb_hbm.at[pl.ds(0, tm * SUBLANES)], buf.at[half, k], sem.at[half]).wait()
    y = _load_tile_rows(buf.at[half, 0], tm) * gates[:, 0:1]
    for k in range(1, TOP_K):
        y = y + _load_tile_rows(buf.at[half, k], tm) * gates[:, k:k + 1]
    o_ref[...] = x_ref[...] + g2 * y


def _combine(x_new, yb, dest, gates, mod, row_of_tile):
    t, d = x_new.shape
    last = t // TM - 1
    return pl.pallas_call(
        functools.partial(_combine_kernel, row_of_tile),
        grid=(t // TM,),
        in_specs=[
            pl.BlockSpec((TM * TOP_K,), lambda i: (i,), memory_space=pltpu.SMEM),
            pl.BlockSpec((TM * TOP_K,), lambda i: (jnp.minimum(i + 1, last),), memory_space=pltpu.SMEM),
            pl.BlockSpec((8, TM), lambda i: (0, i)),
            pl.BlockSpec((TM, d), lambda i: (i, 0)),
            _const_spec((MOD_ROWS, N_MOD * d)),
            pl.BlockSpec(memory_space=pl.ANY),
        ],
        out_specs=pl.BlockSpec((TM, d), lambda i: (i, 0)),
        out_shape=jax.ShapeDtypeStruct((t, d), F32),
        scratch_shapes=[pltpu.VMEM((2, TOP_K, TM * SUBLANES, LANES), F32),
                        pltpu.SemaphoreType.DMA((2,))],
        compiler_params=_cparams(("arbitrary",)),
        name="moe_combine",
    )(dest, dest, gates, x_new, mod, yb)


def _expert_blocks(counts, n_assign):
    n_exp = counts.shape[0]
    n_blocks = n_assign // TE + n_exp
    padded = (counts + TE - 1) // TE * TE
    pad_end = jnp.cumsum(padded)
    pad_start = (pad_end - padded).astype(jnp.int32)
    used_blocks = pad_end[-1] // TE
    w = jnp.arange(n_blocks, dtype=jnp.int32)
    used = w < used_blocks
    ex = jnp.minimum(jnp.sum(w[:, None] * TE >= pad_end[None, :], axis=1), n_exp - 1).astype(jnp.int32)
    ex = jnp.where(used, ex, jnp.max(jnp.where(used, ex, 0)))
    ids = jnp.arange(n_exp, dtype=jnp.int32)
    later = jnp.where((ids[None, :] > ids[:, None]) & (counts[None, :] > 0), ids[None, :], n_exp)
    nxt_of = jnp.min(later, axis=1)
    nxt = jnp.sum(jnp.where(ex[:, None] == ids[None, :], nxt_of[None, :], 0), axis=1)
    nxt = jnp.where(nxt < n_exp, nxt, -1).astype(jnp.int32)
    tail_blk = jnp.where(counts > 0, pad_end // TE - 1, -1).astype(jnp.int32)
    work = (ex, used.astype(jnp.int32), nxt)
    return work, pad_start, tail_blk, used_blocks.astype(jnp.int32).reshape(1), n_blocks


def _moe(x_new, f, idx, gates, rank, cnt, mod, row_of_tile, layer, w_gu, b_gu, w_down, b_down):
    t = idx.shape[1]
    work, pad_start, tail_blk, used_blocks, n_blocks = _expert_blocks(cnt[:, 0], t * TOP_K)
    ids = jnp.arange(pad_start.shape[0], dtype=jnp.int32)
    dest = rank + jnp.sum(jnp.where(idx[..., None] == ids, pad_start, 0), axis=-1)
    dest = dest.reshape(TOP_K, t // TM, TM).transpose(1, 0, 2).reshape(-1)
    xs = _dispatch(f, dest, tail_blk, used_blocks, n_blocks)
    yb = _experts(xs.reshape(-1, SUBLANES, LANES), work, layer, w_gu, b_gu, w_down, b_down)
    return _combine(x_new, yb.reshape(-1, LANES), dest, gates, mod, row_of_tile)


def kernel(x, c, ctx, c_ctx, ada_w, ada_b, norm_mix, norm_ffn, gm_w_in, gm_b_in, gm_v_gain, gm_w_s,
           gm_b_s, gm_w_out, at_w_qkv, at_q_gain, at_k_gain, at_w_o, moe_router_w, moe_router_b,
           moe_w_gu, moe_b_gu, moe_w_down, moe_b_down):
    n_samples, n_lat, d = x.shape
    n_ctx = ctx.shape[1]
    assert d == SUBLANES * LANES, "MoE row movement assumes one f32 tile per token row"
    assert n_lat % TM == 0 and (n_samples * n_ctx) % TM == 0 and n_samples < MOD_ROWS
    assert n_lat % TG == 0 and (n_samples * n_ctx) % TG == 0 and TG % CHUNK == 0
    assert n_lat % TA == 0 and (n_samples * n_ctx) % TA == 0
    assert n_lat % TQ == 0 and (n_samples * n_ctx) % TQ == 0
    assert (n_samples * n_ctx) % n_lat == 0, "latent K/V blocks are addressed in n_lat-row blocks"
    assert (n_samples * n_lat * TOP_K) % TE == 0 and (n_samples * n_ctx * TOP_K) % TE == 0
    n_ctx_tiles = n_samples * n_ctx // TM
    tps = n_lat // TM

    mods = _ada_table(c, c_ctx, ada_w, ada_b)

    def row_all(i):
        return jnp.where(i < n_ctx_tiles, n_samples, (i - n_ctx_tiles) // tps)

    x_new, f, idx, gates, rank, cnt = _gmlp_layer(
        ctx.reshape(-1, d), x.reshape(-1, d), mods[0], n_samples * n_ctx // TG, n_lat // TG, n_samples, norm_mix[0], gm_w_in[0], gm_b_in[0],
        gm_v_gain[0], gm_w_s[0], gm_b_s[0], gm_w_out[0], norm_ffn[0], moe_router_w[0],
        moe_router_b[0])
    x_all = _moe(x_new, f, idx, gates, rank, cnt, mods[0], row_all,
                 0, moe_w_gu, moe_b_gu, moe_w_down, moe_b_down)

    q, k_all, v_all = _qkv_layer(x_all, mods[1], n_samples * n_ctx // TQ, n_lat // TQ, n_samples, n_lat,
                                 norm_mix[1], at_w_qkv[0], at_q_gain[0], at_k_gain[0])
    x_new, f, idx, gates, rank, cnt = _attn_layer(
        x_all, q, k_all, v_all, mods[1], n_samples * n_ctx // TA, n_lat // TA, n_samples, n_ctx, at_w_o[0],
        norm_ffn[1],
        moe_router_w[1], moe_router_b[1])
    out = _moe(x_new, f, idx, gates, rank, cnt, mods[1], lambda i: i // tps,
               1, moe_w_gu, moe_b_gu, moe_w_down, moe_b_down)
    return out.reshape(n_samples, n_lat, d)
```

```python
import functools

import jax
import jax.numpy as jnp
from jax import lax
from jax.experimental import pallas as pl
from jax.experimental.pallas import tpu as pltpu

F32 = jnp.float32
BF16 = jnp.bfloat16
HIGHEST = lax.Precision.HIGHEST

GRID_W = 64
N_MOD = 6
NORM_EPS = 1e-6
CHUNK = 128
GM_GROUPS = 8
HEAD_DIM = 128
N_KV_HEADS = 2
AXIS_DIM = HEAD_DIM // 2
ROPE_THETA = 10000.0
TOP_K = 4
SWIGLU_LIMIT = 7.0
SWIGLU_ALPHA = 1.702

LANES = 128
SUBLANES = 8
V7X_VMEM_LIMIT = 56 * 1024 * 1024
LOG2_E = 1.4426950408889634

TM = 256
TQ = 256
TG = 512
TA = 512
TE = 512
ROW_DMA_UNROLL = 8
ATTN_HEADS_PER_DOT = 2
MOD_ROWS = 16
ADA_COLS = 1536


def _cparams(sem, vmem=V7X_VMEM_LIMIT):
    return pltpu.CompilerParams(dimension_semantics=sem, vmem_limit_bytes=vmem)


def _const_spec(shape):
    nd = len(shape)
    return pl.BlockSpec(shape, lambda *_: (0,) * nd, pipeline_mode=pl.Buffered(1))


def _load_tile_rows(ref, n):
    return jnp.concatenate([ref[pl.ds(s, n, stride=SUBLANES), :] for s in range(SUBLANES)], axis=1)


def _sigmoid(x):
    return 1.0 / (1.0 + jnp.exp(-x))


def _rms(x):
    return x * lax.rsqrt(jnp.mean(x * x, axis=-1, keepdims=True) + NORM_EPS)


def _ada_kernel(s_ref, w_ref, b_ref, o_ref):
    s = s_ref[...]
    s = s * _sigmoid(s)
    o_ref[0] = jnp.dot(s, w_ref[0], precision=HIGHEST, preferred_element_type=F32) + b_ref[0]


def _ada_table(c, c_ctx, ada_w, ada_b):
    depth, d, n = ada_w.shape
    b = c.shape[0]
    s = jnp.concatenate([c, c_ctx[None, :], jnp.zeros((MOD_ROWS - b - 1, d), F32)], axis=0)
    tn = ADA_COLS
    return pl.pallas_call(
        _ada_kernel,
        grid=(depth, n // tn),
        in_specs=[
            pl.BlockSpec((MOD_ROWS, d), lambda i, j: (0, 0)),
            pl.BlockSpec((1, d, tn), lambda i, j: (i, 0, j)),
            pl.BlockSpec((1, 1, tn), lambda i, j: (i, 0, j)),
        ],
        out_specs=pl.BlockSpec((1, MOD_ROWS, tn), lambda i, j: (i, 0, j)),
        out_shape=jax.ShapeDtypeStruct((depth, MOD_ROWS, n), F32),
        compiler_params=_cparams(("arbitrary", "arbitrary")),
        name="ada_table",
    )(s, ada_w, ada_b.reshape(depth, 1, n))


def _mod_slices(mod_ref, row, d, first):
    return [mod_ref[pl.ds(row, 1), pl.ds((first + k) * d, d)] for k in range(3)]


def _router_epilogue(step, x_new, sh2, sc2, nf_ref, rwt_ref, rb_ref,
                     f_ref, idx_ref, gate_ref, rank_ref, cnt_ref, base_ref):
    tm = x_new.shape[0]
    n_exp = rwt_ref.shape[0] // 2
    f = _rms(x_new) * nf_ref[...] * (1.0 + sc2) + sh2
    f_ref[...] = f

    nt = (((1,), (1,)), ((), ()))
    f_hi = f.astype(BF16)
    f_lo = (f - f_hi.astype(F32)).astype(BF16)
    l_hi = lax.dot_general(rwt_ref[...], f_hi, nt, preferred_element_type=F32)
    l_lo = lax.dot_general(rwt_ref[:n_exp, :], f_lo, nt, preferred_element_type=F32)
    logits = l_hi[:n_exp, :] + l_hi[n_exp:, :] + l_lo + rb_ref[...]
    eid = lax.broadcasted_iota(jnp.int32, (n_exp, tm), 0).astype(F32)

    @pl.when(step == 0)
    def _():
        base_ref[...] = jnp.zeros_like(base_ref)

    r_io = lax.broadcasted_iota(jnp.int32, (tm, tm), 0)
    c_io = lax.broadcasted_iota(jnp.int32, (tm, tm), 1)
    before = jnp.where(r_io < c_io, 1.0, 0.0).astype(BF16)
    ones = jnp.ones((tm, tm), BF16)

    vals, idxs, hits = [], [], []
    l = logits
    for _ in range(TOP_K):
        m = jnp.max(l, axis=0, keepdims=True)
        sel = jnp.min(jnp.where(l == m, eid, float(n_exp)), axis=0, keepdims=True)
        hit = eid == sel
        l = jnp.where(hit, -jnp.inf, l)
        vals.append(m)
        idxs.append(sel)
        hits.append(hit)
    onehot = jnp.concatenate([jnp.where(h, 1.0, 0.0) for h in hits], axis=0).astype(BF16)
    prefix = jnp.dot(onehot, before, preferred_element_type=F32)
    count = jnp.dot(onehot, ones, preferred_element_type=F32)
    base = base_ref[...]
    ranks = []
    for k, hit in enumerate(hits):
        pk = prefix[k * n_exp:(k + 1) * n_exp, :]
        ranks.append(jnp.sum(jnp.where(hit, base + pk, 0.0), axis=0, keepdims=True))
        base = base + count[k * n_exp:(k + 1) * n_exp, :]
    base_ref[...] = base
    es = [jnp.exp(v - vals[0]) for v in vals]
    tot = es[0] + es[1] + es[2] + es[3]
    zero = jnp.zeros_like(tot)
    gate_ref[...] = jnp.concatenate([e / tot for e in es] + [zero] * (SUBLANES - TOP_K), axis=0)
    idx_ref[...] = jnp.concatenate(idxs, axis=0).astype(jnp.int32)
    rank_ref[...] = jnp.concatenate(ranks, axis=0).astype(jnp.int32)
    cnt_ref[...] = base_ref[:, :LANES].astype(jnp.int32)


def _router_weights(rw):
    hi = rw.T.astype(BF16)
    lo = (rw.T - hi.astype(F32)).astype(BF16)
    return jnp.concatenate([hi, lo], axis=0)


def _router_out(t, d, n_exp):
    shapes = (
        jax.ShapeDtypeStruct((t, d), F32),
        jax.ShapeDtypeStruct((TOP_K, t), jnp.int32),
        jax.ShapeDtypeStruct((SUBLANES, t), F32),
        jax.ShapeDtypeStruct((TOP_K, t), jnp.int32),
        jax.ShapeDtypeStruct((n_exp, LANES), jnp.int32),
    )
    return shapes


def _router_out_specs(tile_of, tm, d, n_exp):
    return (
        pl.BlockSpec((tm, d), lambda *g: (tile_of(*g), 0)),
        pl.BlockSpec((TOP_K, tm), lambda *g: (0, tile_of(*g))),
        pl.BlockSpec((SUBLANES, tm), lambda *g: (0, tile_of(*g))),
        pl.BlockSpec((TOP_K, tm), lambda *g: (0, tile_of(*g))),
        pl.BlockSpec((n_exp, LANES), lambda *g: (0, 0)),
    )


def _gmlp_kernel(n_ctx_tiles, tiles_per_sample, n_samples,
                 c_ref, x_ref, mod_ref, nm_ref, win_ref, bin_ref, vg_ref, ws_ref, bs_ref, wout_ref,
                 nf_ref, rwt_ref, rb_ref,
                 xo_ref, f_ref, idx_ref, gate_ref, rank_ref, cnt_ref, base_ref):
    i = pl.program_id(0)
    d = x_ref.shape[1]
    gw = wout_ref.shape[0]
    gc = gw // GM_GROUPS
    row = jnp.where(i < n_ctx_tiles, n_samples, (i - n_ctx_tiles) // tiles_per_sample)
    sh1, sc1, g1 = _mod_slices(mod_ref, row, d, 0)
    sh2, sc2, _ = _mod_slices(mod_ref, row, d, 3)

    x = jnp.where(i < n_ctx_tiles, c_ref[...], x_ref[...])
    h = _rms(x) * nm_ref[...] * (1.0 + sc1) + sh1
    hb = h.astype(BF16)

    def proj(col):
        a = jnp.dot(hb, win_ref[:, col:col + gc], preferred_element_type=F32) + bin_ref[:, col:col + gc]
        return 0.5 * a * (1.0 + lax.erf(a * (2.0 ** -0.5)))

    u = [proj(g * gc) for g in range(GM_GROUPS)]
    v = [proj(gw + g * gc) for g in range(GM_GROUPS)]
    ssq = v[0] * v[0]
    for g in range(1, GM_GROUPS):
        ssq = ssq + v[g] * v[g]
    inv = lax.rsqrt(jnp.sum(ssq, axis=-1, keepdims=True) * (1.0 / gw) + NORM_EPS)
    cols = []
    for g in range(GM_GROUPS):
        vn = (v[g] * inv * vg_ref[:, g * gc:(g + 1) * gc]).astype(BF16)
        s = jnp.concatenate(
            [jnp.dot(ws_ref[g], vn[c * CHUNK:(c + 1) * CHUNK, :], preferred_element_type=F32)
             + bs_ref[:, g * gc:(g + 1) * gc] for c in range(x.shape[0] // CHUNK)], axis=0)
        cols.append((u[g] * s).astype(BF16))
    z = jnp.concatenate(cols, axis=1)
    y = jnp.dot(z, wout_ref[...], preferred_element_type=F32)
    x_new = x + g1 * y
    xo_ref[...] = x_new
    _router_epilogue(i, x_new, sh2, sc2, nf_ref, rwt_ref, rb_ref,
                     f_ref, idx_ref, gate_ref, rank_ref, cnt_ref, base_ref)


def _gmlp_layer(ctx2d, x2d, mod, n_ctx_tiles, tiles_per_sample, n_samples,
                nm, w_in, b_in, v_gain, w_s, b_s, w_out, nf, rw, rb):
    d = x2d.shape[1]
    t = ctx2d.shape[0] + x2d.shape[0]
    gw = w_out.shape[0]
    n_exp = rw.shape[1]
    gc = gw // GM_GROUPS
    bs_full = jnp.repeat(b_s.T, gc, axis=1)
    tile = lambda i: i
    outs = pl.pallas_call(
        functools.partial(_gmlp_kernel, n_ctx_tiles, tiles_per_sample, n_samples),
        grid=(t // TG,),
        in_specs=[
            pl.BlockSpec((TG, d), lambda i: (jnp.minimum(i, n_ctx_tiles - 1), 0)),
            pl.BlockSpec((TG, d), lambda i: (jnp.maximum(i - n_ctx_tiles, 0), 0)),
            _const_spec((MOD_ROWS, N_MOD * d)),
            _const_spec((1, d)),
            _const_spec((d, 2 * gw)),
            _const_spec((1, 2 * gw)),
            _const_spec((1, gw)),
            _const_spec((GM_GROUPS, CHUNK, CHUNK)),
            _const_spec((CHUNK, gw)),
            _const_spec((gw, d)),
            _const_spec((1, d)),
            _const_spec((2 * n_exp, d)),
            _const_spec((n_exp, 1)),
        ],
        out_specs=(pl.BlockSpec((TG, d), lambda i: (i, 0)),) + _router_out_specs(tile, TG, d, n_exp),
        out_shape=(jax.ShapeDtypeStruct((t, d), F32),) + _router_out(t, d, n_exp),
        scratch_shapes=[pltpu.VMEM((n_exp, TG), F32)],
        compiler_params=_cparams(("arbitrary",)),
        name="gmlp_mixer",
    )(ctx2d, x2d, mod, nm.reshape(1, d), w_in.astype(BF16), b_in.reshape(1, -1), v_gain.reshape(1, gw),
      w_s.astype(BF16), bs_full, w_out.astype(BF16), nf.reshape(1, d), _router_weights(rw), rb.reshape(n_exp, 1))
    return outs


def _qkv_kernel(n_ctx_tiles, tiles_per_sample, n_samples,
                x_ref, mod_ref, nm_ref, w_ref, qg_ref, kg_ref, cos_ref, sin_ref,
                q_ref, k_ref, v_ref):
    i = pl.program_id(0)
    d = x_ref.shape[1]
    nq = q_ref.shape[1]
    nkv = k_ref.shape[1]
    row = jnp.where(i < n_ctx_tiles, n_samples, (i - n_ctx_tiles) // tiles_per_sample)
    sh1, sc1, _ = _mod_slices(mod_ref, row, d, 0)
    h = _rms(x_ref[...]) * nm_ref[...] * (1.0 + sc1) + sh1
    qkv = jnp.dot(h.astype(BF16), w_ref[...], preferred_element_type=F32)
    cos = cos_ref[...]
    sin = sin_ref[...]
    half = AXIS_DIM // 2
    lane = lax.broadcasted_iota(jnp.int32, (x_ref.shape[0], HEAD_DIM), 1)
    first_half = (lane % AXIS_DIM) < half

    def head(xh, gain):
        xh = _rms(xh) * gain
        partner = jnp.where(first_half, pltpu.roll(xh, HEAD_DIM - half, 1), pltpu.roll(xh, half, 1))
        return xh * cos + partner * sin

    q = [head(qkv[:, j * HEAD_DIM:(j + 1) * HEAD_DIM], qg_ref[...]) for j in range(nq // HEAD_DIM)]
    k = [head(qkv[:, nq + j * HEAD_DIM:nq + (j + 1) * HEAD_DIM], kg_ref[...])
         for j in range(nkv // HEAD_DIM)]
    q_ref[...] = jnp.concatenate(q, axis=1).astype(BF16)
    k_ref[...] = jnp.concatenate(k, axis=1).astype(BF16)
    v_ref[...] = qkv[:, nq + nkv:].astype(BF16)


def _rope_tables(n_lat):
    rows = n_lat // GRID_W
    row = jnp.repeat(jnp.arange(rows, dtype=jnp.int32), GRID_W).astype(F32)
    col = jnp.tile(jnp.arange(GRID_W, dtype=jnp.int32), rows).astype(F32)
    inv_freq = 1.0 / (ROPE_THETA ** (jnp.arange(0, AXIS_DIM, 2, dtype=F32) / AXIS_DIM))
    ang_r = row[:, None] * inv_freq
    ang_c = col[:, None] * inv_freq
    cos = jnp.concatenate([jnp.cos(ang_r)] * 2 + [jnp.cos(ang_c)] * 2, axis=1)
    sin = jnp.concatenate([-jnp.sin(ang_r), jnp.sin(ang_r), -jnp.sin(ang_c), jnp.sin(ang_c)], axis=1)
    cos = jnp.concatenate([jnp.ones((TQ, HEAD_DIM), F32), cos], axis=0)
    sin = jnp.concatenate([jnp.zeros((TQ, HEAD_DIM), F32), sin], axis=0)
    return cos, sin


def _qkv_layer(x_all, mod, n_ctx_tiles, tiles_per_sample, n_samples, n_lat,
               nm, w_qkv, q_gain, k_gain):
    t, d = x_all.shape
    nqkv = w_qkv.shape[1]
    nkv = N_KV_HEADS * HEAD_DIM
    nq = nqkv - 2 * nkv
    cos, sin = _rope_tables(n_lat)
    tps = tiles_per_sample

    def pos_block(i):
        return jnp.where(i < n_ctx_tiles, 0, 1 + (i - n_ctx_tiles) % tps)

    return pl.pallas_call(
        functools.partial(_qkv_kernel, n_ctx_tiles, tiles_per_sample, n_samples),
        grid=(t // TQ,),
        in_specs=[
            pl.BlockSpec((TQ, d), lambda i: (i, 0)),
            _const_spec((MOD_ROWS, N_MOD * d)),
            _const_spec((1, d)),
            _const_spec((d, nqkv)),
            _const_spec((1, HEAD_DIM)),
            _const_spec((1, HEAD_DIM)),
            pl.BlockSpec((TQ, HEAD_DIM), lambda i: (pos_block(i), 0)),
            pl.BlockSpec((TQ, HEAD_DIM), lambda i: (pos_block(i), 0)),
        ],
        out_specs=(
            pl.BlockSpec((TQ, nq), lambda i: (i, 0)),
            pl.BlockSpec((TQ, nkv), lambda i: (i, 0)),
            pl.BlockSpec((TQ, nkv), lambda i: (i, 0)),
        ),
        out_shape=(
            jax.ShapeDtypeStruct((t, nq), BF16),
            jax.ShapeDtypeStruct((t, nkv), BF16),
            jax.ShapeDtypeStruct((t, nkv), BF16),
        ),
        compiler_params=_cparams(("arbitrary",)),
        name="qkv_rope",
    )(x_all, mod, nm.reshape(1, d), w_qkv.astype(BF16), q_gain.reshape(1, HEAD_DIM),
      k_gain.reshape(1, HEAD_DIM), cos, sin)


def _attn_kernel(tiles_per_sample,
                 x_ref, q_ref, kc_ref, kl_ref, vc_ref, vl_ref, mod_ref, wo_ref, nf_ref, rwt_ref, rb_ref,
                 xo_ref, f_ref, idx_ref, gate_ref, rank_ref, cnt_ref, base_ref):
    b = pl.program_id(0)
    j = pl.program_id(1)
    d = x_ref.shape[1]
    n_heads = q_ref.shape[1] // HEAD_DIM
    group = n_heads // N_KV_HEADS
    _, _, g1 = _mod_slices(mod_ref, b, d, 0)
    sh2, sc2, _ = _mod_slices(mod_ref, b, d, 3)
    tq = q_ref.shape[0]
    exp2_scale = (HEAD_DIM ** -0.5) * LOG2_E
    outs = []
    hpd = ATTN_HEADS_PER_DOT
    for h0 in range(0, n_heads, hpd):
        g = h0 // group
        qg = jnp.concatenate([q_ref[:, h * HEAD_DIM:(h + 1) * HEAD_DIM]
                              for h in range(h0, h0 + hpd)], axis=0)
        cols = slice(g * HEAD_DIM, (g + 1) * HEAD_DIM)
        kg = jnp.concatenate([kc_ref[:, cols], kl_ref[:, cols]], axis=0)
        vg = jnp.concatenate([vc_ref[:, cols], vl_ref[:, cols]], axis=0)
        s = lax.dot_general(qg, kg, (((1,), (1,)), ((), ())), preferred_element_type=F32)
        p = jnp.exp2((s - jnp.max(s, axis=-1, keepdims=True)) * exp2_scale)
        den = jnp.sum(p, axis=-1, keepdims=True)
        pb = p.astype(BF16)
        for h in range(hpd):
            rows = slice(h * tq, (h + 1) * tq)
            o = jnp.dot(pb[rows, :], vg, preferred_element_type=F32) / den[rows, :]
            outs.append(o.astype(BF16))
    o_all = jnp.concatenate(outs, axis=1)
    y = jnp.dot(o_all, wo_ref[...], preferred_element_type=F32)
    x_new = x_ref[...] + g1 * y
    xo_ref[...] = x_new
    _router_epilogue(b * tiles_per_sample + j, x_new, sh2, sc2, nf_ref, rwt_ref, rb_ref,
                     f_ref, idx_ref, gate_ref, rank_ref, cnt_ref, base_ref)


def _attn_layer(x_all, q, k_all, v_all, mod, n_ctx_tiles, tiles_per_sample, n_samples, n_ctx,
                w_o, nf, rw, rb):
    d = x_all.shape[1]
    t_lat = n_samples * tiles_per_sample * TA
    n_exp = rw.shape[1]
    nq = q.shape[1]
    nkv = k_all.shape[1]
    n_lat = tiles_per_sample * TA
    lat_block0 = n_samples * n_ctx // n_lat
    ctx_kv = pl.BlockSpec((n_ctx, nkv), lambda b, j: (b, 0))
    lat_kv = pl.BlockSpec((n_lat, nkv), lambda b, j: (lat_block0 + b, 0))
    tps = tiles_per_sample
    lat_tile = lambda b, j: n_ctx_tiles + b * tps + j
    tile = lambda b, j: b * tps + j
    return pl.pallas_call(
        functools.partial(_attn_kernel, tiles_per_sample),
        grid=(n_samples, tps),
        in_specs=[
            pl.BlockSpec((TA, d), lambda b, j: (lat_tile(b, j), 0)),
            pl.BlockSpec((TA, nq), lambda b, j: (lat_tile(b, j), 0)),
            ctx_kv, lat_kv, ctx_kv, lat_kv,
            _const_spec((MOD_ROWS, N_MOD * d)),
            _const_spec((nq, d)),
            _const_spec((1, d)),
            _const_spec((2 * n_exp, d)),
            _const_spec((n_exp, 1)),
        ],
        out_specs=(pl.BlockSpec((TA, d), lambda b, j: (tile(b, j), 0)),)
        + _router_out_specs(tile, TA, d, n_exp),
        out_shape=(jax.ShapeDtypeStruct((t_lat, d), F32),) + _router_out(t_lat, d, n_exp),
        scratch_shapes=[pltpu.VMEM((n_exp, TA), F32)],
        compiler_params=_cparams(("arbitrary", "arbitrary")),
        name="attn_mixer",
    )(x_all, q, k_all, k_all, v_all, v_all, mod, w_o.astype(BF16), nf.reshape(1, d), _router_weights(rw),
      rb.reshape(n_exp, 1))


def _dispatch_kernel(tail_ref, used_ref, dest_ref, f_hbm, xs_hbm, zeros, fbuf, sem, zsem, fsem):
    i = pl.program_id(0)
    n_tiles = pl.num_programs(0)
    tm = dest_ref.shape[0] // TOP_K
    n_exp = tail_ref.shape[0]
    block_rows = zeros.shape[0]
    n_blocks = xs_hbm.shape[0] // block_rows

    def load(tile):
        rows = pl.ds(pl.multiple_of(tile * tm, tm), tm)
        return [pltpu.make_async_copy(f_hbm.at[rows, pl.ds(s * LANES, LANES)], fbuf.at[tile % 3, :, s, :],
                                      fsem.at[tile % 3]) for s in range(SUBLANES)]

    def drain(tile):
        for k in range(TOP_K):
            pltpu.make_async_copy(fbuf.at[0], xs_hbm.at[pl.ds(0, tm)], sem.at[tile % 2]).wait()

    def fill(blk, fsem_):
        start = pl.multiple_of(blk * block_rows, block_rows)
        return pltpu.make_async_copy(zeros, xs_hbm.at[pl.ds(start, block_rows)], fsem_)

    def tails(fn):
        def body(e, c):
            @pl.when(tail_ref[e] >= 0)
            def _():
                fn(fill(tail_ref[e], zsem.at[0]))
            return c

        lax.fori_loop(0, n_exp, body, 0)

    def spares(fn):
        def body(j, c):
            @pl.when(used_ref[0] + j < n_blocks)
            def _():
                fn(fill(used_ref[0] + j, zsem.at[1]))
            return c

        lax.fori_loop(0, n_exp, body, 0)

    @pl.when(i == 0)
    def _():
        for c in load(0):
            c.start()
        zeros[...] = jnp.zeros_like(zeros)
        tails(lambda c: c.start())
        spares(lambda c: c.start())
        tails(lambda c: c.wait())

    @pl.when(i + 1 < n_tiles)
    def _():
        for c in load(i + 1):
            c.start()

    for c in load(i):
        c.wait()
    tile = fbuf.at[i % 3]

    def start(t, c):
        for k in range(TOP_K):
            pltpu.make_async_copy(tile.at[t], xs_hbm.at[dest_ref[k * tm + t]], sem.at[i % 2]).start(priority=k % 2)
        return c

    lax.fori_loop(0, tm, start, 0, unroll=ROW_DMA_UNROLL)

    @pl.when(i >= 1)
    def _():
        drain(i - 1)

    @pl.when(i == n_tiles - 1)
    def _():
        drain(i)
        spares(lambda c: c.wait())


def _dispatch(f, dest, tail_blk, used_blocks, n_blocks):
    t = dest.shape[0] // TOP_K
    grid_spec = pltpu.PrefetchScalarGridSpec(
        num_scalar_prefetch=2,
        grid=(t // TM,),
        in_specs=[pl.BlockSpec((TM * TOP_K,), lambda i, *_: (i,), memory_space=pltpu.SMEM),
                  pl.BlockSpec(memory_space=pl.ANY)],
        out_specs=pl.BlockSpec(memory_space=pl.ANY),
        scratch_shapes=[pltpu.VMEM((TE, SUBLANES, LANES), F32),
                        pltpu.VMEM((3, TM, SUBLANES, LANES), F32),
                        pltpu.SemaphoreType.DMA((2,)), pltpu.SemaphoreType.DMA((2,)),
                        pltpu.SemaphoreType.DMA((3,))],
    )
    return pl.pallas_call(
        _dispatch_kernel,
        grid_spec=grid_spec,
        out_shape=jax.ShapeDtypeStruct((n_blocks * TE, SUBLANES, LANES), F32),
        compiler_params=_cparams(("arbitrary",)),
        name="moe_dispatch",
    )(tail_blk, used_blocks, dest, f)


def _expert_kernel(layer, exp_ref, used_ref, nxt_ref,
                   xs_hbm, wgu_hbm, bgu_ref, wd_hbm, bd_ref, yb_hbm,
                   xbuf, ybuf, gu_stage, d_stage, wgu_b, wd_b, wsem, xsem, ysem):
    w = pl.program_id(0)
    n_blocks = pl.num_programs(0)
    te = xbuf.shape[1]
    de = wd_b.shape[0]
    slot = w % 2
    new_expert = (w == 0) | (exp_ref[w] != exp_ref[jnp.maximum(w - 1, 0)])

    def x_copies(blk, sl):
        rows = pl.ds(pl.multiple_of(blk * te, te), te)
        return [pltpu.make_async_copy(xs_hbm.at[rows, s], xbuf.at[sl, :, pl.ds(s * LANES, LANES)],
                                      xsem.at[sl]) for s in range(SUBLANES)]

    def y_copies(blk, sl):
        rows = pl.ds(pl.multiple_of(blk * te, te), te)
        return [pltpu.make_async_copy(ybuf.at[sl, :, pl.ds(s * LANES, LANES)], yb_hbm.at[rows, s],
                                      ysem.at[sl]) for s in range(SUBLANES)]

    def weight_copies(e):
        return (pltpu.make_async_copy(wgu_hbm.at[layer, e], gu_stage, wsem.at[0]),
                pltpu.make_async_copy(wd_hbm.at[layer, e], d_stage, wsem.at[1]))

    @pl.when(w == 0)
    def _():
        for c in x_copies(0, 0):
            c.start()
        for c in weight_copies(exp_ref[0]):
            c.start()

    @pl.when(w + 1 < n_blocks)
    def _():
        for c in x_copies(w + 1, 1 - slot):
            c.start()

    @pl.when(new_expert)
    def _():
        for c in weight_copies(exp_ref[w]):
            c.wait()
        wgu_b[...] = gu_stage[...].astype(BF16)
        wd_b[...] = d_stage[...].astype(BF16)

        @pl.when(nxt_ref[w] >= 0)
        def _():
            for c in weight_copies(nxt_ref[w]):
                c.start()

    for c in x_copies(w, slot):
        c.wait()

    @pl.when(w >= 2)
    def _():
        for c in y_copies(w - 2, slot):
            c.wait()

    @pl.when(used_ref[w] == 0)
    def _():
        ybuf[slot] = jnp.zeros(ybuf.shape[1:], F32)

    @pl.when(used_ref[w] != 0)
    def _():
        x = xbuf[slot].astype(BF16)
        gu = jnp.dot(x, wgu_b[...], preferred_element_type=F32) + bgu_ref[0]
        g = jnp.minimum(gu[:, :de], SWIGLU_LIMIT)
        u = jnp.clip(gu[:, de:], -SWIGLU_LIMIT, SWIGLU_LIMIT)
        act = g * _sigmoid(SWIGLU_ALPHA * g) * (u + 1.0)
        ybuf[slot] = jnp.dot(act.astype(BF16), wd_b[...], preferred_element_type=F32) + bd_ref[0]

    for c in y_copies(w, slot):
        c.start()

    @pl.when(w == n_blocks - 1)
    def _():
        for c in y_copies(w, slot):
            c.wait()

        @pl.when(w >= 1)
        def _():
            for c in y_copies(w - 1, 1 - slot):
                c.wait()


def _experts(xs, work, layer, w_gu, b_gu, w_down, b_down):
    _, n_exp, d, de2 = w_gu.shape
    de = w_down.shape[2]
    n_blocks = work[0].shape[0]
    bias = lambda w, ex, used, nxt: (ex[w], 0, 0)
    grid_spec = pltpu.PrefetchScalarGridSpec(
        num_scalar_prefetch=3,
        grid=(n_blocks,),
        in_specs=[
            pl.BlockSpec(memory_space=pl.ANY),
            pl.BlockSpec(memory_space=pl.ANY),
            pl.BlockSpec((1, 1, de2), bias),
            pl.BlockSpec(memory_space=pl.ANY),
            pl.BlockSpec((1, 1, d), bias),
        ],
        out_specs=pl.BlockSpec(memory_space=pl.ANY),
        scratch_shapes=[pltpu.VMEM((2, TE, d), F32), pltpu.VMEM((2, TE, d), F32),
                        pltpu.VMEM((d, de2), F32), pltpu.VMEM((de, d), F32),
                        pltpu.VMEM((d, de2), BF16), pltpu.VMEM((de, d), BF16),
                        pltpu.SemaphoreType.DMA((2,)), pltpu.SemaphoreType.DMA((2,)),
                        pltpu.SemaphoreType.DMA((2,))],
    )
    return pl.pallas_call(
        functools.partial(_expert_kernel, layer),
        grid_spec=grid_spec,
        out_shape=jax.ShapeDtypeStruct(xs.shape, F32),
        compiler_params=_cparams(("arbitrary",)),
        name="moe_experts",
    )(*work, xs, w_gu, b_gu[layer].reshape(n_exp, 1, de2), w_down, b_down[layer].reshape(n_exp, 1, d))


def _combine_kernel(row_of_tile, dest_ref, next_ref, gate_ref, x_ref, mod_ref, yb_hbm, o_ref, buf, sem):
    i = pl.program_id(0)
    n_tiles = pl.num_programs(0)
    tm, d = x_ref.shape
    half = i % 2

    def fetch(idx_ref, h):
        def start(t, c):
            slot = pl.ds(pl.multiple_of(t * SUBLANES, SUBLANES), SUBLANES)
            for k in range(TOP_K):
                row = pl.ds(pl.multiple_of(idx_ref[k * tm + t] * SUBLANES, SUBLANES), SUBLANES)
                pltpu.make_async_copy(yb_hbm.at[row], buf.at[h, k, slot], sem.at[h]).start(priority=k % 2)
            return c

        lax.fori_loop(0, tm, start, 0, unroll=ROW_DMA_UNROLL)

    @pl.when(i == 0)
    def _():
        fetch(dest_ref, 0)

    @pl.when(i + 1 < n_tiles)
    def _():
        fetch(next_ref, 1 - half)

    g2 = mod_ref[pl.ds(row_of_tile(i), 1), pl.ds(5 * d, d)]
    gates = jnp.concatenate([gate_ref[...], jnp.zeros((LANES - SUBLANES, tm), F32)], axis=0).T
    for k in range(TOP_K):
        pltpu.make_async_copy(yb_hbm.at[pl.ds(0, tm * SUBLANES)], buf.at[half, k], sem.at[half]).wait()
    y = _load_tile_rows(buf.at[half, 0], tm) * gates[:, 0:1]
    for k in range(1, TOP_K):
        y = y + _load_tile_rows(buf.at[half, k], tm) * gates[:, k:k + 1]
    o_ref[...] = x_ref[...] + g2 * y


def _combine(x_new, yb, dest, gates, mod, row_of_tile):
    t, d = x_new.shape
    last = t // TM - 1
    return pl.pallas_call(
        functools.partial(_combine_kernel, row_of_tile),
        grid=(t // TM,),
        in_specs=[
            pl.BlockSpec((TM * TOP_K,), lambda i: (i,), memory_space=pltpu.SMEM),
            pl.BlockSpec((TM * TOP_K,), lambda i: (jnp.minimum(i + 1, last),), memory_space=pltpu.SMEM),
            pl.BlockSpec((SUBLANES, TM), lambda i: (0, i)),
            pl.BlockSpec((TM, d), lambda i: (i, 0)),
            _const_spec((MOD_ROWS, N_MOD * d)),
            pl.BlockSpec(memory_space=pl.ANY),
        ],
        out_specs=pl.BlockSpec((TM, d), lambda i: (i, 0)),
        out_shape=jax.ShapeDtypeStruct((t, d), F32),
        scratch_shapes=[pltpu.VMEM((2, TOP_K, TM * SUBLANES, LANES), F32),
                        pltpu.SemaphoreType.DMA((2,))],
        compiler_params=_cparams(("arbitrary",)),
        name="moe_combine",
    )(dest, dest, gates, x_new, mod, yb)


def _expert_blocks(counts, n_assign):
    n_exp = counts.shape[0]
    n_blocks = n_assign // TE + n_exp
    padded = (counts + TE - 1) // TE * TE
    pad_end = jnp.cumsum(padded)
    pad_start = (pad_end - padded).astype(jnp.int32)
    used_blocks = pad_end[-1] // TE
    w = jnp.arange(n_blocks, dtype=jnp.int32)
    used = w < used_blocks
    ex = jnp.minimum(jnp.sum(w[:, None] * TE >= pad_end[None, :], axis=1), n_exp - 1).astype(jnp.int32)
    ex = jnp.where(used, ex, jnp.max(jnp.where(used, ex, 0)))
    ids = jnp.arange(n_exp, dtype=jnp.int32)
    later = jnp.where((ids[None, :] > ids[:, None]) & (counts[None, :] > 0), ids[None, :], n_exp)
    nxt_of = jnp.min(later, axis=1)
    nxt = jnp.sum(jnp.where(ex[:, None] == ids[None, :], nxt_of[None, :], 0), axis=1)
    nxt = jnp.where(nxt < n_exp, nxt, -1).astype(jnp.int32)
    tail_blk = jnp.where(counts > 0, pad_end // TE - 1, -1).astype(jnp.int32)
    work = (ex, used.astype(jnp.int32), nxt)
    return work, pad_start, tail_blk, used_blocks.astype(jnp.int32).reshape(1), n_blocks


def _moe(x_new, f, idx, gates, rank, cnt, mod, row_of_tile, layer, w_gu, b_gu, w_down, b_down):
    t = idx.shape[1]
    work, pad_start, tail_blk, used_blocks, n_blocks = _expert_blocks(cnt[:, 0], t * TOP_K)
    ids = jnp.arange(pad_start.shape[0], dtype=jnp.int32)
    dest = rank + jnp.sum(jnp.where(idx[..., None] == ids, pad_start, 0), axis=-1)
    dest = dest.reshape(TOP_K, t // TM, TM).transpose(1, 0, 2).reshape(-1)
    xs = _dispatch(f, dest, tail_blk, used_blocks, n_blocks)
    yb = _experts(xs, work, layer, w_gu, b_gu, w_down, b_down)
    return _combine(x_new, yb.reshape(-1, LANES), dest, gates, mod, row_of_tile)


def kernel(x, c, ctx, c_ctx, ada_w, ada_b, norm_mix, norm_ffn, gm_w_in, gm_b_in, gm_v_gain, gm_w_s,
           gm_b_s, gm_w_out, at_w_qkv, at_q_gain, at_k_gain, at_w_o, moe_router_w, moe_router_b,
           moe_w_gu, moe_b_gu, moe_w_down, moe_b_down):
    n_samples, n_lat, d = x.shape
    n_ctx = ctx.shape[1]
    assert d == SUBLANES * LANES, "MoE row movement assumes one f32 tile per token row"
    assert n_lat % TM == 0 and n_ctx % TM == 0 and n_samples < MOD_ROWS
    assert n_lat % TG == 0 and (n_samples * n_ctx) % TG == 0 and TG % CHUNK == 0
    assert n_lat % TA == 0 and (n_samples * n_ctx) % TA == 0
    assert n_lat % TQ == 0 and (n_samples * n_ctx) % TQ == 0
    assert (n_samples * n_ctx) % n_lat == 0, "latent K/V blocks are addressed in n_lat-row blocks"
    assert (n_samples * n_lat * TOP_K) % TE == 0 and (n_samples * n_ctx * TOP_K) % TE == 0
    n_ctx_tiles = n_samples * n_ctx // TM
    tps = n_lat // TM

    mods = _ada_table(c, c_ctx, ada_w, ada_b)

    def row_all(i):
        return jnp.where(i < n_ctx_tiles, n_samples, (i - n_ctx_tiles) // tps)

    x_new, f, idx, gates, rank, cnt = _gmlp_layer(
        ctx.reshape(-1, d), x.reshape(-1, d), mods[0], n_samples * n_ctx // TG, n_lat // TG, n_samples, norm_mix[0], gm_w_in[0], gm_b_in[0],
        gm_v_gain[0], gm_w_s[0], gm_b_s[0], gm_w_out[0], norm_ffn[0], moe_router_w[0],
        moe_router_b[0])
    x_all = _moe(x_new, f, idx, gates, rank, cnt, mods[0], row_all,
                 0, moe_w_gu, moe_b_gu, moe_w_down, moe_b_down)

    q, k_all, v_all = _qkv_layer(x_all, mods[1], n_samples * n_ctx // TQ, n_lat // TQ, n_samples, n_lat,
                                 norm_mix[1], at_w_qkv[0], at_q_gain[0], at_k_gain[0])
    x_new, f, idx, gates, rank, cnt = _attn_layer(
        x_all, q, k_all, v_all, mods[1], n_samples * n_ctx // TA, n_lat // TA, n_samples, n_ctx, at_w_o[0],
        norm_ffn[1],
        moe_router_w[1], moe_router_b[1])
    out = _moe(x_new, f, idx, gates, rank, cnt, mods[1], lambda i: i // tps,
               1, moe_w_gu, moe_b_gu, moe_w_down, moe_b_down)
    return out.reshape(n_samples, n_lat, d)
```

```python
import functools

import jax
import jax.numpy as jnp
from jax import lax
from jax.experimental import pallas as pl
from jax.experimental.pallas import tpu as pltpu

F32 = jnp.float32
BF16 = jnp.bfloat16

GRID_W = 64
N_MOD = 6
NORM_EPS = 1e-6
CHUNK = 128
GM_GROUPS = 8
HEAD_DIM = 128
N_KV_HEADS = 2
AXIS_DIM = HEAD_DIM // 2
ROPE_THETA = 10000.0
TOP_K = 4
SWIGLU_LIMIT = 7.0
SWIGLU_ALPHA = 1.702

LANES = 128
SUBLANES = 8
V7X_VMEM_LIMIT = 56 * 1024 * 1024
LOG2_E = 1.4426950408889634

TM = 256
TQ = 256
TG = 512
TA = 512
TE = 512
ROW_DMA_UNROLL = 8
ATTN_HEADS_PER_DOT = 2
MOD_ROWS = 16
ADA_COLS = 1536


def _cparams(sem, vmem=V7X_VMEM_LIMIT):
    return pltpu.CompilerParams(dimension_semantics=sem, vmem_limit_bytes=vmem)


def _const_spec(shape):
    nd = len(shape)
    return pl.BlockSpec(shape, lambda *_: (0,) * nd, pipeline_mode=pl.Buffered(1))


def _load_tile_rows(ref, n):
    return jnp.concatenate([ref[pl.ds(s, n, stride=SUBLANES), :] for s in range(SUBLANES)], axis=1)


def _sigmoid(x):
    return 1.0 / (1.0 + jnp.exp(-x))


def _rms(x):
    return x * lax.rsqrt(jnp.mean(x * x, axis=-1, keepdims=True) + NORM_EPS)


def _ada_kernel(s_ref, w_ref, b_ref, o_ref):
    s = s_ref[...]
    s = s * _sigmoid(s)
    w = w_ref[0]
    s_hi = s.astype(BF16)
    s_lo = (s - s_hi.astype(F32)).astype(BF16)
    w_hi = w.astype(BF16)
    w_lo = (w - w_hi.astype(F32)).astype(BF16)
    acc = jnp.dot(jnp.concatenate([s_hi, s_lo], axis=0), w_hi, preferred_element_type=F32)
    acc = acc[:MOD_ROWS] + acc[MOD_ROWS:] + jnp.dot(s_hi, w_lo, preferred_element_type=F32)
    o_ref[0] = acc + b_ref[0]


def _ada_table(c, c_ctx, ada_w, ada_b):
    depth, d, n = ada_w.shape
    b = c.shape[0]
    s = jnp.concatenate([c, c_ctx[None, :], jnp.zeros((MOD_ROWS - b - 1, d), F32)], axis=0)
    tn = ADA_COLS
    return pl.pallas_call(
        _ada_kernel,
        grid=(depth, n // tn),
        in_specs=[
            pl.BlockSpec((MOD_ROWS, d), lambda i, j: (0, 0)),
            pl.BlockSpec((1, d, tn), lambda i, j: (i, 0, j)),
            pl.BlockSpec((1, 1, tn), lambda i, j: (i, 0, j)),
        ],
        out_specs=pl.BlockSpec((1, MOD_ROWS, tn), lambda i, j: (i, 0, j)),
        out_shape=jax.ShapeDtypeStruct((depth, MOD_ROWS, n), F32),
        compiler_params=_cparams(("arbitrary", "arbitrary")),
        name="ada_table",
    )(s, ada_w, ada_b.reshape(depth, 1, n))


def _mod_slices(mod_ref, row, d, first):
    return [mod_ref[pl.ds(row, 1), pl.ds((first + k) * d, d)] for k in range(3)]


def _router_epilogue(step, x_new, sh2, sc2, nf_ref, rwt_ref, rb_ref,
                     f_ref, idx_ref, gate_ref, rank_ref, cnt_ref, base_ref):
    tm = x_new.shape[0]
    n_exp = rwt_ref.shape[0] // 2
    f = _rms(x_new) * nf_ref[...] * (1.0 + sc2) + sh2
    f_ref[...] = f

    nt = (((1,), (1,)), ((), ()))
    f_hi = f.astype(BF16)
    f_lo = (f - f_hi.astype(F32)).astype(BF16)
    l_hi = lax.dot_general(rwt_ref[...], f_hi, nt, preferred_element_type=F32)
    l_lo = lax.dot_general(rwt_ref[:n_exp, :], f_lo, nt, preferred_element_type=F32)
    logits = l_hi[:n_exp, :] + l_hi[n_exp:, :] + l_lo + rb_ref[...]
    eid = lax.broadcasted_iota(jnp.int32, (n_exp, tm), 0).astype(F32)

    @pl.when(step == 0)
    def _():
        base_ref[...] = jnp.zeros_like(base_ref)

    r_io = lax.broadcasted_iota(jnp.int32, (tm, tm), 0)
    c_io = lax.broadcasted_iota(jnp.int32, (tm, tm), 1)
    before = jnp.where(r_io < c_io, 1.0, 0.0).astype(BF16)
    ones = jnp.ones((tm, tm), BF16)

    vals, idxs, hits = [], [], []
    l = logits
    for _ in range(TOP_K):
        m = jnp.max(l, axis=0, keepdims=True)
        sel = jnp.min(jnp.where(l == m, eid, float(n_exp)), axis=0, keepdims=True)
        hit = eid == sel
        l = jnp.where(hit, -jnp.inf, l)
        vals.append(m)
        idxs.append(sel)
        hits.append(hit)
    onehot = jnp.concatenate([jnp.where(h, 1.0, 0.0) for h in hits], axis=0).astype(BF16)
    prefix = jnp.dot(onehot, before, preferred_element_type=F32)
    count = jnp.dot(onehot, ones, preferred_element_type=F32)
    base = base_ref[...]
    ranks = []
    for k, hit in enumerate(hits):
        pk = prefix[k * n_exp:(k + 1) * n_exp, :]
        ranks.append(jnp.sum(jnp.where(hit, base + pk, 0.0), axis=0, keepdims=True))
        base = base + count[k * n_exp:(k + 1) * n_exp, :]
    base_ref[...] = base
    es = [jnp.exp(v - vals[0]) for v in vals]
    tot = es[0] + es[1] + es[2] + es[3]
    zero = jnp.zeros_like(tot)
    gate_ref[...] = jnp.concatenate([e / tot for e in es] + [zero] * (SUBLANES - TOP_K), axis=0)
    idx_ref[...] = jnp.concatenate(idxs, axis=0).astype(jnp.int32)
    rank_ref[...] = jnp.concatenate(ranks, axis=0).astype(jnp.int32)
    cnt_ref[...] = base_ref[:, :LANES].astype(jnp.int32)


def _router_weights(rw):
    hi = rw.T.astype(BF16)
    lo = (rw.T - hi.astype(F32)).astype(BF16)
    return jnp.concatenate([hi, lo], axis=0)


def _router_out(t, d, n_exp):
    shapes = (
        jax.ShapeDtypeStruct((t, d), F32),
        jax.ShapeDtypeStruct((TOP_K, t), jnp.int32),
        jax.ShapeDtypeStruct((SUBLANES, t), F32),
        jax.ShapeDtypeStruct((TOP_K, t), jnp.int32),
        jax.ShapeDtypeStruct((n_exp, LANES), jnp.int32),
    )
    return shapes


def _router_out_specs(tile_of, tm, d, n_exp):
    return (
        pl.BlockSpec((tm, d), lambda *g: (tile_of(*g), 0)),
        pl.BlockSpec((TOP_K, tm), lambda *g: (0, tile_of(*g))),
        pl.BlockSpec((SUBLANES, tm), lambda *g: (0, tile_of(*g))),
        pl.BlockSpec((TOP_K, tm), lambda *g: (0, tile_of(*g))),
        pl.BlockSpec((n_exp, LANES), lambda *g: (0, 0)),
    )


def _gmlp_kernel(n_ctx_tiles, tiles_per_sample, n_samples,
                 c_ref, x_ref, mod_ref, nm_ref, win_ref, bin_ref, vg_ref, ws_ref, bs_ref, wout_ref,
                 nf_ref, rwt_ref, rb_ref,
                 xo_ref, f_ref, idx_ref, gate_ref, rank_ref, cnt_ref, base_ref):
    i = pl.program_id(0)
    d = x_ref.shape[1]
    gw = wout_ref.shape[0]
    gc = gw // GM_GROUPS
    row = jnp.where(i < n_ctx_tiles, n_samples, (i - n_ctx_tiles) // tiles_per_sample)
    sh1, sc1, g1 = _mod_slices(mod_ref, row, d, 0)
    sh2, sc2, _ = _mod_slices(mod_ref, row, d, 3)

    x = jnp.where(i < n_ctx_tiles, c_ref[...], x_ref[...])
    h = _rms(x) * nm_ref[...] * (1.0 + sc1) + sh1
    hb = h.astype(BF16)

    def proj(col):
        a = jnp.dot(hb, win_ref[:, col:col + gc], preferred_element_type=F32) + bin_ref[:, col:col + gc]
        return 0.5 * a * (1.0 + lax.erf(a * (2.0 ** -0.5)))

    u = [proj(g * gc) for g in range(GM_GROUPS)]
    v = [proj(gw + g * gc) for g in range(GM_GROUPS)]
    ssq = v[0] * v[0]
    for g in range(1, GM_GROUPS):
        ssq = ssq + v[g] * v[g]
    inv = lax.rsqrt(jnp.sum(ssq, axis=-1, keepdims=True) * (1.0 / gw) + NORM_EPS)
    cols = []
    for g in range(GM_GROUPS):
        vn = (v[g] * inv * vg_ref[:, g * gc:(g + 1) * gc]).astype(BF16)
        s = jnp.concatenate(
            [jnp.dot(ws_ref[g], vn[c * CHUNK:(c + 1) * CHUNK, :], preferred_element_type=F32)
             + bs_ref[:, g * gc:(g + 1) * gc] for c in range(x.shape[0] // CHUNK)], axis=0)
        cols.append((u[g] * s).astype(BF16))
    z = jnp.concatenate(cols, axis=1)
    y = jnp.dot(z, wout_ref[...], preferred_element_type=F32)
    x_new = x + g1 * y
    xo_ref[...] = x_new
    _router_epilogue(i, x_new, sh2, sc2, nf_ref, rwt_ref, rb_ref,
                     f_ref, idx_ref, gate_ref, rank_ref, cnt_ref, base_ref)


def _gmlp_layer(ctx2d, x2d, mod, n_ctx_tiles, tiles_per_sample, n_samples,
                nm, w_in, b_in, v_gain, w_s, b_s, w_out, nf, rw, rb):
    d = x2d.shape[1]
    t = ctx2d.shape[0] + x2d.shape[0]
    gw = w_out.shape[0]
    n_exp = rw.shape[1]
    gc = gw // GM_GROUPS
    bs_full = jnp.repeat(b_s.T, gc, axis=1)
    tile = lambda i: i
    outs = pl.pallas_call(
        functools.partial(_gmlp_kernel, n_ctx_tiles, tiles_per_sample, n_samples),
        grid=(t // TG,),
        in_specs=[
            pl.BlockSpec((TG, d), lambda i: (jnp.minimum(i, n_ctx_tiles - 1), 0)),
            pl.BlockSpec((TG, d), lambda i: (jnp.maximum(i - n_ctx_tiles, 0), 0)),
            _const_spec((MOD_ROWS, N_MOD * d)),
            _const_spec((1, d)),
            _const_spec((d, 2 * gw)),
            _const_spec((1, 2 * gw)),
            _const_spec((1, gw)),
            _const_spec((GM_GROUPS, CHUNK, CHUNK)),
            _const_spec((CHUNK, gw)),
            _const_spec((gw, d)),
            _const_spec((1, d)),
            _const_spec((2 * n_exp, d)),
            _const_spec((n_exp, 1)),
        ],
        out_specs=(pl.BlockSpec((TG, d), lambda i: (i, 0)),) + _router_out_specs(tile, TG, d, n_exp),
        out_shape=(jax.ShapeDtypeStruct((t, d), F32),) + _router_out(t, d, n_exp),
        scratch_shapes=[pltpu.VMEM((n_exp, TG), F32)],
        compiler_params=_cparams(("arbitrary",)),
        name="gmlp_mixer",
    )(ctx2d, x2d, mod, nm.reshape(1, d), w_in.astype(BF16), b_in.reshape(1, -1), v_gain.reshape(1, gw),
      w_s.astype(BF16), bs_full, w_out.astype(BF16), nf.reshape(1, d), _router_weights(rw), rb.reshape(n_exp, 1))
    return outs


def _qkv_kernel(n_ctx_tiles, tiles_per_sample, n_samples,
                x_ref, mod_ref, nm_ref, w_ref, qg_ref, kg_ref, cos_ref, sin_ref,
                q_ref, k_ref, v_ref):
    i = pl.program_id(0)
    d = x_ref.shape[1]
    nq = q_ref.shape[1]
    nkv = k_ref.shape[1]
    row = jnp.where(i < n_ctx_tiles, n_samples, (i - n_ctx_tiles) // tiles_per_sample)
    sh1, sc1, _ = _mod_slices(mod_ref, row, d, 0)
    h = _rms(x_ref[...]) * nm_ref[...] * (1.0 + sc1) + sh1
    qkv = jnp.dot(h.astype(BF16), w_ref[...], preferred_element_type=F32)
    cos = cos_ref[...]
    sin = sin_ref[...]
    half = AXIS_DIM // 2
    lane = lax.broadcasted_iota(jnp.int32, (x_ref.shape[0], HEAD_DIM), 1)
    first_half = (lane % AXIS_DIM) < half

    def head(xh, gain):
        xh = _rms(xh) * gain
        partner = jnp.where(first_half, pltpu.roll(xh, HEAD_DIM - half, 1), pltpu.roll(xh, half, 1))
        return xh * cos + partner * sin

    q = [head(qkv[:, j * HEAD_DIM:(j + 1) * HEAD_DIM], qg_ref[...]) for j in range(nq // HEAD_DIM)]
    k = [head(qkv[:, nq + j * HEAD_DIM:nq + (j + 1) * HEAD_DIM], kg_ref[...])
         for j in range(nkv // HEAD_DIM)]
    q_ref[...] = jnp.concatenate(q, axis=1).astype(BF16)
    k_ref[...] = jnp.concatenate(k, axis=1).astype(BF16)
    v_ref[...] = qkv[:, nq + nkv:].astype(BF16)


def _rope_tables(n_lat):
    rows = n_lat // GRID_W
    row = jnp.repeat(jnp.arange(rows, dtype=jnp.int32), GRID_W).astype(F32)
    col = jnp.tile(jnp.arange(GRID_W, dtype=jnp.int32), rows).astype(F32)
    inv_freq = 1.0 / (ROPE_THETA ** (jnp.arange(0, AXIS_DIM, 2, dtype=F32) / AXIS_DIM))
    ang_r = row[:, None] * inv_freq
    ang_c = col[:, None] * inv_freq
    cos = jnp.concatenate([jnp.cos(ang_r)] * 2 + [jnp.cos(ang_c)] * 2, axis=1)
    sin = jnp.concatenate([-jnp.sin(ang_r), jnp.sin(ang_r), -jnp.sin(ang_c), jnp.sin(ang_c)], axis=1)
    cos = jnp.concatenate([jnp.ones((TQ, HEAD_DIM), F32), cos], axis=0)
    sin = jnp.concatenate([jnp.zeros((TQ, HEAD_DIM), F32), sin], axis=0)
    return cos, sin


def _qkv_layer(x_all, mod, n_ctx_tiles, tiles_per_sample, n_samples, n_lat,
               nm, w_qkv, q_gain, k_gain):
    t, d = x_all.shape
    nqkv = w_qkv.shape[1]
    nkv = N_KV_HEADS * HEAD_DIM
    nq = nqkv - 2 * nkv
    cos, sin = _rope_tables(n_lat)
    tps = tiles_per_sample

    def pos_block(i):
        return jnp.where(i < n_ctx_tiles, 0, 1 + (i - n_ctx_tiles) % tps)

    return pl.pallas_call(
        functools.partial(_qkv_kernel, n_ctx_tiles, tiles_per_sample, n_samples),
        grid=(t // TQ,),
        in_specs=[
            pl.BlockSpec((TQ, d), lambda i: (i, 0)),
            _const_spec((MOD_ROWS, N_MOD * d)),
            _const_spec((1, d)),
            _const_spec((d, nqkv)),
            _const_spec((1, HEAD_DIM)),
            _const_spec((1, HEAD_DIM)),
            pl.BlockSpec((TQ, HEAD_DIM), lambda i: (pos_block(i), 0)),
            pl.BlockSpec((TQ, HEAD_DIM), lambda i: (pos_block(i), 0)),
        ],
        out_specs=(
            pl.BlockSpec((TQ, nq), lambda i: (i, 0)),
            pl.BlockSpec((TQ, nkv), lambda i: (i, 0)),
            pl.BlockSpec((TQ, nkv), lambda i: (i, 0)),
        ),
        out_shape=(
            jax.ShapeDtypeStruct((t, nq), BF16),
            jax.ShapeDtypeStruct((t, nkv), BF16),
            jax.ShapeDtypeStruct((t, nkv), BF16),
        ),
        compiler_params=_cparams(("arbitrary",)),
        name="qkv_rope",
    )(x_all, mod, nm.reshape(1, d), w_qkv.astype(BF16), q_gain.reshape(1, HEAD_DIM),
      k_gain.reshape(1, HEAD_DIM), cos, sin)


def _attn_kernel(tiles_per_sample,
                 x_ref, q_ref, kc_ref, kl_ref, vc_ref, vl_ref, mod_ref, wo_ref, nf_ref, rwt_ref, rb_ref,
                 xo_ref, f_ref, idx_ref, gate_ref, rank_ref, cnt_ref, base_ref):
    b = pl.program_id(0)
    j = pl.program_id(1)
    d = x_ref.shape[1]
    n_heads = q_ref.shape[1] // HEAD_DIM
    group = n_heads // N_KV_HEADS
    _, _, g1 = _mod_slices(mod_ref, b, d, 0)
    sh2, sc2, _ = _mod_slices(mod_ref, b, d, 3)
    tq = q_ref.shape[0]
    exp2_scale = (HEAD_DIM ** -0.5) * LOG2_E
    outs = []
    hpd = ATTN_HEADS_PER_DOT
    for h0 in range(0, n_heads, hpd):
        g = h0 // group
        qg = jnp.concatenate([q_ref[:, h * HEAD_DIM:(h + 1) * HEAD_DIM]
                              for h in range(h0, h0 + hpd)], axis=0)
        cols = slice(g * HEAD_DIM, (g + 1) * HEAD_DIM)
        kg = jnp.concatenate([kc_ref[:, cols], kl_ref[:, cols]], axis=0)
        vg = jnp.concatenate([vc_ref[:, cols], vl_ref[:, cols]], axis=0)
        s = lax.dot_general(qg, kg, (((1,), (1,)), ((), ())), preferred_element_type=F32)
        p = jnp.exp2((s - jnp.max(s, axis=-1, keepdims=True)) * exp2_scale)
        den = jnp.sum(p, axis=-1, keepdims=True)
        pb = p.astype(BF16)
        for h in range(hpd):
            rows = slice(h * tq, (h + 1) * tq)
            o = jnp.dot(pb[rows, :], vg, preferred_element_type=F32) / den[rows, :]
            outs.append(o.astype(BF16))
    o_all = jnp.concatenate(outs, axis=1)
    y = jnp.dot(o_all, wo_ref[...], preferred_element_type=F32)
    x_new = x_ref[...] + g1 * y
    xo_ref[...] = x_new
    _router_epilogue(b * tiles_per_sample + j, x_new, sh2, sc2, nf_ref, rwt_ref, rb_ref,
                     f_ref, idx_ref, gate_ref, rank_ref, cnt_ref, base_ref)


def _attn_layer(x_all, q, k_all, v_all, mod, n_ctx_tiles, tiles_per_sample, n_samples, n_ctx,
                w_o, nf, rw, rb):
    d = x_all.shape[1]
    t_lat = n_samples * tiles_per_sample * TA
    n_exp = rw.shape[1]
    nq = q.shape[1]
    nkv = k_all.shape[1]
    n_lat = tiles_per_sample * TA
    lat_block0 = n_samples * n_ctx // n_lat
    ctx_kv = pl.BlockSpec((n_ctx, nkv), lambda b, j: (b, 0))
    lat_kv = pl.BlockSpec((n_lat, nkv), lambda b, j: (lat_block0 + b, 0))
    tps = tiles_per_sample
    lat_tile = lambda b, j: n_ctx_tiles + b * tps + j
    tile = lambda b, j: b * tps + j
    return pl.pallas_call(
        functools.partial(_attn_kernel, tiles_per_sample),
        grid=(n_samples, tps),
        in_specs=[
            pl.BlockSpec((TA, d), lambda b, j: (lat_tile(b, j), 0)),
            pl.BlockSpec((TA, nq), lambda b, j: (lat_tile(b, j), 0)),
            ctx_kv, lat_kv, ctx_kv, lat_kv,
            _const_spec((MOD_ROWS, N_MOD * d)),
            _const_spec((nq, d)),
            _const_spec((1, d)),
            _const_spec((2 * n_exp, d)),
            _const_spec((n_exp, 1)),
        ],
        out_specs=(pl.BlockSpec((TA, d), lambda b, j: (tile(b, j), 0)),)
        + _router_out_specs(tile, TA, d, n_exp),
        out_shape=(jax.ShapeDtypeStruct((t_lat, d), F32),) + _router_out(t_lat, d, n_exp),
        scratch_shapes=[pltpu.VMEM((n_exp, TA), F32)],
        compiler_params=_cparams(("arbitrary", "arbitrary")),
        name="attn_mixer",
    )(x_all, q, k_all, k_all, v_all, v_all, mod, w_o.astype(BF16), nf.reshape(1, d), _router_weights(rw),
      rb.reshape(n_exp, 1))


def _dispatch_kernel(tail_ref, used_ref, dest_ref, f_hbm, xs_hbm, zeros, fbuf, sem, zsem, fsem):
    i = pl.program_id(0)
    n_tiles = pl.num_programs(0)
    tm = dest_ref.shape[0] // TOP_K
    n_exp = tail_ref.shape[0]
    block_rows = zeros.shape[0]
    n_blocks = xs_hbm.shape[0] // block_rows

    def load(tile):
        rows = pl.ds(pl.multiple_of(tile * tm, tm), tm)
        return [pltpu.make_async_copy(f_hbm.at[rows, pl.ds(s * LANES, LANES)], fbuf.at[tile % 3, :, s, :],
                                      fsem.at[tile % 3]) for s in range(SUBLANES)]

    def drain(tile):
        for k in range(TOP_K):
            pltpu.make_async_copy(fbuf.at[0], xs_hbm.at[pl.ds(0, tm)], sem.at[tile % 2]).wait()

    def fill(blk, fsem_):
        start = pl.multiple_of(blk * block_rows, block_rows)
        return pltpu.make_async_copy(zeros, xs_hbm.at[pl.ds(start, block_rows)], fsem_)

    def tails(fn):
        def body(e, c):
            @pl.when(tail_ref[e] >= 0)
            def _():
                fn(fill(tail_ref[e], zsem.at[0]))
            return c

        lax.fori_loop(0, n_exp, body, 0)

    def spares(fn):
        def body(j, c):
            @pl.when(used_ref[0] + j < n_blocks)
            def _():
                fn(fill(used_ref[0] + j, zsem.at[1]))
            return c

        lax.fori_loop(0, n_exp, body, 0)

    @pl.when(i == 0)
    def _():
        for c in load(0):
            c.start()
        zeros[...] = jnp.zeros_like(zeros)
        tails(lambda c: c.start())
        spares(lambda c: c.start())
        tails(lambda c: c.wait())

    @pl.when(i + 1 < n_tiles)
    def _():
        for c in load(i + 1):
            c.start()

    for c in load(i):
        c.wait()
    tile = fbuf.at[i % 3]

    def start(t, c):
        for k in range(TOP_K):
            pltpu.make_async_copy(tile.at[t], xs_hbm.at[dest_ref[k * tm + t]], sem.at[i % 2]).start(priority=k % 2)
        return c

    lax.fori_loop(0, tm, start, 0, unroll=ROW_DMA_UNROLL)

    @pl.when(i >= 1)
    def _():
        drain(i - 1)

    @pl.when(i == n_tiles - 1)
    def _():
        drain(i)
        spares(lambda c: c.wait())


def _dispatch(f, dest, tail_blk, used_blocks, n_blocks):
    t = dest.shape[0] // TOP_K
    grid_spec = pltpu.PrefetchScalarGridSpec(
        num_scalar_prefetch=2,
        grid=(t // TM,),
        in_specs=[pl.BlockSpec((TM * TOP_K,), lambda i, *_: (i,), memory_space=pltpu.SMEM),
                  pl.BlockSpec(memory_space=pl.ANY)],
        out_specs=pl.BlockSpec(memory_space=pl.ANY),
        scratch_shapes=[pltpu.VMEM((TE, SUBLANES, LANES), F32),
                        pltpu.VMEM((3, TM, SUBLANES, LANES), F32),
                        pltpu.SemaphoreType.DMA((2,)), pltpu.SemaphoreType.DMA((2,)),
                        pltpu.SemaphoreType.DMA((3,))],
    )
    return pl.pallas_call(
        _dispatch_kernel,
        grid_spec=grid_spec,
        out_shape=jax.ShapeDtypeStruct((n_blocks * TE, SUBLANES, LANES), F32),
        compiler_params=_cparams(("arbitrary",)),
        name="moe_dispatch",
    )(tail_blk, used_blocks, dest, f)


def _expert_kernel(layer, exp_ref, used_ref, nxt_ref,
                   xs_hbm, wgu_hbm, bgu_ref, wd_hbm, bd_ref, yb_hbm,
                   xbuf, ybuf, gu_stage, d_stage, wgu_b, wd_b, wsem, xsem, ysem):
    w = pl.program_id(0)
    n_blocks = pl.num_programs(0)
    te = xbuf.shape[1]
    de = wd_b.shape[0]
    slot = w % 2
    new_expert = (w == 0) | (exp_ref[w] != exp_ref[jnp.maximum(w - 1, 0)])

    def x_copies(blk, sl):
        rows = pl.ds(pl.multiple_of(blk * te, te), te)
        return [pltpu.make_async_copy(xs_hbm.at[rows, s], xbuf.at[sl, :, pl.ds(s * LANES, LANES)],
                                      xsem.at[sl]) for s in range(SUBLANES)]

    def y_copies(blk, sl):
        rows = pl.ds(pl.multiple_of(blk * te, te), te)
        return [pltpu.make_async_copy(ybuf.at[sl, :, pl.ds(s * LANES, LANES)], yb_hbm.at[rows, s],
                                      ysem.at[sl]) for s in range(SUBLANES)]

    def weight_copies(e):
        return (pltpu.make_async_copy(wgu_hbm.at[layer, e], gu_stage, wsem.at[0]),
                pltpu.make_async_copy(wd_hbm.at[layer, e], d_stage, wsem.at[1]))

    @pl.when(w == 0)
    def _():
        for c in x_copies(0, 0):
            c.start()
        for c in weight_copies(exp_ref[0]):
            c.start()

    @pl.when(w + 1 < n_blocks)
    def _():
        for c in x_copies(w + 1, 1 - slot):
            c.start()

    @pl.when(new_expert)
    def _():
        for c in weight_copies(exp_ref[w]):
            c.wait()
        wgu_b[...] = gu_stage[...].astype(BF16)
        wd_b[...] = d_stage[...].astype(BF16)

        @pl.when(nxt_ref[w] >= 0)
        def _():
            for c in weight_copies(nxt_ref[w]):
                c.start()

    for c in x_copies(w, slot):
        c.wait()

    @pl.when(w >= 2)
    def _():
        for c in y_copies(w - 2, slot):
            c.wait()

    @pl.when(used_ref[w] == 0)
    def _():
        ybuf[slot] = jnp.zeros(ybuf.shape[1:], F32)

    @pl.when(used_ref[w] != 0)
    def _():
        x = xbuf[slot].astype(BF16)
        gu = jnp.dot(x, wgu_b[...], preferred_element_type=F32) + bgu_ref[0]
        g = jnp.minimum(gu[:, :de], SWIGLU_LIMIT)
        u = jnp.clip(gu[:, de:], -SWIGLU_LIMIT, SWIGLU_LIMIT)
        act = g * _sigmoid(SWIGLU_ALPHA * g) * (u + 1.0)
        ybuf[slot] = jnp.dot(act.astype(BF16), wd_b[...], preferred_element_type=F32) + bd_ref[0]

    for c in y_copies(w, slot):
        c.start()

    @pl.when(w == n_blocks - 1)
    def _():
        for c in y_copies(w, slot):
            c.wait()

        @pl.when(w >= 1)
        def _():
            for c in y_copies(w - 1, 1 - slot):
                c.wait()


def _experts(xs, work, layer, w_gu, b_gu, w_down, b_down):
    _, n_exp, d, de2 = w_gu.shape
    de = w_down.shape[2]
    n_blocks = work[0].shape[0]
    bias = lambda w, ex, used, nxt: (ex[w], 0, 0)
    grid_spec = pltpu.PrefetchScalarGridSpec(
        num_scalar_prefetch=3,
        grid=(n_blocks,),
        in_specs=[
            pl.BlockSpec(memory_space=pl.ANY),
            pl.BlockSpec(memory_space=pl.ANY),
            pl.BlockSpec((1, 1, de2), bias),
            pl.BlockSpec(memory_space=pl.ANY),
            pl.BlockSpec((1, 1, d), bias),
        ],
        out_specs=pl.BlockSpec(memory_space=pl.ANY),
        scratch_shapes=[pltpu.VMEM((2, TE, d), F32), pltpu.VMEM((2, TE, d), F32),
                        pltpu.VMEM((d, de2), F32), pltpu.VMEM((de, d), F32),
                        pltpu.VMEM((d, de2), BF16), pltpu.VMEM((de, d), BF16),
                        pltpu.SemaphoreType.DMA((2,)), pltpu.SemaphoreType.DMA((2,)),
                        pltpu.SemaphoreType.DMA((2,))],
    )
    return pl.pallas_call(
        functools.partial(_expert_kernel, layer),
        grid_spec=grid_spec,
        out_shape=jax.ShapeDtypeStruct(xs.shape, F32),
        compiler_params=_cparams(("arbitrary",)),
        name="moe_experts",
    )(*work, xs, w_gu, b_gu[layer].reshape(n_exp, 1, de2), w_down, b_down[layer].reshape(n_exp, 1, d))


def _combine_kernel(row_of_tile, dest_ref, next_ref, gate_ref, x_ref, mod_ref, yb_hbm, o_ref, buf, sem):
    i = pl.program_id(0)
    n_tiles = pl.num_programs(0)
    tm, d = x_ref.shape
    half = i % 2

    def fetch(idx_ref, h):
        def start(t, c):
            slot = pl.ds(pl.multiple_of(t * SUBLANES, SUBLANES), SUBLANES)
            for k in range(TOP_K):
                row = pl.ds(pl.multiple_of(idx_ref[k * tm + t] * SUBLANES, SUBLANES), SUBLANES)
                pltpu.make_async_copy(yb_hbm.at[row], buf.at[h, k, slot], sem.at[h]).start(priority=k % 2)
            return c

        lax.fori_loop(0, tm, start, 0, unroll=ROW_DMA_UNROLL)

    @pl.when(i == 0)
    def _():
        fetch(dest_ref, 0)

    @pl.when(i + 1 < n_tiles)
    def _():
        fetch(next_ref, 1 - half)

    g2 = mod_ref[pl.ds(row_of_tile(i), 1), pl.ds(5 * d, d)]
    gates = jnp.concatenate([gate_ref[...], jnp.zeros((LANES - SUBLANES, tm), F32)], axis=0).T
    for k in range(TOP_K):
        pltpu.make_async_copy(yb_hbm.at[pl.ds(0, tm * SUBLANES)], buf.at[half, k], sem.at[half]).wait()
    y = _load_tile_rows(buf.at[half, 0], tm) * gates[:, 0:1]
    for k in range(1, TOP_K):
        y = y + _load_tile_rows(buf.at[half, k], tm) * gates[:, k:k + 1]
    o_ref[...] = x_ref[...] + g2 * y


def _combine(x_new, yb, dest, gates, mod, row_of_tile):
    t, d = x_new.shape
    last = t // TM - 1
    return pl.pallas_call(
        functools.partial(_combine_kernel, row_of_tile),
        grid=(t // TM,),
        in_specs=[
            pl.BlockSpec((TM * TOP_K,), lambda i: (i,), memory_space=pltpu.SMEM),
            pl.BlockSpec((TM * TOP_K,), lambda i: (jnp.minimum(i + 1, last),), memory_space=pltpu.SMEM),
            pl.BlockSpec((SUBLANES, TM), lambda i: (0, i)),
            pl.BlockSpec((TM, d), lambda i: (i, 0)),
            _const_spec((MOD_ROWS, N_MOD * d)),
            pl.BlockSpec(memory_space=pl.ANY),
        ],
        out_specs=pl.BlockSpec((TM, d), lambda i: (i, 0)),
        out_shape=jax.ShapeDtypeStruct((t, d), F32),
        scratch_shapes=[pltpu.VMEM((2, TOP_K, TM * SUBLANES, LANES), F32),
                        pltpu.SemaphoreType.DMA((2,))],
        compiler_params=_cparams(("arbitrary",)),
        name="moe_combine",
    )(dest, dest, gates, x_new, mod, yb)


def _expert_blocks(counts, n_assign):
    n_exp = counts.shape[0]
    n_blocks = n_assign // TE + n_exp
    padded = (counts + TE - 1) // TE * TE
    pad_end = jnp.cumsum(padded)
    pad_start = (pad_end - padded).astype(jnp.int32)
    used_blocks = pad_end[-1] // TE
    w = jnp.arange(n_blocks, dtype=jnp.int32)
    used = w < used_blocks
    ex = jnp.minimum(jnp.sum(w[:, None] * TE >= pad_end[None, :], axis=1), n_exp - 1).astype(jnp.int32)
    ex = jnp.where(used, ex, jnp.max(jnp.where(used, ex, 0)))
    ids = jnp.arange(n_exp, dtype=jnp.int32)
    later = jnp.where((ids[None, :] > ids[:, None]) & (counts[None, :] > 0), ids[None, :], n_exp)
    nxt_of = jnp.min(later, axis=1)
    nxt = jnp.sum(jnp.where(ex[:, None] == ids[None, :], nxt_of[None, :], 0), axis=1)
    nxt = jnp.where(nxt < n_exp, nxt, -1).astype(jnp.int32)
    tail_blk = jnp.where(counts > 0, pad_end // TE - 1, -1).astype(jnp.int32)
    work = (ex, used.astype(jnp.int32), nxt)
    return work, pad_start, tail_blk, used_blocks.astype(jnp.int32).reshape(1), n_blocks


def _moe(x_new, f, idx, gates, rank, cnt, mod, row_of_tile, layer, w_gu, b_gu, w_down, b_down):
    t = idx.shape[1]
    work, pad_start, tail_blk, used_blocks, n_blocks = _expert_blocks(cnt[:, 0], t * TOP_K)
    ids = jnp.arange(pad_start.shape[0], dtype=jnp.int32)
    dest = rank + jnp.sum(jnp.where(idx[..., None] == ids, pad_start, 0), axis=-1)
    dest = dest.reshape(TOP_K, t // TM, TM).transpose(1, 0, 2).reshape(-1)
    xs = _dispatch(f, dest, tail_blk, used_blocks, n_blocks)
    yb = _experts(xs, work, layer, w_gu, b_gu, w_down, b_down)
    return _combine(x_new, yb.reshape(-1, LANES), dest, gates, mod, row_of_tile)


def kernel(x, c, ctx, c_ctx, ada_w, ada_b, norm_mix, norm_ffn, gm_w_in, gm_b_in, gm_v_gain, gm_w_s,
           gm_b_s, gm_w_out, at_w_qkv, at_q_gain, at_k_gain, at_w_o, moe_router_w, moe_router_b,
           moe_w_gu, moe_b_gu, moe_w_down, moe_b_down):
    n_samples, n_lat, d = x.shape
    n_ctx = ctx.shape[1]
    assert d == SUBLANES * LANES, "MoE row movement assumes one f32 tile per token row"
    assert n_lat % TM == 0 and n_ctx % TM == 0 and n_samples < MOD_ROWS
    assert n_lat % TG == 0 and (n_samples * n_ctx) % TG == 0 and TG % CHUNK == 0
    assert n_lat % TA == 0 and (n_samples * n_ctx) % TA == 0
    assert n_lat % TQ == 0 and (n_samples * n_ctx) % TQ == 0
    assert (n_samples * n_ctx) % n_lat == 0, "latent K/V blocks are addressed in n_lat-row blocks"
    assert (n_samples * n_lat * TOP_K) % TE == 0 and (n_samples * n_ctx * TOP_K) % TE == 0
    n_ctx_tiles = n_samples * n_ctx // TM
    tps = n_lat // TM

    mods = _ada_table(c, c_ctx, ada_w, ada_b)

    def row_all(i):
        return jnp.where(i < n_ctx_tiles, n_samples, (i - n_ctx_tiles) // tps)

    x_new, f, idx, gates, rank, cnt = _gmlp_layer(
        ctx.reshape(-1, d), x.reshape(-1, d), mods[0], n_samples * n_ctx // TG, n_lat // TG, n_samples, norm_mix[0], gm_w_in[0], gm_b_in[0],
        gm_v_gain[0], gm_w_s[0], gm_b_s[0], gm_w_out[0], norm_ffn[0], moe_router_w[0],
        moe_router_b[0])
    x_all = _moe(x_new, f, idx, gates, rank, cnt, mods[0], row_all,
                 0, moe_w_gu, moe_b_gu, moe_w_down, moe_b_down)

    q, k_all, v_all = _qkv_layer(x_all, mods[1], n_samples * n_ctx // TQ, n_lat // TQ, n_samples, n_lat,
                                 norm_mix[1], at_w_qkv[0], at_q_gain[0], at_k_gain[0])
    x_new, f, idx, gates, rank, cnt = _attn_layer(
        x_all, q, k_all, v_all, mods[1], n_samples * n_ctx // TA, n_lat // TA, n_samples, n_ctx, at_w_o[0],
        norm_ffn[1],
        moe_router_w[1], moe_router_b[1])
    out = _moe(x_new, f, idx, gates, rank, cnt, mods[1], lambda i: i // tps,
               1, moe_w_gu, moe_b_gu, moe_w_down, moe_b_down)
    return out.reshape(n_samples, n_lat, d)
```
